```python
import math
import jax, jax.numpy as jnp
from jax import lax
import numpy as np

D_MODEL = 1024
BATCH = 8
SEQ = 4096
DEPTH = 2

LRU_WIDTH = D_MODEL // 4
LRU_BLOCKS = 4
LRU_BLOCK_W = LRU_WIDTH // LRU_BLOCKS
LRU_CONV = 4
LRU_C = 8.0
GLA_HEADS = 4
GLA_DV = (D_MODEL // 4) // GLA_HEADS
GLA_DK = GLA_DV // 2
GLA_GATE_RANK = 16
GLA_TAU = 16.0
GLA_CHUNK = 64
MOBA_HEADS = 8
MOBA_DH = (D_MODEL // 2) // MOBA_HEADS
MOBA_BLOCK = 256
MOBA_TOPK = 3
MOBA_Q_CHUNK = 128
REL_BUCKETS = 32
REL_MAX_DIST = 128
D_FF = -(-8 * D_MODEL // (3 * 256)) * 256

GLA_QK_W = GLA_HEADS * GLA_DK
GLA_V_W = GLA_HEADS * GLA_DV
MOBA_W = MOBA_HEADS * MOBA_DH
D_MIX = LRU_WIDTH + GLA_V_W + MOBA_W
IN_SPLITS = (LRU_WIDTH, LRU_WIDTH, GLA_QK_W, GLA_QK_W, GLA_V_W, GLA_GATE_RANK, GLA_V_W, MOBA_W, MOBA_W, MOBA_W)
D_IN = sum(IN_SPLITS)
RMS_EPS = 1e-6
NEG_INF = -1e30

kernel_name = "hymba_rglru_gla_moba_sandwich"


def rms_norm(x, g):
    xf = x.astype(jnp.float32)
    y = xf * lax.rsqrt(jnp.mean(xf * xf, axis=-1, keepdims=True) + RMS_EPS)
    return (y * g.astype(jnp.float32)).astype(x.dtype)


def split_heads(t, n):
    b, s, w = t.shape
    return t.astype(jnp.float32).reshape(b, s, n, w // n).transpose(0, 2, 1, 3)


def t5_bucket(rel):
    n = jnp.maximum(rel, 0)
    max_exact = REL_BUCKETS // 2
    nf = jnp.maximum(n, 1).astype(jnp.float32)
    large = max_exact + (jnp.log(nf / max_exact) / math.log(REL_MAX_DIST / max_exact)
                         * (REL_BUCKETS - max_exact)).astype(jnp.int32)
    large = jnp.minimum(large, REL_BUCKETS - 1)
    return jnp.where(n < max_exact, n, large)


def rg_lru_branch(xb, gb, conv_w, conv_b, wa, ba, wx, bx, lam):
    b, s, c = xb.shape
    xc = lax.conv_general_dilated(xb, conv_w[:, None, :].astype(xb.dtype), window_strides=(1,),
                                  padding=[(LRU_CONV - 1, 0)],
                                  dimension_numbers=('NWC', 'WIO', 'NWC'),
                                  feature_group_count=c)
    xc = xc.astype(jnp.float32) + conv_b.astype(jnp.float32)
    xg = xc.reshape(b, s, LRU_BLOCKS, LRU_BLOCK_W)
    r = jax.nn.sigmoid(jnp.einsum('bsgi,gij->bsgj', xg, wa.astype(jnp.float32)).reshape(b, s, c)
                       + ba.astype(jnp.float32))
    i = jax.nn.sigmoid(jnp.einsum('bsgi,gij->bsgj', xg, wx.astype(jnp.float32)).reshape(b, s, c)
                       + bx.astype(jnp.float32))
    log_a = -LRU_C * r * jax.nn.softplus(-lam.astype(jnp.float32))
    a = jnp.exp(log_a)
    mult = jnp.sqrt(-jnp.expm1(2.0 * log_a))
    mult = mult.at[:, 0].set(1.0)
    u = mult * (i * xc)

    def combine(left, right):
        a1, b1 = left
        a2, b2 = right
        return a1 * a2, a2 * b1 + b2

    _, h = lax.associative_scan(combine, (a, u), axis=1)
    return h * jax.nn.gelu(gb.astype(jnp.float32))


def gla_chunked(q, k, v, g):
    b, h, s, dk = q.shape
    dv = v.shape[-1]
    c = GLA_CHUNK
    n = s // c
    q = q.reshape(b, h, n, c, dk) * (dk ** -0.5)
    k = k.reshape(b, h, n, c, dk)
    v = v.reshape(b, h, n, c, dv)
    cum = jnp.cumsum(g.reshape(b, h, n, c, dk), axis=3)
    last = cum[:, :, :, -1:, :]
    q_d = q * jnp.exp(cum)
    k_d = k * jnp.exp(-cum)
    causal = jnp.tril(jnp.ones((c, c), dtype=bool))
    attn = jnp.where(causal, jnp.einsum('bhncd,bhned->bhnce', q_d, k_d), 0.0)
    intra = jnp.einsum('bhnce,bhnev->bhncv', attn, v)
    kv = jnp.einsum('bhncd,bhncv->bhndv', k * jnp.exp(last - cum), v)
    decay = jnp.exp(last[:, :, :, 0, :])

    def step(state, inp):
        kv_n, d_n = inp
        return d_n[..., None] * state + kv_n, state

    init = jnp.zeros((b, h, dk, dv), jnp.float32)
    _, s_prev = lax.scan(step, init, (kv.transpose(2, 0, 1, 3, 4), decay.transpose(2, 0, 1, 3)))
    s_prev = s_prev.transpose(1, 2, 0, 3, 4)
    inter = jnp.einsum('bhncd,bhndv->bhncv', q_d, s_prev)
    return (intra + inter).reshape(b, h, s, dv)


def moba_attention(q, k, v, rel_bias):
    b, h, s, d = q.shape
    nb = -(-s // MOBA_BLOCK)
    pad = nb * MOBA_BLOCK - s
    k_blk = jnp.pad(k, ((0, 0), (0, 0), (0, pad), (0, 0))).reshape(b, h, nb, MOBA_BLOCK, d)
    v_blk = jnp.pad(v, ((0, 0), (0, 0), (0, pad), (0, 0))).reshape(b, h, nb, MOBA_BLOCK, d)
    k_mean = jnp.mean(k_blk, axis=3)
    topk = min(MOBA_TOPK, nb)
    scale = d ** -0.5
    bias_h = rel_bias.T.astype(jnp.float32)
    bi = jnp.arange(b)[:, None, None]
    hi = jnp.arange(h)[None, :, None]
    blk_ids = jnp.arange(nb)
    offs = jnp.arange(MOBA_BLOCK)

    def chunk(ci):
        q0 = ci * MOBA_Q_CHUNK
        qc = lax.dynamic_slice_in_dim(q, q0, MOBA_Q_CHUNK, axis=2)
        qpos = q0 + jnp.arange(MOBA_Q_CHUNK)
        n_past = q0 // MOBA_BLOCK
        gate = jnp.einsum('bhqd,bhnd->bhqn', qc, k_mean)
        gate = jnp.where(blk_ids < n_past, gate, -jnp.inf)
        _, sel = lax.top_k(gate, topk)
        k_own = lax.dynamic_index_in_dim(k_blk, n_past, axis=2, keepdims=False)
        v_own = lax.dynamic_index_in_dim(v_blk, n_past, axis=2, keepdims=False)
        rel_own = qpos[:, None] - (n_past * MOBA_BLOCK + offs)[None, :]
        logit_own = (jnp.einsum('bhqd,bhkd->bhqk', qc, k_own) * scale
                     + bias_h[:, t5_bucket(rel_own)])
        logit_own = jnp.where(rel_own >= 0, logit_own, NEG_INF)
        logits = []
        for j in range(topk):
            idx = sel[..., j]
            kj = k_blk[bi, hi, idx]
            rel = qpos[:, None] - (idx[..., None] * MOBA_BLOCK + offs)
            bias = bias_h[hi[..., None], t5_bucket(rel)]
            lj = jnp.einsum('bhqd,bhqkd->bhqk', qc, kj) * scale + bias
            logits.append(jnp.where(j < n_past, lj, NEG_INF))
        logits.append(logit_own)
        p = jax.nn.softmax(jnp.concatenate(logits, axis=-1), axis=-1)
        p = p.reshape(b, h, MOBA_Q_CHUNK, topk + 1, MOBA_BLOCK)
        out = jnp.einsum('bhqk,bhkd->bhqd', p[..., topk, :], v_own)
        for j in range(topk):
            vj = v_blk[bi, hi, sel[..., j]]
            out = out + jnp.einsum('bhqk,bhqkd->bhqd', p[..., j, :], vj)
        return out

    out = lax.map(chunk, jnp.arange(s // MOBA_Q_CHUNK))
    return out.transpose(1, 2, 0, 3, 4).reshape(b, h, s, d)


def hybrid_mixer(h, w_in, conv_w, conv_b, wa, ba, wx, bx, lam, gate_w2, gate_b, gla_gain, rel_bias):
    bsz, s, _ = h.shape
    proj = h @ w_in
    split_points = [int(p) for p in np.cumsum(IN_SPLITS)[:-1]]
    lru_x, lru_g, gq, gk, gv, g_lr, g_out, mq, mk, mv = jnp.split(proj, split_points, axis=-1)
    y_lru = rg_lru_branch(lru_x, lru_g, conv_w, conv_b, wa, ba, wx, bx, lam)
    log_alpha = jax.nn.log_sigmoid((g_lr @ gate_w2 + gate_b).astype(jnp.float32)) / GLA_TAU
    o = gla_chunked(split_heads(gq, GLA_HEADS), split_heads(gk, GLA_HEADS),
                    split_heads(gv, GLA_HEADS), split_heads(log_alpha, GLA_HEADS))
    o = o.transpose(0, 2, 1, 3)
    o = o * lax.rsqrt(jnp.mean(o * o, axis=-1, keepdims=True) + RMS_EPS)
    y_gla = (o.reshape(bsz, s, GLA_V_W) * gla_gain.astype(jnp.float32)
             * jax.nn.silu(g_out.astype(jnp.float32)))
    y_moba = moba_attention(split_heads(mq, MOBA_HEADS), split_heads(mk, MOBA_HEADS),
                            split_heads(mv, MOBA_HEADS), rel_bias)
    y_moba = y_moba.transpose(0, 2, 1, 3).reshape(bsz, s, MOBA_W)
    return jnp.concatenate([y_lru, y_gla, y_moba], axis=-1).astype(h.dtype)


def setup_inputs(seed: int = 0) -> dict:
    key = jax.random.key(seed)
    ks = jax.random.split(key, 24)
    f32 = jnp.float32
    L = DEPTH

    def nrm(k, shape, scale):
        return jax.random.normal(k, shape, f32) * scale

    u = jax.random.uniform(ks[12], (L, LRU_WIDTH), f32, 0.9, 0.999) ** (1.0 / LRU_C)
    return {
        'x': nrm(ks[0], (BATCH, SEQ, D_MODEL), 1.0),
        'pre_mix_norm': 1.0 + nrm(ks[1], (L, D_MODEL), 0.02),
        'post_mix_norm': 1.0 + nrm(ks[2], (L, D_MODEL), 0.02),
        'pre_ffn_norm': 1.0 + nrm(ks[3], (L, D_MODEL), 0.02),
        'post_ffn_norm': 1.0 + nrm(ks[4], (L, D_MODEL), 0.02),
        'w_in': nrm(ks[5], (L, D_MODEL, D_IN), D_MODEL ** -0.5),
        'w_out': nrm(ks[6], (L, D_MIX, D_MODEL), D_MIX ** -0.5),
        'lru_conv_w': nrm(ks[7], (L, LRU_CONV, LRU_WIDTH), LRU_CONV ** -0.5),
        'lru_conv_b': nrm(ks[8], (L, LRU_WIDTH), 0.01),
        'lru_wa': nrm(ks[9], (L, LRU_BLOCKS, LRU_BLOCK_W, LRU_BLOCK_W), LRU_BLOCK_W ** -0.5),
        'lru_ba': nrm(ks[10], (L, LRU_WIDTH), 0.01),
        'lru_wx': nrm(ks[11], (L, LRU_BLOCKS, LRU_BLOCK_W, LRU_BLOCK_W), LRU_BLOCK_W ** -0.5),
        'lru_bx': nrm(ks[13], (L, LRU_WIDTH), 0.01),
        'lru_lambda': jnp.log(u) - jnp.log1p(-u),
        'gla_gate_w2': nrm(ks[14], (L, GLA_GATE_RANK, GLA_QK_W), GLA_GATE_RANK ** -0.5),
        'gla_gate_b': nrm(ks[15], (L, GLA_QK_W), 0.01),
        'gla_norm': 1.0 + nrm(ks[16], (L, GLA_V_W), 0.02),
        'rel_bias': nrm(ks[17], (REL_BUCKETS, MOBA_HEADS), 0.2),
        'w_ffn_gate': nrm(ks[18], (L, D_MODEL, D_FF), D_MODEL ** -0.5),
        'w_ffn_up': nrm(ks[19], (L, D_MODEL, D_FF), D_MODEL ** -0.5),
        'w_ffn_down': nrm(ks[20], (L, D_FF, D_MODEL), D_FF ** -0.5),
    }


def reference(x, pre_mix_norm, post_mix_norm, pre_ffn_norm, post_ffn_norm, w_in, w_out,
              lru_conv_w, lru_conv_b, lru_wa, lru_ba, lru_wx, lru_bx, lru_lambda,
              gla_gate_w2, gla_gate_b, gla_norm, rel_bias, w_ffn_gate, w_ffn_up, w_ffn_down):
    for l in range(DEPTH):
        h = rms_norm(x, pre_mix_norm[l])
        mix = hybrid_mixer(h, w_in[l], lru_conv_w[l], lru_conv_b[l], lru_wa[l], lru_ba[l],
                           lru_wx[l], lru_bx[l], lru_lambda[l], gla_gate_w2[l], gla_gate_b[l],
                           gla_norm[l], rel_bias)
        x = x + rms_norm(mix @ w_out[l], post_mix_norm[l])
        h = rms_norm(x, pre_ffn_norm[l])
        f = (jax.nn.silu(h @ w_ffn_gate[l]) * (h @ w_ffn_up[l])) @ w_ffn_down[l]
        x = x + rms_norm(f, post_ffn_norm[l])
    return x
```

```python
import functools
import math

import numpy as np
import jax
import jax.numpy as jnp
from jax import lax
from jax.experimental import pallas as pl
from jax.experimental.pallas import tpu as pltpu

F32 = jnp.float32
BF16 = jnp.bfloat16

LRU_W = 256
LRU_BLOCKS = 4
LRU_CONV = 4
LRU_C = 8.0
GLA_HEADS = 4
GLA_DV = 64
GLA_DK = 32
GLA_RANK = 16
GLA_TAU = 16.0
GLA_CHUNK = 64
GLA_QK_W = GLA_HEADS * GLA_DK
GLA_V_W = GLA_HEADS * GLA_DV
MOBA_HEADS = 8
MOBA_DH = 64
MOBA_BLOCK = 256
MOBA_TOPK = 3
MOBA_W = MOBA_HEADS * MOBA_DH
REL_BUCKETS = 32
REL_MAX_DIST = 128
RMS_EPS = 1e-6
NEG_INF = -1e30

LANES = 128
SUBLANES = 8
VMEM_LIMIT = 56 * 1024 * 1024

GLA_IN_W = GLA_QK_W * 2 + GLA_V_W * 2 + LANES
HEADS_PER_STEP = LANES // MOBA_DH

NT_DIMS = (((1,), (1,)), ((), ()))
TN_DIMS = (((0,), (0,)), ((), ()))


def _rms(x, g):
    return x * lax.rsqrt(jnp.mean(x * x, axis=-1, keepdims=True) + RMS_EPS) * g


def _params(*sem):
    return pltpu.CompilerParams(dimension_semantics=sem, vmem_limit_bytes=VMEM_LIMIT)


def _const_spec(shape):
    return pl.BlockSpec(shape, lambda *_: (0,) * len(shape))


def _inproj_kernel(x_ref, g_ref, wn_ref, wt_ref, lru_ref, gla_ref, mk_ref, mqv_ref):
    h = _rms(x_ref[0], g_ref[...]).astype(BF16)
    pn = jnp.dot(h, wn_ref[...], preferred_element_type=F32)
    lru_ref[0] = pn[:, :2 * LRU_W]
    gla_ref[0] = pn[:, 2 * LRU_W:2 * LRU_W + GLA_IN_W]
    mk_ref[0] = pn[:, 2 * LRU_W + GLA_IN_W:].astype(BF16)
    pt = lax.dot_general(wt_ref[...], h, NT_DIMS, preferred_element_type=F32)
    mqv_ref[0, :MOBA_W, :] = (pt[:MOBA_W] * (MOBA_DH ** -0.5)).astype(BF16)
    mqv_ref[0, MOBA_W:, :] = pt[MOBA_W:].astype(BF16)


def _inproj(x, g, wn, wt, ts):
    b, s, d = x.shape
    nn = wn.shape[1]
    return pl.pallas_call(
        _inproj_kernel,
        grid=(b, s // ts),
        in_specs=[
            pl.BlockSpec((1, ts, d), lambda i, j: (i, j, 0)),
            _const_spec((1, d)),
            _const_spec((d, nn)),
            _const_spec((2 * MOBA_W, d)),
        ],
        out_specs=[
            pl.BlockSpec((1, ts, 2 * LRU_W), lambda i, j: (i, j, 0)),
            pl.BlockSpec((1, ts, GLA_IN_W), lambda i, j: (i, j, 0)),
            pl.BlockSpec((1, ts, MOBA_W), lambda i, j: (i, j, 0)),
            pl.BlockSpec((1, 2 * MOBA_W, ts), lambda i, j: (i, 0, j)),
        ],
        out_shape=[
            jax.ShapeDtypeStruct((b, s, 2 * LRU_W), F32),
            jax.ShapeDtypeStruct((b, s, GLA_IN_W), F32),
            jax.ShapeDtypeStruct((b, s, MOBA_W), BF16),
            jax.ShapeDtypeStruct((b, 2 * MOBA_W, s), BF16),
        ],
        compiler_params=_params("parallel", "parallel"),
        name="inproj",
    )(x, g, wn, wt)


def _shift_rows(x, prev_tail, k, row8):
    sh = pltpu.roll(x, k, axis=0)
    top = jnp.where(row8 < k, pltpu.roll(prev_tail, k, axis=0), sh[:SUBLANES])
    return jnp.concatenate([top, sh[SUBLANES:]], axis=0)


def _lru_kernel(in_ref, cw_ref, cb_ref, wa_ref, ba_ref, wx_ref, bx_ref, lam_ref, y_ref,
                h_sc, tail_sc):
    j = pl.program_id(1)
    ts = in_ref.shape[1]

    @pl.when(j == 0)
    def _():
        h_sc[...] = jnp.zeros_like(h_sc)
        tail_sc[...] = jnp.zeros_like(tail_sc)

    xb = in_ref[0, :, :LRU_W]
    gb = in_ref[0, :, LRU_W:]
    row = lax.broadcasted_iota(jnp.int32, (ts, LRU_W), 0)
    row8 = lax.broadcasted_iota(jnp.int32, (SUBLANES, LRU_W), 0)
    tail = tail_sc[...]
    xc = xb * cw_ref[LRU_CONV - 1:LRU_CONV, :] + cb_ref[...]
    for k in range(1, LRU_CONV):
        xc = xc + _shift_rows(xb, tail, k, row8) * cw_ref[LRU_CONV - 1 - k:LRU_CONV - k, :]
    tail_sc[...] = xb[ts - SUBLANES:]

    xcb = xc.astype(BF16)
    r = jax.nn.sigmoid(jnp.dot(xcb, wa_ref[...], preferred_element_type=F32) + ba_ref[...])
    i = jax.nn.sigmoid(jnp.dot(xcb, wx_ref[...], preferred_element_type=F32) + bx_ref[...])
    log_a = -LRU_C * r * jax.nn.softplus(-lam_ref[...])
    a = jnp.exp(log_a)
    mult = jnp.sqrt(-jnp.tanh(log_a) * (a * a + 1.0))
    mult = jnp.where(row + j * ts == 0, 1.0, mult)
    u = mult * (i * xc)

    d = 1
    while d < ts:
        keep = row >= d
        a_sh = jnp.where(keep, pltpu.roll(a, d, axis=0), 1.0)
        u_sh = jnp.where(keep, pltpu.roll(u, d, axis=0), 0.0)
        u = a * u_sh + u
        a = a * a_sh
        d *= 2
    h = u + a * h_sc[...]
    h_sc[...] = h[ts - 1:ts]
    y_ref[0] = (h * jax.nn.gelu(gb)).astype(y_ref.dtype)


def _lru(lru_in, cw, cb, wa, ba, wx, bx, lam, ts):
    b, s, _ = lru_in.shape
    return pl.pallas_call(
        _lru_kernel,
        grid=(b, s // ts),
        in_specs=[
            pl.BlockSpec((1, ts, 2 * LRU_W), lambda i, j: (i, j, 0)),
            _const_spec((LRU_CONV, LRU_W)),
            _const_spec((1, LRU_W)),
            _const_spec((LRU_W, LRU_W)),
            _const_spec((1, LRU_W)),
            _const_spec((LRU_W, LRU_W)),
            _const_spec((1, LRU_W)),
            _const_spec((1, LRU_W)),
        ],
        out_specs=pl.BlockSpec((1, ts, LRU_W), lambda i, j: (i, j, 0)),
        out_shape=jax.ShapeDtypeStruct((b, s, LRU_W), BF16),
        scratch_shapes=[pltpu.VMEM((1, LRU_W), F32), pltpu.VMEM((SUBLANES, LRU_W), F32)],
        compiler_params=_params("parallel", "arbitrary"),
        name="rglru",
    )(lru_in, cw, cb, wa, ba, wx, bx, lam)


def _gla_kernel(in_ref, w2_ref, gb_ref, gain_ref, y_ref, st_sc):
    ts = in_ref.shape[1]
    c = GLA_CHUNK

    @pl.when(pl.program_id(1) == 0)
    def _():
        st_sc[...] = jnp.zeros_like(st_sc)

    q = in_ref[0, :, :GLA_QK_W]
    k = in_ref[0, :, GLA_QK_W:2 * GLA_QK_W]
    v = in_ref[0, :, 2 * GLA_QK_W:2 * GLA_QK_W + GLA_V_W]
    g_out = in_ref[0, :, 2 * GLA_QK_W + GLA_V_W:2 * GLA_QK_W + 2 * GLA_V_W]
    g_lr = in_ref[0, :, 2 * GLA_QK_W + 2 * GLA_V_W:]

    gate = jnp.dot(g_lr.astype(BF16), w2_ref[...], preferred_element_type=F32) + gb_ref[...]
    log_alpha = jax.nn.log_sigmoid(gate) / GLA_TAU

    ri = lax.broadcasted_iota(jnp.int32, (ts, ts), 0)
    ci = lax.broadcasted_iota(jnp.int32, (ts, ts), 1)
    tri = jnp.where((ri // c == ci // c) & (ci <= ri), 1.0, 0.0).astype(F32)
    cum = jnp.dot(tri, log_alpha, preferred_element_type=F32, precision=lax.Precision.HIGHEST)

    qk_lane = lax.broadcasted_iota(jnp.int32, (1, GLA_QK_W), 1) // GLA_DK
    v_lane = lax.broadcasted_iota(jnp.int32, (1, GLA_V_W), 1) // GLA_DV
    st_row = lax.broadcasted_iota(jnp.int32, (GLA_V_W, GLA_QK_W), 0) // GLA_DV
    st_col = lax.broadcasted_iota(jnp.int32, (GLA_V_W, GLA_QK_W), 1) // GLA_DK
    same_head = st_row == st_col
    causal = (lax.broadcasted_iota(jnp.int32, (c, c), 1)
              <= lax.broadcasted_iota(jnp.int32, (c, c), 0))

    outs = []
    for n in range(ts // c):
        sl = slice(n * c, (n + 1) * c)
        cum_n = cum[sl]
        last = cum_n[c - 1:c]
        q_d = q[sl] * (GLA_DK ** -0.5) * jnp.exp(cum_n)
        k_d = (k[sl] * jnp.exp(-cum_n)).astype(BF16)
        k_l = (k[sl] * jnp.exp(last - cum_n)).astype(BF16)
        v_n = v[sl]
        st_prev = st_sc[...]
        o_n = lax.dot_general(q_d.astype(BF16), st_prev.astype(BF16), NT_DIMS,
                              preferred_element_type=F32)
        for hh in range(GLA_HEADS):
            q_h = jnp.where(qk_lane == hh, q_d, 0.0).astype(BF16)
            attn = lax.dot_general(q_h, k_d, NT_DIMS, preferred_element_type=F32)
            attn = jnp.where(causal, attn, 0.0).astype(BF16)
            v_h = jnp.where(v_lane == hh, v_n, 0.0).astype(BF16)
            o_n = o_n + jnp.dot(attn, v_h, preferred_element_type=F32)
        kv = lax.dot_general(v_n.astype(BF16), k_l, TN_DIMS, preferred_element_type=F32)
        st_sc[...] = st_prev * jnp.exp(last) + jnp.where(same_head, kv, 0.0)
        outs.append(o_n)
    o = jnp.concatenate(outs, axis=0)

    hr = lax.broadcasted_iota(jnp.int32, (GLA_V_W, GLA_V_W), 0) // GLA_DV
    hc = lax.broadcasted_iota(jnp.int32, (GLA_V_W, GLA_V_W), 1) // GLA_DV
    head_mean = jnp.where(hr == hc, 1.0 / GLA_DV, 0.0).astype(F32)
    ms = jnp.dot(o * o, head_mean, preferred_element_type=F32, precision=lax.Precision.HIGHEST)
    y = o * lax.rsqrt(ms + RMS_EPS) * gain_ref[...] * jax.nn.silu(g_out)
    y_ref[0] = y.astype(y_ref.dtype)


def _gla(gla_in, w2, gb, gain, ts):
    b, s, _ = gla_in.shape
    return pl.pallas_call(
        _gla_kernel,
        grid=(b, s // ts),
        in_specs=[
            pl.BlockSpec((1, ts, GLA_IN_W), lambda i, j: (i, j, 0)),
            _const_spec((LANES, GLA_QK_W)),
            _const_spec((1, GLA_QK_W)),
            _const_spec((1, GLA_V_W)),
        ],
        out_specs=pl.BlockSpec((1, ts, GLA_V_W), lambda i, j: (i, j, 0)),
        out_shape=jax.ShapeDtypeStruct((b, s, GLA_V_W), BF16),
        scratch_shapes=[pltpu.VMEM((GLA_V_W, GLA_QK_W), F32)],
        compiler_params=_params("parallel", "arbitrary"),
        name="gla",
    )(gla_in, w2, gb, gain)


def _t5_bucket_np(rel):
    n = np.maximum(rel, 0)
    max_exact = REL_BUCKETS // 2
    nf = np.maximum(n, 1).astype(np.float32)
    large = max_exact + (np.log(nf / np.float32(max_exact)) / np.float32(math.log(REL_MAX_DIST / max_exact))
                         * np.float32(REL_BUCKETS - max_exact)).astype(np.int32)
    large = np.minimum(large, REL_BUCKETS - 1)
    return np.where(n < max_exact, n, large).astype(np.int32)


FAR_DIST = int(np.argmax(_t5_bucket_np(np.arange(4 * REL_MAX_DIST)) == REL_BUCKETS - 1))
assert np.all(_t5_bucket_np(np.arange(FAR_DIST, 1 << 16)) == REL_BUCKETS - 1)
assert FAR_DIST <= MOBA_BLOCK


def _bias_kernel(bkt_ref, rb_ref, out_ref):
    h = pl.program_id(0)
    for t in range(2):
        bkt = bkt_ref[t]
        acc = jnp.zeros(bkt.shape, F32)
        for bb in range(REL_BUCKETS):
            acc = jnp.where(bkt == bb, rb_ref[bb, h], acc)
        out_ref[0, t] = acc


def _bias_tables(rel_bias):
    ko = np.arange(MOBA_BLOCK)[:, None]
    qo = np.arange(MOBA_BLOCK)[None, :]
    bkt = np.stack([_t5_bucket_np(qo - ko), _t5_bucket_np(qo - ko + MOBA_BLOCK)])
    return pl.pallas_call(
        _bias_kernel,
        grid=(MOBA_HEADS,),
        in_specs=[
            _const_spec((2, MOBA_BLOCK, MOBA_BLOCK)),
            pl.BlockSpec(memory_space=pltpu.SMEM),
        ],
        out_specs=pl.BlockSpec((1, 2, MOBA_BLOCK, MOBA_BLOCK), lambda h: (h, 0, 0, 0)),
        out_shape=jax.ShapeDtypeStruct((MOBA_HEADS, 2, MOBA_BLOCK, MOBA_BLOCK), F32),
        compiler_params=_params("parallel"),
        name="moba_bias",
    )(jnp.asarray(bkt), rel_bias.astype(F32))


def _moba_kernel(q_ref, v_ref, k_ref, bias_ref, far_ref, y_ref,
                 kmean_sc, sel_sc, m_sc, l_sc, acc_sc):
    hp = pl.program_id(1)
    n = pl.program_id(2)
    blk = MOBA_BLOCK
    nb = k_ref.shape[1] // blk

    @pl.when(n == 0)
    def _():
        for jb in range(nb):
            kb = k_ref[0, jb * blk:(jb + 1) * blk, :].astype(F32)
            kmean_sc[jb:jb + 1, :] = jnp.mean(kb, axis=0, keepdims=True)

    q2 = q_ref[0]
    feat = lax.broadcasted_iota(jnp.int32, (LANES, blk), 0) // MOBA_DH
    lane_head = lax.broadcasted_iota(jnp.int32, (nb, LANES), 1) // MOBA_DH
    blk_id = lax.broadcasted_iota(jnp.int32, (nb, blk), 0)
    kpos = lax.broadcasted_iota(jnp.int32, (blk, blk), 0)
    qpos = lax.broadcasted_iota(jnp.int32, (blk, blk), 1)
    q2f = q2.astype(F32)

    def attend(hh, k_blk, v_blk, s_bias, mask, first):
        qz = jnp.where(feat == hh, q2, jnp.zeros_like(q2))
        s = jnp.dot(k_blk, qz, preferred_element_type=F32) + s_bias
        s = jnp.where(mask, s, NEG_INF)
        if first:
            m_new = jnp.max(s, axis=0, keepdims=True)
            p = jnp.exp(s - m_new)
            l_sc[hh] = jnp.sum(p, axis=0, keepdims=True)
            acc_sc[hh] = jnp.dot(v_blk, p.astype(BF16), preferred_element_type=F32)
        else:
            m_old = m_sc[hh]
            m_new = jnp.maximum(m_old, jnp.max(s, axis=0, keepdims=True))
            alpha = jnp.exp(m_old - m_new)
            p = jnp.exp(s - m_new)
            l_sc[hh] = alpha * l_sc[hh] + jnp.sum(p, axis=0, keepdims=True)
            acc_sc[hh] = alpha * acc_sc[hh] + jnp.dot(v_blk, p.astype(BF16),
                                                      preferred_element_type=F32)
        m_sc[hh] = m_new

    own = pl.multiple_of(n * blk, blk)
    for hh in range(HEADS_PER_STEP):
        kmz = jnp.where(lane_head == hh, kmean_sc[...], 0.0)
        g = jnp.dot(kmz, q2f, preferred_element_type=F32, precision=lax.Precision.HIGHEST)
        g = jnp.where(blk_id < n, g, -jnp.inf)
        rank = jnp.zeros((nb, blk), F32)
        for jb in range(nb):
            gj = g[jb:jb + 1, :]
            beats = (gj > g) | ((gj == g) & (blk_id > jb))
            rank = rank + jnp.where(beats, 1.0, 0.0)
        sel_sc[hh] = jnp.where((blk_id < n) & (rank < MOBA_TOPK), 1.0, 0.0)

        v_rows = slice(hh * MOBA_DH, (hh + 1) * MOBA_DH)
        attend(hh, k_ref[0, pl.ds(own, blk), :], v_ref[0, v_rows, pl.ds(own, blk)],
               bias_ref[hh, 0], kpos <= qpos, True)

    @pl.when(n >= 1)
    def _():
        prev = pl.multiple_of((n - 1) * blk, blk)
        for hh in range(HEADS_PER_STEP):
            selm = sel_sc[hh, pl.ds(n - 1, 1), :] > 0.5
            attend(hh, k_ref[0, pl.ds(prev, blk), :],
                   v_ref[0, hh * MOBA_DH:(hh + 1) * MOBA_DH, pl.ds(prev, blk)],
                   bias_ref[hh, 1], selm, False)

    def far_block(jb, carry):
        start = pl.multiple_of(jb * blk, blk)
        for hh in range(HEADS_PER_STEP):
            selm = sel_sc[hh, pl.ds(jb, 1), :] > 0.5
            attend(hh, k_ref[0, pl.ds(start, blk), :],
                   v_ref[0, hh * MOBA_DH:(hh + 1) * MOBA_DH, pl.ds(start, blk)],
                   far_ref[hp * HEADS_PER_STEP + hh], selm, False)
        return carry

    lax.fori_loop(0, n - 1, far_block, 0)

    out_t = jnp.concatenate([acc_sc[hh] / l_sc[hh] for hh in range(HEADS_PER_STEP)], axis=0)
    y_ref[0] = out_t.T.astype(y_ref.dtype)


def _moba(mqv, mk, bias_tab, far_bias):
    b, s, _ = mk.shape
    blk = MOBA_BLOCK
    nb = s // blk
    npair = MOBA_HEADS // HEADS_PER_STEP
    return pl.pallas_call(
        _moba_kernel,
        grid=(b, npair, nb),
        in_specs=[
            pl.BlockSpec((1, LANES, blk), lambda i, p, n: (i, p, n)),
            pl.BlockSpec((1, LANES, s), lambda i, p, n: (i, npair + p, 0)),
            pl.BlockSpec((1, s, LANES), lambda i, p, n: (i, 0, p)),
            pl.BlockSpec((HEADS_PER_STEP, 2, blk, blk), lambda i, p, n: (p, 0, 0, 0)),
            pl.BlockSpec(memory_space=pltpu.SMEM),
        ],
        out_specs=pl.BlockSpec((1, blk, LANES), lambda i, p, n: (i, n, p)),
        out_shape=jax.ShapeDtypeStruct((b, s, MOBA_W), BF16),
        scratch_shapes=[
            pltpu.VMEM((nb, LANES), F32),
            pltpu.VMEM((HEADS_PER_STEP, nb, blk), F32),
            pltpu.VMEM((HEADS_PER_STEP, 1, blk), F32),
            pltpu.VMEM((HEADS_PER_STEP, 1, blk), F32),
            pltpu.VMEM((HEADS_PER_STEP, MOBA_DH, blk), F32),
        ],
        compiler_params=_params("parallel", "parallel", "arbitrary"),
        name="moba",
    )(mqv, mqv, mk, bias_tab, far_bias)


def _outproj_kernel(x_ref, lru_ref, gla_ref, moba_ref, wl_ref, wg_ref, wm_ref, g_ref, o_ref):
    y = jnp.dot(lru_ref[...], wl_ref[...], preferred_element_type=F32)
    y = y + jnp.dot(gla_ref[...], wg_ref[...], preferred_element_type=F32)
    y = y + jnp.dot(moba_ref[...], wm_ref[...], preferred_element_type=F32)
    o_ref[...] = x_ref[...] + _rms(y, g_ref[...])


def _outproj(x, y_lru, y_gla, y_moba, w_out, g, tt):
    t, d = x.shape
    wl = w_out[:LRU_W]
    wg = w_out[LRU_W:LRU_W + GLA_V_W]
    wm = w_out[LRU_W + GLA_V_W:]
    row = lambda w: pl.BlockSpec((tt, w), lambda i: (i, 0))
    return pl.pallas_call(
        _outproj_kernel,
        grid=(t // tt,),
        in_specs=[row(d), row(LRU_W), row(GLA_V_W), row(MOBA_W),
                  _const_spec(wl.shape), _const_spec(wg.shape), _const_spec(wm.shape),
                  _const_spec((1, d))],
        out_specs=row(d),
        out_shape=jax.ShapeDtypeStruct((t, d), F32),
        compiler_params=_params("parallel"),
        name="outproj",
    )(x, y_lru, y_gla, y_moba, wl, wg, wm, g)


def _ffn_kernel(x_ref, gpre_ref, wg_ref, wu_ref, wd_ref, gpost_ref, o_ref):
    x = x_ref[...]
    h = _rms(x, gpre_ref[...]).astype(BF16)
    a = jax.nn.silu(jnp.dot(h, wg_ref[...], preferred_element_type=F32))
    a = (a * jnp.dot(h, wu_ref[...], preferred_element_type=F32)).astype(BF16)
    f = jnp.dot(a, wd_ref[...], preferred_element_type=F32)
    o_ref[...] = x + _rms(f, gpost_ref[...])


def _ffn(x, gpre, wg, wu, wd, gpost, tt):
    t, d = x.shape
    dff = wg.shape[1]
    once = dict(pipeline_mode=pl.Buffered(1))
    return pl.pallas_call(
        _ffn_kernel,
        grid=(t // tt,),
        in_specs=[
            pl.BlockSpec((tt, d), lambda i: (i, 0)),
            _const_spec((1, d)),
            pl.BlockSpec((d, dff), lambda i: (0, 0), **once),
            pl.BlockSpec((d, dff), lambda i: (0, 0), **once),
            pl.BlockSpec((dff, d), lambda i: (0, 0), **once),
            _const_spec((1, d)),
        ],
        out_specs=pl.BlockSpec((tt, d), lambda i: (i, 0)),
        out_shape=jax.ShapeDtypeStruct((t, d), F32),
        compiler_params=_params("parallel"),
        name="ffn",
    )(x, gpre, wg, wu, wd, gpost)


def _block_diag(w):
    g, n, _ = w.shape
    eye = jnp.eye(g, dtype=w.dtype)
    return (eye[:, None, :, None] * w[:, :, None, :]).reshape(g * n, g * n)


def _split_w_in(w_in):
    sizes = (LRU_W, LRU_W, GLA_QK_W, GLA_QK_W, GLA_V_W, GLA_RANK, GLA_V_W, MOBA_W, MOBA_W, MOBA_W)
    offs = np.cumsum((0,) + sizes)
    col = lambda i: w_in[:, offs[i]:offs[i + 1]]
    lru_x, lru_g, gq, gk, gv, g_lr, g_out, mq, mk, mv = (col(i) for i in range(len(sizes)))
    g_lr = jnp.pad(g_lr, ((0, 0), (0, LANES - GLA_RANK)))
    w_nat = jnp.concatenate([lru_x, lru_g, gq, gk, gv, g_out, g_lr, mk], axis=1).astype(BF16)
    w_t = jnp.concatenate([mq, mv], axis=1).T.astype(BF16)
    return w_nat, w_t


def kernel(x, pre_mix_norm, post_mix_norm, pre_ffn_norm, post_ffn_norm, w_in, w_out, lru_conv_w, lru_conv_b, lru_wa, lru_ba, lru_wx, lru_bx, lru_lambda, gla_gate_w2, gla_gate_b, gla_norm, rel_bias, w_ffn_gate, w_ffn_up, w_ffn_down):
    b, s, d = x.shape
    assert s % MOBA_BLOCK == 0 and d == LRU_W + GLA_V_W + MOBA_W
    depth = w_in.shape[0]
    ts_proj = min(512, s)
    ts_seq = min(256, s)
    tt = min(512, b * s)
    row = lambda v: v.reshape(1, -1).astype(F32)

    bias_tab = _bias_tables(rel_bias)
    far_bias = rel_bias[REL_BUCKETS - 1].astype(F32)
    for l in range(depth):
        w_nat, w_t = _split_w_in(w_in[l])
        lru_in, gla_in, mk, mqv = _inproj(x, row(pre_mix_norm[l]), w_nat, w_t, ts_proj)
        y_lru = _lru(lru_in, lru_conv_w[l].astype(F32), row(lru_conv_b[l]),
                     _block_diag(lru_wa[l]).astype(BF16), row(lru_ba[l]),
                     _block_diag(lru_wx[l]).astype(BF16), row(lru_bx[l]),
                     row(lru_lambda[l]), ts_seq)
        w2 = jnp.pad(gla_gate_w2[l], ((0, LANES - GLA_RANK), (0, 0))).astype(BF16)
        y_gla = _gla(gla_in, w2, row(gla_gate_b[l]), row(gla_norm[l]), ts_seq)
        y_moba = _moba(mqv, mk, bias_tab, far_bias)
        x2 = _outproj(x.reshape(b * s, d), y_lru.reshape(b * s, -1), y_gla.reshape(b * s, -1),
                      y_moba.reshape(b * s, -1), w_out[l].astype(BF16), row(post_mix_norm[l]), tt)
        x2 = _ffn(x2, row(pre_ffn_norm[l]), w_ffn_gate[l].astype(BF16), w_ffn_up[l].astype(BF16),
                  w_ffn_down[l].astype(BF16), row(post_ffn_norm[l]), tt)
        x = x2.reshape(b, s, d)
    return x
```

```python
import functools
import math

import numpy as np
import jax
import jax.numpy as jnp
from jax import lax
from jax.experimental import pallas as pl
from jax.experimental.pallas import tpu as pltpu

F32 = jnp.float32
BF16 = jnp.bfloat16

LRU_W = 256
LRU_BLOCKS = 4
LRU_CONV = 4
LRU_C = 8.0
GLA_HEADS = 4
GLA_DV = 64
GLA_DK = 32
GLA_RANK = 16
GLA_TAU = 16.0
GLA_CHUNK = 64
GLA_QK_W = GLA_HEADS * GLA_DK
GLA_V_W = GLA_HEADS * GLA_DV
MOBA_HEADS = 8
MOBA_DH = 64
MOBA_BLOCK = 256
MOBA_TOPK = 3
MOBA_W = MOBA_HEADS * MOBA_DH
REL_BUCKETS = 32
REL_MAX_DIST = 128
RMS_EPS = 1e-6
NEG_INF = -1e30
LOG2E = math.log2(math.e)

LANES = 128
SUBLANES = 8
VMEM_LIMIT = 56 * 1024 * 1024

GLA_IN_W = GLA_QK_W * 2 + GLA_V_W * 2 + LANES
HEADS_PER_STEP = LANES // MOBA_DH

NT_DIMS = (((1,), (1,)), ((), ()))
TN_DIMS = (((0,), (0,)), ((), ()))


def _rms(x, g):
    return x * lax.rsqrt(jnp.mean(x * x, axis=-1, keepdims=True) + RMS_EPS) * g


def _params(*sem):
    return pltpu.CompilerParams(dimension_semantics=sem, vmem_limit_bytes=VMEM_LIMIT)


def _const_spec(shape):
    return pl.BlockSpec(shape, lambda *_: (0,) * len(shape))


def _inproj_kernel(x_ref, g_ref, wn_ref, wt_ref, lru_ref, gla_ref, mk_ref, mqv_ref):
    h = _rms(x_ref[0], g_ref[...]).astype(BF16)
    pn = jnp.dot(h, wn_ref[...], preferred_element_type=F32)
    lru_ref[0] = pn[:, :2 * LRU_W]
    gla_ref[0] = pn[:, 2 * LRU_W:2 * LRU_W + GLA_IN_W]
    mk_ref[0] = pn[:, 2 * LRU_W + GLA_IN_W:].astype(BF16)
    pt = lax.dot_general(wt_ref[...], h, NT_DIMS, preferred_element_type=F32)
    mqv_ref[0, :MOBA_W, :] = (pt[:MOBA_W] * (MOBA_DH ** -0.5 * LOG2E)).astype(BF16)
    mqv_ref[0, MOBA_W:, :] = pt[MOBA_W:].astype(BF16)


def _inproj(x, g, wn, wt, ts):
    b, s, d = x.shape
    nn = wn.shape[1]
    return pl.pallas_call(
        _inproj_kernel,
        grid=(b, s // ts),
        in_specs=[
            pl.BlockSpec((1, ts, d), lambda i, j: (i, j, 0)),
            _const_spec((1, d)),
            _const_spec((d, nn)),
            _const_spec((2 * MOBA_W, d)),
        ],
        out_specs=[
            pl.BlockSpec((1, ts, 2 * LRU_W), lambda i, j: (i, j, 0)),
            pl.BlockSpec((1, ts, GLA_IN_W), lambda i, j: (i, j, 0)),
            pl.BlockSpec((1, ts, MOBA_W), lambda i, j: (i, j, 0)),
            pl.BlockSpec((1, 2 * MOBA_W, ts), lambda i, j: (i, 0, j)),
        ],
        out_shape=[
            jax.ShapeDtypeStruct((b, s, 2 * LRU_W), F32),
            jax.ShapeDtypeStruct((b, s, GLA_IN_W), F32),
            jax.ShapeDtypeStruct((b, s, MOBA_W), BF16),
            jax.ShapeDtypeStruct((b, 2 * MOBA_W, s), BF16),
        ],
        compiler_params=_params("parallel", "parallel"),
        name="inproj",
    )(x, g, wn, wt)


def _shift_rows(x, prev_tail, k, row8):
    sh = pltpu.roll(x, k, axis=0)
    top = jnp.where(row8 < k, pltpu.roll(prev_tail, k, axis=0), sh[:SUBLANES])
    return jnp.concatenate([top, sh[SUBLANES:]], axis=0)


def _lru_kernel(in_ref, cw_ref, cb_ref, wa_ref, ba_ref, wx_ref, bx_ref, lam_ref, y_ref,
                h_sc, tail_sc):
    j = pl.program_id(1)
    ts = in_ref.shape[1]

    @pl.when(j == 0)
    def _():
        h_sc[...] = jnp.zeros_like(h_sc)
        tail_sc[...] = jnp.zeros_like(tail_sc)

    xb = in_ref[0, :, :LRU_W]
    gb = in_ref[0, :, LRU_W:]
    row = lax.broadcasted_iota(jnp.int32, (ts, LRU_W), 0)
    row8 = lax.broadcasted_iota(jnp.int32, (SUBLANES, LRU_W), 0)
    tail = tail_sc[...]
    xc = xb * cw_ref[LRU_CONV - 1:LRU_CONV, :] + cb_ref[...]
    for k in range(1, LRU_CONV):
        xc = xc + _shift_rows(xb, tail, k, row8) * cw_ref[LRU_CONV - 1 - k:LRU_CONV - k, :]
    tail_sc[...] = xb[ts - SUBLANES:]

    xcb = xc.astype(BF16)
    r = jax.nn.sigmoid(jnp.dot(xcb, wa_ref[...], preferred_element_type=F32) + ba_ref[...])
    i = jax.nn.sigmoid(jnp.dot(xcb, wx_ref[...], preferred_element_type=F32) + bx_ref[...])
    log_a = -LRU_C * r * jax.nn.softplus(-lam_ref[...])
    a = jnp.exp(log_a)
    mult = jnp.sqrt(-jnp.tanh(log_a) * (a * a + 1.0))
    mult = jnp.where(row + j * ts == 0, 1.0, mult)
    u = mult * (i * xc)

    d = 1
    while d < ts:
        keep = row >= d
        a_sh = jnp.where(keep, pltpu.roll(a, d, axis=0), 1.0)
        u_sh = jnp.where(keep, pltpu.roll(u, d, axis=0), 0.0)
        u = a * u_sh + u
        a = a * a_sh
        d *= 2
    h = u + a * h_sc[...]
    h_sc[...] = h[ts - 1:ts]
    y_ref[0] = (h * jax.nn.gelu(gb)).astype(y_ref.dtype)


def _lru(lru_in, cw, cb, wa, ba, wx, bx, lam, ts):
    b, s, _ = lru_in.shape
    return pl.pallas_call(
        _lru_kernel,
        grid=(b, s // ts),
        in_specs=[
            pl.BlockSpec((1, ts, 2 * LRU_W), lambda i, j: (i, j, 0)),
            _const_spec((LRU_CONV, LRU_W)),
            _const_spec((1, LRU_W)),
            _const_spec((LRU_W, LRU_W)),
            _const_spec((1, LRU_W)),
            _const_spec((LRU_W, LRU_W)),
            _const_spec((1, LRU_W)),
            _const_spec((1, LRU_W)),
        ],
        out_specs=pl.BlockSpec((1, ts, LRU_W), lambda i, j: (i, j, 0)),
        out_shape=jax.ShapeDtypeStruct((b, s, LRU_W), BF16),
        scratch_shapes=[pltpu.VMEM((1, LRU_W), F32), pltpu.VMEM((SUBLANES, LRU_W), F32)],
        compiler_params=_params("parallel", "arbitrary"),
        name="rglru",
    )(lru_in, cw, cb, wa, ba, wx, bx, lam)


def _gla_kernel(in_ref, w2_ref, gb_ref, gain_ref, y_ref, st_sc):
    ts = in_ref.shape[1]
    c = GLA_CHUNK

    @pl.when(pl.program_id(1) == 0)
    def _():
        st_sc[...] = jnp.zeros_like(st_sc)

    q = in_ref[0, :, :GLA_QK_W]
    k = in_ref[0, :, GLA_QK_W:2 * GLA_QK_W]
    v = in_ref[0, :, 2 * GLA_QK_W:2 * GLA_QK_W + GLA_V_W]
    g_out = in_ref[0, :, 2 * GLA_QK_W + GLA_V_W:2 * GLA_QK_W + 2 * GLA_V_W]
    g_lr = in_ref[0, :, 2 * GLA_QK_W + 2 * GLA_V_W:]

    gate = jnp.dot(g_lr.astype(BF16), w2_ref[...], preferred_element_type=F32) + gb_ref[...]
    log_alpha = jax.nn.log_sigmoid(gate) / GLA_TAU

    ri = lax.broadcasted_iota(jnp.int32, (ts, ts), 0)
    ci = lax.broadcasted_iota(jnp.int32, (ts, ts), 1)
    tri = jnp.where((ri // c == ci // c) & (ci <= ri), 1.0, 0.0).astype(F32)
    cum = jnp.dot(tri, log_alpha, preferred_element_type=F32, precision=lax.Precision.HIGHEST)

    qk_lane = lax.broadcasted_iota(jnp.int32, (1, GLA_QK_W), 1) // GLA_DK
    v_lane = lax.broadcasted_iota(jnp.int32, (1, GLA_V_W), 1) // GLA_DV
    st_row = lax.broadcasted_iota(jnp.int32, (GLA_V_W, GLA_QK_W), 0) // GLA_DV
    st_col = lax.broadcasted_iota(jnp.int32, (GLA_V_W, GLA_QK_W), 1) // GLA_DK
    same_head = st_row == st_col
    causal = (lax.broadcasted_iota(jnp.int32, (c, c), 1)
              <= lax.broadcasted_iota(jnp.int32, (c, c), 0))

    outs = []
    for n in range(ts // c):
        sl = slice(n * c, (n + 1) * c)
        cum_n = cum[sl]
        last = cum_n[c - 1:c]
        q_d = q[sl] * (GLA_DK ** -0.5) * jnp.exp(cum_n)
        k_d = (k[sl] * jnp.exp(-cum_n)).astype(BF16)
        k_l = (k[sl] * jnp.exp(last - cum_n)).astype(BF16)
        v_n = v[sl]
        st_prev = st_sc[...]
        o_n = lax.dot_general(q_d.astype(BF16), st_prev.astype(BF16), NT_DIMS,
                              preferred_element_type=F32)
        for hh in range(GLA_HEADS):
            q_h = jnp.where(qk_lane == hh, q_d, 0.0).astype(BF16)
            attn = lax.dot_general(q_h, k_d, NT_DIMS, preferred_element_type=F32)
            attn = jnp.where(causal, attn, 0.0).astype(BF16)
            v_h = jnp.where(v_lane == hh, v_n, 0.0).astype(BF16)
            o_n = o_n + jnp.dot(attn, v_h, preferred_element_type=F32)
        kv = lax.dot_general(v_n.astype(BF16), k_l, TN_DIMS, preferred_element_type=F32)
        st_sc[...] = st_prev * jnp.exp(last) + jnp.where(same_head, kv, 0.0)
        outs.append(o_n)
    o = jnp.concatenate(outs, axis=0)

    hr = lax.broadcasted_iota(jnp.int32, (GLA_V_W, GLA_V_W), 0) // GLA_DV
    hc = lax.broadcasted_iota(jnp.int32, (GLA_V_W, GLA_V_W), 1) // GLA_DV
    head_mean = jnp.where(hr == hc, 1.0 / GLA_DV, 0.0).astype(F32)
    ms = jnp.dot(o * o, head_mean, preferred_element_type=F32, precision=lax.Precision.HIGHEST)
    y = o * lax.rsqrt(ms + RMS_EPS) * gain_ref[...] * jax.nn.silu(g_out)
    y_ref[0] = y.astype(y_ref.dtype)


def _gla(gla_in, w2, gb, gain, ts):
    b, s, _ = gla_in.shape
    return pl.pallas_call(
        _gla_kernel,
        grid=(b, s // ts),
        in_specs=[
            pl.BlockSpec((1, ts, GLA_IN_W), lambda i, j: (i, j, 0)),
            _const_spec((LANES, GLA_QK_W)),
            _const_spec((1, GLA_QK_W)),
            _const_spec((1, GLA_V_W)),
        ],
        out_specs=pl.BlockSpec((1, ts, GLA_V_W), lambda i, j: (i, j, 0)),
        out_shape=jax.ShapeDtypeStruct((b, s, GLA_V_W), BF16),
        scratch_shapes=[pltpu.VMEM((GLA_V_W, GLA_QK_W), F32)],
        compiler_params=_params("parallel", "arbitrary"),
        name="gla",
    )(gla_in, w2, gb, gain)


def _t5_bucket_np(rel):
    n = np.maximum(rel, 0)
    max_exact = REL_BUCKETS // 2
    nf = np.maximum(n, 1).astype(np.float32)
    large = max_exact + (np.log(nf / np.float32(max_exact)) / np.float32(math.log(REL_MAX_DIST / max_exact))
                         * np.float32(REL_BUCKETS - max_exact)).astype(np.int32)
    large = np.minimum(large, REL_BUCKETS - 1)
    return np.where(n < max_exact, n, large).astype(np.int32)


FAR_DIST = int(np.argmax(_t5_bucket_np(np.arange(4 * REL_MAX_DIST)) == REL_BUCKETS - 1))
assert np.all(_t5_bucket_np(np.arange(FAR_DIST, 1 << 16)) == REL_BUCKETS - 1)
assert FAR_DIST <= MOBA_BLOCK

N_TABLES = 3
V_AUG = MOBA_DH + 16


def _bias_kernel(bkt_ref, rb_ref, out_ref):
    h = pl.program_id(0)
    for t in range(2):
        bkt = bkt_ref[t]
        acc = jnp.zeros(bkt.shape, F32)
        for bb in range(REL_BUCKETS):
            acc = jnp.where(bkt == bb, rb_ref[bb, h], acc)
        out_ref[0, t] = jnp.where(bkt < 0, NEG_INF, acc * LOG2E)
    out_ref[0, 2] = jnp.full(bkt_ref.shape[1:], rb_ref[REL_BUCKETS - 1, h] * LOG2E, F32)


def _bias_tables(rel_bias):
    ko = np.arange(MOBA_BLOCK)[:, None]
    qo = np.arange(MOBA_BLOCK)[None, :]
    own = np.where(qo >= ko, _t5_bucket_np(qo - ko), -1)
    bkt = np.stack([own, _t5_bucket_np(qo - ko + MOBA_BLOCK)]).astype(np.int32)
    return pl.pallas_call(
        _bias_kernel,
        grid=(MOBA_HEADS,),
        in_specs=[
            _const_spec((2, MOBA_BLOCK, MOBA_BLOCK)),
            pl.BlockSpec(memory_space=pltpu.SMEM),
        ],
        out_specs=pl.BlockSpec((1, N_TABLES, MOBA_BLOCK, MOBA_BLOCK), lambda h: (h, 0, 0, 0)),
        out_shape=jax.ShapeDtypeStruct((MOBA_HEADS, N_TABLES, MOBA_BLOCK, MOBA_BLOCK), F32),
        compiler_params=_params("parallel"),
        name="moba_bias",
    )(jnp.asarray(bkt), rel_bias.astype(F32))


def _moba_kernel(q_ref, v_ref, k_ref, bias_ref, y_ref,
                 kmean_sc, vaug_sc, qz_sc, mb_sc, s_sc, p_sc, m_sc, alpha_sc, acc_sc):
    n = pl.program_id(2)
    blk = MOBA_BLOCK
    seq = k_ref.shape[1]
    nb = seq // blk
    heads = range(HEADS_PER_STEP)

    @pl.when(n == 0)
    def _():
        for jb in range(nb):
            kb = k_ref[0, jb * blk:(jb + 1) * blk, :].astype(F32)
            kmean_sc[jb:jb + 1, :] = jnp.mean(kb, axis=0, keepdims=True)
        ones_row = lax.broadcasted_iota(jnp.int32, (V_AUG - MOBA_DH, seq), 0) == 0
        for hh in heads:
            vaug_sc[hh, :MOBA_DH, :] = v_ref[0, hh * MOBA_DH:(hh + 1) * MOBA_DH, :]
            vaug_sc[hh, MOBA_DH:, :] = jnp.where(ones_row, 1.0, 0.0).astype(BF16)
            mb_sc[hh, nb:, :] = jnp.full((mb_sc.shape[1] - nb, blk), NEG_INF, F32)

    q2 = q_ref[0]
    feat = lax.broadcasted_iota(jnp.int32, (LANES, blk), 0) // MOBA_DH
    lane_head = lax.broadcasted_iota(jnp.int32, (nb, LANES), 1) // MOBA_DH
    blk_id = lax.broadcasted_iota(jnp.int32, (nb, blk), 0)
    q2f = q2.astype(F32)
    for hh in heads:
        qz_sc[:, hh * blk:(hh + 1) * blk] = jnp.where(feat == hh, q2, jnp.zeros_like(q2))
        kmz = jnp.where(lane_head == hh, kmean_sc[...], 0.0)
        g = jnp.dot(kmz, q2f, preferred_element_type=F32, precision=lax.Precision.HIGHEST)
        g = jnp.where(blk_id < n, g, -jnp.inf)
        rank = jnp.zeros((nb, blk), F32)
        for jb in range(nb):
            gj = g[jb:jb + 1, :]
            beats = (gj > g) | ((gj == g) & (blk_id > jb))
            rank = rank + jnp.where(beats, 1.0, 0.0)
        keep = ((blk_id < n) & (rank < MOBA_TOPK)) | (blk_id == n)
        mb_sc[hh, :nb, :] = jnp.where(keep, 0.0, NEG_INF)
        m_sc[hh] = jnp.full((1, blk), NEG_INF, F32)
        acc_sc[hh] = jnp.zeros((V_AUG, blk), F32)
        p_sc[1, hh] = jnp.zeros((blk, blk), BF16)
        alpha_sc[1, hh] = jnp.ones((1, blk), F32)

    def logits(i, slot):
        jq = jnp.maximum(n - i, 0)
        kb = k_ref[0, pl.ds(pl.multiple_of(jq * blk, blk), blk), :]
        s_sc[slot] = jnp.dot(kb, qz_sc[...], preferred_element_type=F32)

    def softmax_update(t, slot):
        tab = jnp.minimum(t, N_TABLES - 1)
        row = jnp.where(t <= n, n - t, nb)
        for hh in heads:
            s = s_sc[slot, :, hh * blk:(hh + 1) * blk] + bias_ref[hh, tab]
            mb = mb_sc[hh, pl.ds(row, 1), :]
            m_old = m_sc[hh]
            m_new = jnp.maximum(m_old, jnp.max(s, axis=0, keepdims=True) + mb)
            alpha_sc[slot, hh] = jnp.exp2(m_old - m_new)
            m_sc[hh] = m_new
            p_sc[slot, hh] = jnp.exp2(s + (mb - m_new)).astype(BF16)

    def values(t, slot):
        jv = jnp.clip(n - t, 0, n)
        start = pl.multiple_of(jv * blk, blk)
        for hh in heads:
            pv = jnp.dot(vaug_sc[hh, :, pl.ds(start, blk)], p_sc[slot, hh],
                         preferred_element_type=F32)
            acc_sc[hh] = alpha_sc[slot, hh] * acc_sc[hh] + pv

    def stage(i, slot):
        logits(i, slot)
        softmax_update(i - 1, 1 - slot)
        values(i - 2, slot)

    def stage_pair(d, carry):
        stage(2 * d + 1, 1)
        stage(2 * d + 2, 0)
        return carry

    logits(0, 0)
    lax.fori_loop(0, (n + 3) // 2, stage_pair, 0)

    out_t = jnp.concatenate(
        [acc_sc[hh, :MOBA_DH, :] / acc_sc[hh, MOBA_DH:MOBA_DH + 1, :] for hh in heads], axis=0)
    y_ref[0] = out_t.T.astype(y_ref.dtype)


def _moba(mqv, mk, bias_tab):
    b, s, _ = mk.shape
    blk = MOBA_BLOCK
    nb = s // blk
    npair = MOBA_HEADS // HEADS_PER_STEP
    return pl.pallas_call(
        _moba_kernel,
        grid=(b, npair, nb),
        in_specs=[
            pl.BlockSpec((1, LANES, blk), lambda i, p, n: (i, p, n)),
            pl.BlockSpec((1, LANES, s), lambda i, p, n: (i, npair + p, 0)),
            pl.BlockSpec((1, s, LANES), lambda i, p, n: (i, 0, p)),
            pl.BlockSpec((HEADS_PER_STEP, N_TABLES, blk, blk), lambda i, p, n: (p, 0, 0, 0)),
        ],
        out_specs=pl.BlockSpec((1, blk, LANES), lambda i, p, n: (i, n, p)),
        out_shape=jax.ShapeDtypeStruct((b, s, MOBA_W), BF16),
        scratch_shapes=[
            pltpu.VMEM((nb, LANES), F32),
            pltpu.VMEM((HEADS_PER_STEP, V_AUG, s), BF16),
            pltpu.VMEM((LANES, HEADS_PER_STEP * blk), BF16),
            pltpu.VMEM((HEADS_PER_STEP, nb + SUBLANES, blk), F32),
            pltpu.VMEM((2, blk, HEADS_PER_STEP * blk), F32),
            pltpu.VMEM((2, HEADS_PER_STEP, blk, blk), BF16),
            pltpu.VMEM((HEADS_PER_STEP, 1, blk), F32),
            pltpu.VMEM((2, HEADS_PER_STEP, 1, blk), F32),
            pltpu.VMEM((HEADS_PER_STEP, V_AUG, blk), F32),
        ],
        compiler_params=_params("parallel", "parallel", "arbitrary"),
        name="moba",
    )(mqv, mqv, mk, bias_tab)


def _outproj_kernel(x_ref, lru_ref, gla_ref, moba_ref, wl_ref, wg_ref, wm_ref, g_ref, o_ref):
    y = jnp.dot(lru_ref[...], wl_ref[...], preferred_element_type=F32)
    y = y + jnp.dot(gla_ref[...], wg_ref[...], preferred_element_type=F32)
    y = y + jnp.dot(moba_ref[...], wm_ref[...], preferred_element_type=F32)
    o_ref[...] = x_ref[...] + _rms(y, g_ref[...])


def _outproj(x, y_lru, y_gla, y_moba, w_out, g, tt):
    t, d = x.shape
    wl = w_out[:LRU_W]
    wg = w_out[LRU_W:LRU_W + GLA_V_W]
    wm = w_out[LRU_W + GLA_V_W:]
    row = lambda w: pl.BlockSpec((tt, w), lambda i: (i, 0))
    return pl.pallas_call(
        _outproj_kernel,
        grid=(t // tt,),
        in_specs=[row(d), row(LRU_W), row(GLA_V_W), row(MOBA_W),
                  _const_spec(wl.shape), _const_spec(wg.shape), _const_spec(wm.shape),
                  _const_spec((1, d))],
        out_specs=row(d),
        out_shape=jax.ShapeDtypeStruct((t, d), F32),
        compiler_params=_params("parallel"),
        name="outproj",
    )(x, y_lru, y_gla, y_moba, wl, wg, wm, g)


def _ffn_kernel(x_ref, gpre_ref, wg_ref, wu_ref, wd_ref, gpost_ref, o_ref):
    x = x_ref[...]
    h = _rms(x, gpre_ref[...]).astype(BF16)
    a = jax.nn.silu(jnp.dot(h, wg_ref[...], preferred_element_type=F32))
    a = (a * jnp.dot(h, wu_ref[...], preferred_element_type=F32)).astype(BF16)
    f = jnp.dot(a, wd_ref[...], preferred_element_type=F32)
    o_ref[...] = x + _rms(f, gpost_ref[...])


def _ffn(x, gpre, wg, wu, wd, gpost, tt):
    t, d = x.shape
    dff = wg.shape[1]
    once = dict(pipeline_mode=pl.Buffered(1))
    return pl.pallas_call(
        _ffn_kernel,
        grid=(t // tt,),
        in_specs=[
            pl.BlockSpec((tt, d), lambda i: (i, 0)),
            _const_spec((1, d)),
            pl.BlockSpec((d, dff), lambda i: (0, 0), **once),
            pl.BlockSpec((d, dff), lambda i: (0, 0), **once),
            pl.BlockSpec((dff, d), lambda i: (0, 0), **once),
            _const_spec((1, d)),
        ],
        out_specs=pl.BlockSpec((tt, d), lambda i: (i, 0)),
        out_shape=jax.ShapeDtypeStruct((t, d), F32),
        compiler_params=_params("parallel"),
        name="ffn",
    )(x, gpre, wg, wu, wd, gpost)


def _block_diag(w):
    g, n, _ = w.shape
    eye = jnp.eye(g, dtype=w.dtype)
    return (eye[:, None, :, None] * w[:, :, None, :]).reshape(g * n, g * n)


def _split_w_in(w_in):
    sizes = (LRU_W, LRU_W, GLA_QK_W, GLA_QK_W, GLA_V_W, GLA_RANK, GLA_V_W, MOBA_W, MOBA_W, MOBA_W)
    offs = np.cumsum((0,) + sizes)
    col = lambda i: w_in[:, offs[i]:offs[i + 1]]
    lru_x, lru_g, gq, gk, gv, g_lr, g_out, mq, mk, mv = (col(i) for i in range(len(sizes)))
    g_lr = jnp.pad(g_lr, ((0, 0), (0, LANES - GLA_RANK)))
    w_nat = jnp.concatenate([lru_x, lru_g, gq, gk, gv, g_out, g_lr, mk], axis=1).astype(BF16)
    w_t = jnp.concatenate([mq, mv], axis=1).T.astype(BF16)
    return w_nat, w_t


def kernel(x, pre_mix_norm, post_mix_norm, pre_ffn_norm, post_ffn_norm, w_in, w_out, lru_conv_w, lru_conv_b, lru_wa, lru_ba, lru_wx, lru_bx, lru_lambda, gla_gate_w2, gla_gate_b, gla_norm, rel_bias, w_ffn_gate, w_ffn_up, w_ffn_down):
    b, s, d = x.shape
    assert s % MOBA_BLOCK == 0 and d == LRU_W + GLA_V_W + MOBA_W
    depth = w_in.shape[0]
    ts_proj = min(512, s)
    ts_seq = min(256, s)
    tt = min(512, b * s)
    row = lambda v: v.reshape(1, -1).astype(F32)

    bias_tab = _bias_tables(rel_bias)
    for l in range(depth):
        w_nat, w_t = _split_w_in(w_in[l])
        lru_in, gla_in, mk, mqv = _inproj(x, row(pre_mix_norm[l]), w_nat, w_t, ts_proj)
        y_lru = _lru(lru_in, lru_conv_w[l].astype(F32), row(lru_conv_b[l]),
                     _block_diag(lru_wa[l]).astype(BF16), row(lru_ba[l]),
                     _block_diag(lru_wx[l]).astype(BF16), row(lru_bx[l]),
                     row(lru_lambda[l]), ts_seq)
        w2 = jnp.pad(gla_gate_w2[l], ((0, LANES - GLA_RANK), (0, 0))).astype(BF16)
        y_gla = _gla(gla_in, w2, row(gla_gate_b[l]), row(gla_norm[l]), ts_seq)
        y_moba = _moba(mqv, mk, bias_tab)
        x2 = _outproj(x.reshape(b * s, d), y_lru.reshape(b * s, -1), y_gla.reshape(b * s, -1),
                      y_moba.reshape(b * s, -1), w_out[l].astype(BF16), row(post_mix_norm[l]), tt)
        x2 = _ffn(x2, row(pre_ffn_norm[l]), w_ffn_gate[l].astype(BF16), w_ffn_up[l].astype(BF16),
                  w_ffn_down[l].astype(BF16), row(post_ffn_norm[l]), tt)
        x = x2.reshape(b, s, d)
    return x
```

```python
import functools
import math

import numpy as np
import jax
import jax.numpy as jnp
from jax import lax
from jax.experimental import pallas as pl
from jax.experimental.pallas import tpu as pltpu

F32 = jnp.float32
BF16 = jnp.bfloat16

LRU_W = 256
LRU_BLOCKS = 4
LRU_CONV = 4
LRU_C = 8.0
GLA_HEADS = 4
GLA_DV = 64
GLA_DK = 32
GLA_RANK = 16
GLA_TAU = 16.0
GLA_CHUNK = 64
GLA_QK_W = GLA_HEADS * GLA_DK
GLA_V_W = GLA_HEADS * GLA_DV
MOBA_HEADS = 8
MOBA_DH = 64
MOBA_BLOCK = 256
MOBA_TOPK = 3
MOBA_W = MOBA_HEADS * MOBA_DH
REL_BUCKETS = 32
REL_MAX_DIST = 128
RMS_EPS = 1e-6
NEG_INF = -1e30
LOG2E = math.log2(math.e)

LANES = 128
SUBLANES = 8
VMEM_LIMIT = 56 * 1024 * 1024

GLA_IN_W = GLA_QK_W * 2 + GLA_V_W * 2 + LANES
HEADS_PER_STEP = LANES // MOBA_DH

NT_DIMS = (((1,), (1,)), ((), ()))
TN_DIMS = (((0,), (0,)), ((), ()))


def _rms(x, g):
    return x * lax.rsqrt(jnp.mean(x * x, axis=-1, keepdims=True) + RMS_EPS) * g


def _split_hi_lo(x):
    hi = x.astype(BF16)
    return hi, (x - hi.astype(F32)).astype(BF16)


def _params(*sem):
    return pltpu.CompilerParams(dimension_semantics=sem, vmem_limit_bytes=VMEM_LIMIT)


def _const_spec(shape):
    return pl.BlockSpec(shape, lambda *_: (0,) * len(shape))


def _inproj_kernel(x_ref, g_ref, wn_ref, wt_ref, lru_ref, gla_ref, mk_ref, mqv_ref):
    h = _rms(x_ref[0], g_ref[...]).astype(BF16)
    pn = jnp.dot(h, wn_ref[...], preferred_element_type=F32)
    lru_ref[0] = pn[:, :2 * LRU_W]
    gla_ref[0] = pn[:, 2 * LRU_W:2 * LRU_W + GLA_IN_W]
    mk_ref[0] = pn[:, 2 * LRU_W + GLA_IN_W:].astype(BF16)
    pt = lax.dot_general(wt_ref[...], h, NT_DIMS, preferred_element_type=F32)
    mqv_ref[0, :MOBA_W, :] = (pt[:MOBA_W] * (MOBA_DH ** -0.5 * LOG2E)).astype(BF16)
    mqv_ref[0, MOBA_W:, :] = pt[MOBA_W:].astype(BF16)


def _inproj(x, g, wn, wt, ts):
    b, s, d = x.shape
    nn = wn.shape[1]
    return pl.pallas_call(
        _inproj_kernel,
        grid=(b, s // ts),
        in_specs=[
            pl.BlockSpec((1, ts, d), lambda i, j: (i, j, 0)),
            _const_spec((1, d)),
            _const_spec((d, nn)),
            _const_spec((2 * MOBA_W, d)),
        ],
        out_specs=[
            pl.BlockSpec((1, ts, 2 * LRU_W), lambda i, j: (i, j, 0)),
            pl.BlockSpec((1, ts, GLA_IN_W), lambda i, j: (i, j, 0)),
            pl.BlockSpec((1, ts, MOBA_W), lambda i, j: (i, j, 0)),
            pl.BlockSpec((1, 2 * MOBA_W, ts), lambda i, j: (i, 0, j)),
        ],
        out_shape=[
            jax.ShapeDtypeStruct((b, s, 2 * LRU_W), F32),
            jax.ShapeDtypeStruct((b, s, GLA_IN_W), F32),
            jax.ShapeDtypeStruct((b, s, MOBA_W), BF16),
            jax.ShapeDtypeStruct((b, 2 * MOBA_W, s), BF16),
        ],
        compiler_params=_params("parallel", "parallel"),
        name="inproj",
    )(x, g, wn, wt)


def _shift_rows(x, prev_tail, k, row8):
    sh = pltpu.roll(x, k, axis=0)
    top = jnp.where(row8 < k, pltpu.roll(prev_tail, k, axis=0), sh[:SUBLANES])
    return jnp.concatenate([top, sh[SUBLANES:]], axis=0)


def _lru_kernel(in_ref, cw_ref, cb_ref, wa_ref, ba_ref, wx_ref, bx_ref, lam_ref, y_ref,
                h_sc, tail_sc):
    j = pl.program_id(1)
    ts = in_ref.shape[1]

    @pl.when(j == 0)
    def _():
        h_sc[...] = jnp.zeros_like(h_sc)
        tail_sc[...] = jnp.zeros_like(tail_sc)

    xb = in_ref[0, :, :LRU_W]
    gb = in_ref[0, :, LRU_W:]
    row = lax.broadcasted_iota(jnp.int32, (ts, LRU_W), 0)
    row8 = lax.broadcasted_iota(jnp.int32, (SUBLANES, LRU_W), 0)
    tail = tail_sc[...]
    xc = xb * cw_ref[LRU_CONV - 1:LRU_CONV, :] + cb_ref[...]
    for k in range(1, LRU_CONV):
        xc = xc + _shift_rows(xb, tail, k, row8) * cw_ref[LRU_CONV - 1 - k:LRU_CONV - k, :]
    tail_sc[...] = xb[ts - SUBLANES:]

    xcb = xc.astype(BF16)
    r = jax.nn.sigmoid(jnp.dot(xcb, wa_ref[...], preferred_element_type=F32) + ba_ref[...])
    i = jax.nn.sigmoid(jnp.dot(xcb, wx_ref[...], preferred_element_type=F32) + bx_ref[...])
    log_a = -LRU_C * r * jax.nn.softplus(-lam_ref[...])
    a = jnp.exp(log_a)
    mult = jnp.sqrt(-jnp.tanh(log_a) * (a * a + 1.0))
    mult = jnp.where(row + j * ts == 0, 1.0, mult)
    u = mult * (i * xc)

    d = 1
    while d < ts:
        keep = row >= d
        a_sh = jnp.where(keep, pltpu.roll(a, d, axis=0), 1.0)
        u_sh = jnp.where(keep, pltpu.roll(u, d, axis=0), 0.0)
        u = a * u_sh + u
        a = a * a_sh
        d *= 2
    h = u + a * h_sc[...]
    h_sc[...] = h[ts - 1:ts]
    y_ref[0] = (h * jax.nn.gelu(gb)).astype(y_ref.dtype)


def _lru(lru_in, cw, cb, wa, ba, wx, bx, lam, ts):
    b, s, _ = lru_in.shape
    return pl.pallas_call(
        _lru_kernel,
        grid=(b, s // ts),
        in_specs=[
            pl.BlockSpec((1, ts, 2 * LRU_W), lambda i, j: (i, j, 0)),
            _const_spec((LRU_CONV, LRU_W)),
            _const_spec((1, LRU_W)),
            _const_spec((LRU_W, LRU_W)),
            _const_spec((1, LRU_W)),
            _const_spec((LRU_W, LRU_W)),
            _const_spec((1, LRU_W)),
            _const_spec((1, LRU_W)),
        ],
        out_specs=pl.BlockSpec((1, ts, LRU_W), lambda i, j: (i, j, 0)),
        out_shape=jax.ShapeDtypeStruct((b, s, LRU_W), BF16),
        scratch_shapes=[pltpu.VMEM((1, LRU_W), F32), pltpu.VMEM((SUBLANES, LRU_W), F32)],
        compiler_params=_params("parallel", "arbitrary"),
        name="rglru",
    )(lru_in, cw, cb, wa, ba, wx, bx, lam)


def _gla_kernel(in_ref, w2_ref, gb_ref, gain_ref, y_ref, st_sc):
    ts = in_ref.shape[1]
    c = GLA_CHUNK

    @pl.when(pl.program_id(1) == 0)
    def _():
        st_sc[...] = jnp.zeros_like(st_sc)

    q = in_ref[0, :, :GLA_QK_W]
    k = in_ref[0, :, GLA_QK_W:2 * GLA_QK_W]
    v = in_ref[0, :, 2 * GLA_QK_W:2 * GLA_QK_W + GLA_V_W]
    g_out = in_ref[0, :, 2 * GLA_QK_W + GLA_V_W:2 * GLA_QK_W + 2 * GLA_V_W]
    g_lr = in_ref[0, :, 2 * GLA_QK_W + 2 * GLA_V_W:]

    gate = jnp.dot(g_lr.astype(BF16), w2_ref[...], preferred_element_type=F32) + gb_ref[...]
    log_alpha = jax.nn.log_sigmoid(gate) / GLA_TAU

    ri = lax.broadcasted_iota(jnp.int32, (ts, ts), 0)
    ci = lax.broadcasted_iota(jnp.int32, (ts, ts), 1)
    in_chunk = (ri // c == ci // c) & (ci <= ri)
    tri = jnp.where(in_chunk, 1.0, 0.0).astype(BF16)
    la_hi, la_lo = _split_hi_lo(log_alpha)
    cum = (jnp.dot(tri, la_hi, preferred_element_type=F32)
           + jnp.dot(tri, la_lo, preferred_element_type=F32))

    qk_lane = lax.broadcasted_iota(jnp.int32, (1, GLA_QK_W), 1) // GLA_DK
    v_lane = lax.broadcasted_iota(jnp.int32, (1, GLA_V_W), 1) // GLA_DV
    st_row = lax.broadcasted_iota(jnp.int32, (GLA_V_W, GLA_QK_W), 0) // GLA_DV
    st_col = lax.broadcasted_iota(jnp.int32, (GLA_V_W, GLA_QK_W), 1) // GLA_DK
    same_head = st_row == st_col
    chunks = range(ts // c)
    rows = [slice(n * c, (n + 1) * c) for n in chunks]
    lasts = [cum[(n + 1) * c - 1:(n + 1) * c] for n in chunks]
    last_full = jnp.concatenate([jnp.broadcast_to(l, (c, GLA_QK_W)) for l in lasts], axis=0)

    q_d = (q * (GLA_DK ** -0.5) * jnp.exp(cum)).astype(BF16)
    k_d = (k * jnp.exp(-cum)).astype(BF16)
    k_l = (k * jnp.exp(last_full - cum)).astype(BF16)
    v_b = v.astype(BF16)

    o = jnp.zeros((ts, GLA_V_W), F32)
    for hh in range(GLA_HEADS):
        q_h = jnp.where(qk_lane == hh, q_d, jnp.zeros_like(q_d))
        attn = lax.dot_general(q_h, k_d, NT_DIMS, preferred_element_type=F32)
        attn = jnp.where(in_chunk, attn, 0.0).astype(BF16)
        v_h = jnp.where(v_lane == hh, v_b, jnp.zeros_like(v_b))
        o = o + jnp.dot(attn, v_h, preferred_element_type=F32)

    st = st_sc[...]
    inter = []
    for n in chunks:
        inter.append(lax.dot_general(q_d[rows[n]], st.astype(BF16), NT_DIMS,
                                     preferred_element_type=F32))
        kv = lax.dot_general(v_b[rows[n]], k_l[rows[n]], TN_DIMS, preferred_element_type=F32)
        st = st * jnp.exp(lasts[n]) + jnp.where(same_head, kv, 0.0)
    st_sc[...] = st
    o = o + jnp.concatenate(inter, axis=0)

    hr = lax.broadcasted_iota(jnp.int32, (GLA_V_W, GLA_V_W), 0) // GLA_DV
    hc = lax.broadcasted_iota(jnp.int32, (GLA_V_W, GLA_V_W), 1) // GLA_DV
    head_mean = jnp.where(hr == hc, 1.0 / GLA_DV, 0.0).astype(BF16)
    sq_hi, sq_lo = _split_hi_lo(o * o)
    ms = (jnp.dot(sq_hi, head_mean, preferred_element_type=F32)
          + jnp.dot(sq_lo, head_mean, preferred_element_type=F32))
    y = o * lax.rsqrt(ms + RMS_EPS) * gain_ref[...] * jax.nn.silu(g_out)
    y_ref[0] = y.astype(y_ref.dtype)


def _gla(gla_in, w2, gb, gain, ts):
    b, s, _ = gla_in.shape
    return pl.pallas_call(
        _gla_kernel,
        grid=(b, s // ts),
        in_specs=[
            pl.BlockSpec((1, ts, GLA_IN_W), lambda i, j: (i, j, 0)),
            _const_spec((LANES, GLA_QK_W)),
            _const_spec((1, GLA_QK_W)),
            _const_spec((1, GLA_V_W)),
        ],
        out_specs=pl.BlockSpec((1, ts, GLA_V_W), lambda i, j: (i, j, 0)),
        out_shape=jax.ShapeDtypeStruct((b, s, GLA_V_W), BF16),
        scratch_shapes=[pltpu.VMEM((GLA_V_W, GLA_QK_W), F32)],
        compiler_params=_params("parallel", "arbitrary"),
        name="gla",
    )(gla_in, w2, gb, gain)


def _t5_bucket_np(rel):
    n = np.maximum(rel, 0)
    max_exact = REL_BUCKETS // 2
    nf = np.maximum(n, 1).astype(np.float32)
    large = max_exact + (np.log(nf / np.float32(max_exact)) / np.float32(math.log(REL_MAX_DIST / max_exact))
                         * np.float32(REL_BUCKETS - max_exact)).astype(np.int32)
    large = np.minimum(large, REL_BUCKETS - 1)
    return np.where(n < max_exact, n, large).astype(np.int32)


FAR_DIST = int(np.argmax(_t5_bucket_np(np.arange(4 * REL_MAX_DIST)) == REL_BUCKETS - 1))
assert np.all(_t5_bucket_np(np.arange(FAR_DIST, 1 << 16)) == REL_BUCKETS - 1)
assert FAR_DIST <= MOBA_BLOCK

N_TABLES = 2
V_AUG = MOBA_DH + 16


def _bias_kernel(bkt_ref, rb_ref, out_ref):
    h = pl.program_id(0)
    for t in range(2):
        bkt = bkt_ref[t]
        acc = jnp.zeros(bkt.shape, F32)
        for bb in range(REL_BUCKETS):
            acc = jnp.where(bkt == bb, rb_ref[bb, h], acc)
        out_ref[0, t] = jnp.where(bkt < 0, NEG_INF, acc * LOG2E)


def _bias_tables(rel_bias):
    ko = np.arange(MOBA_BLOCK)[:, None]
    qo = np.arange(MOBA_BLOCK)[None, :]
    own = np.where(qo >= ko, _t5_bucket_np(qo - ko), -1)
    bkt = np.stack([own, _t5_bucket_np(qo - ko + MOBA_BLOCK)]).astype(np.int32)
    return pl.pallas_call(
        _bias_kernel,
        grid=(MOBA_HEADS,),
        in_specs=[
            _const_spec((2, MOBA_BLOCK, MOBA_BLOCK)),
            pl.BlockSpec(memory_space=pltpu.SMEM),
        ],
        out_specs=pl.BlockSpec((1, N_TABLES, MOBA_BLOCK, MOBA_BLOCK), lambda h: (h, 0, 0, 0)),
        out_shape=jax.ShapeDtypeStruct((MOBA_HEADS, N_TABLES, MOBA_BLOCK, MOBA_BLOCK), F32),
        compiler_params=_params("parallel"),
        name="moba_bias",
    )(jnp.asarray(bkt), rel_bias.astype(F32))


def _moba_kernel(q_ref, v_ref, k_ref, bias_ref, far_ref, y_ref,
                 kmean_sc, vaug_sc, qz_sc, mb_sc, s_sc, cmax_sc, p_sc, m_sc, alpha_sc, acc_sc):
    hp = pl.program_id(1)
    n = pl.program_id(2)
    blk = MOBA_BLOCK
    seq = k_ref.shape[1]
    nb = seq // blk
    heads = range(HEADS_PER_STEP)

    @pl.when(n == 0)
    def _():
        for jb in range(nb):
            kb = k_ref[0, jb * blk:(jb + 1) * blk, :].astype(F32)
            kmean_sc[jb:jb + 1, :] = jnp.mean(kb, axis=0, keepdims=True)
        ones_row = lax.broadcasted_iota(jnp.int32, (V_AUG - MOBA_DH, seq), 0) == 0
        for hh in heads:
            vaug_sc[hh, :MOBA_DH, :] = v_ref[0, hh * MOBA_DH:(hh + 1) * MOBA_DH, :]
            vaug_sc[hh, MOBA_DH:, :] = jnp.where(ones_row, 1.0, 0.0).astype(BF16)
            mb_sc[hh, nb:, :] = jnp.full((mb_sc.shape[1] - nb, blk), NEG_INF, F32)

    q2 = q_ref[0]
    feat = lax.broadcasted_iota(jnp.int32, (LANES, blk), 0) // MOBA_DH
    lane_head = lax.broadcasted_iota(jnp.int32, (nb, LANES), 1) // MOBA_DH
    blk_id = lax.broadcasted_iota(jnp.int32, (nb, blk), 0)
    for hh in heads:
        qz_sc[:, hh * blk:(hh + 1) * blk] = jnp.where(feat == hh, q2, jnp.zeros_like(q2))
        km_hi, km_lo = _split_hi_lo(jnp.where(lane_head == hh, kmean_sc[...], 0.0))
        g = (jnp.dot(km_hi, q2, preferred_element_type=F32)
             + jnp.dot(km_lo, q2, preferred_element_type=F32))
        g = jnp.where(blk_id < n, g, -jnp.inf)
        rank = jnp.zeros((nb, blk), F32)
        for jb in range(nb):
            gj = g[jb:jb + 1, :]
            beats = (gj > g) | ((gj == g) & (blk_id > jb))
            rank = rank + jnp.where(beats, 1.0, 0.0)
        keep = ((blk_id < n) & (rank < MOBA_TOPK)) | (blk_id == n)
        far_bias = far_ref[hp * HEADS_PER_STEP + hh] * LOG2E
        mb_sc[hh, :nb, :] = jnp.where(keep, jnp.where(blk_id < n - 1, far_bias, 0.0), NEG_INF)
        m_sc[hh] = jnp.full((1, blk), NEG_INF, F32)
        acc_sc[hh] = jnp.zeros((V_AUG, blk), F32)
        p_sc[1, hh] = jnp.zeros((blk, blk), BF16)
        alpha_sc[1, hh] = jnp.ones((1, blk), F32)

    def logits(i, slot):
        jq = jnp.maximum(n - i, 0)
        kb = k_ref[0, pl.ds(pl.multiple_of(jq * blk, blk), blk), :]
        s = jnp.dot(kb, qz_sc[...], preferred_element_type=F32)
        s_sc[slot] = s
        cmax_sc[slot] = jnp.max(s.reshape(blk // SUBLANES, SUBLANES, s.shape[1]), axis=0)

    def softmax_update(t, slot, table=None):
        row = jnp.where(t <= n, n - t, nb)
        for hh in heads:
            cols = slice(hh * blk, (hh + 1) * blk)
            if table is None:
                s = s_sc[slot, :, cols]
                cm = jnp.max(cmax_sc[slot, :, cols], axis=0, keepdims=True)
            else:
                s = s_sc[slot, :, cols] + bias_ref[hh, table]
                cm = jnp.max(s, axis=0, keepdims=True)
            mb = mb_sc[hh, pl.ds(row, 1), :]
            m_old = m_sc[hh]
            m_new = jnp.maximum(m_old, cm + mb)
            alpha_sc[slot, hh] = jnp.exp2(m_old - m_new)
            m_sc[hh] = m_new
            p_sc[slot, hh] = jnp.exp2(s + (mb - m_new)).astype(BF16)

    def values(t, slot):
        jv = jnp.clip(n - t, 0, n)
        start = pl.multiple_of(jv * blk, blk)
        for hh in heads:
            pv = jnp.dot(vaug_sc[hh, :, pl.ds(start, blk)], p_sc[slot, hh],
                         preferred_element_type=F32)
            acc_sc[hh] = alpha_sc[slot, hh] * acc_sc[hh] + pv

    def stage(i, slot, table=None):
        logits(i, slot)
        softmax_update(i - 1, 1 - slot, table)
        values(i - 2, slot)

    def stage_pair(d, carry):
        stage(2 * d + 3, 1)
        stage(2 * d + 4, 0)
        return carry

    logits(0, 0)
    stage(1, 1, table=0)
    stage(2, 0, table=1)
    lax.fori_loop(0, (n + 1) // 2, stage_pair, 0)

    out_t = jnp.concatenate(
        [acc_sc[hh, :MOBA_DH, :] / acc_sc[hh, MOBA_DH:MOBA_DH + 1, :] for hh in heads], axis=0)
    y_ref[0] = out_t.T.astype(y_ref.dtype)


def _moba(mqv, mk, bias_tab, far_bias):
    b, s, _ = mk.shape
    blk = MOBA_BLOCK
    nb = s // blk
    npair = MOBA_HEADS // HEADS_PER_STEP
    return pl.pallas_call(
        _moba_kernel,
        grid=(b, npair, nb),
        in_specs=[
            pl.BlockSpec((1, LANES, blk), lambda i, p, n: (i, p, n)),
            pl.BlockSpec((1, LANES, s), lambda i, p, n: (i, npair + p, 0)),
            pl.BlockSpec((1, s, LANES), lambda i, p, n: (i, 0, p)),
            pl.BlockSpec((HEADS_PER_STEP, N_TABLES, blk, blk), lambda i, p, n: (p, 0, 0, 0)),
            pl.BlockSpec(memory_space=pltpu.SMEM),
        ],
        out_specs=pl.BlockSpec((1, blk, LANES), lambda i, p, n: (i, n, p)),
        out_shape=jax.ShapeDtypeStruct((b, s, MOBA_W), BF16),
        scratch_shapes=[
            pltpu.VMEM((nb, LANES), F32),
            pltpu.VMEM((HEADS_PER_STEP, V_AUG, s), BF16),
            pltpu.VMEM((LANES, HEADS_PER_STEP * blk), BF16),
            pltpu.VMEM((HEADS_PER_STEP, nb + SUBLANES, blk), F32),
            pltpu.VMEM((2, blk, HEADS_PER_STEP * blk), F32),
            pltpu.VMEM((2, SUBLANES, HEADS_PER_STEP * blk), F32),
            pltpu.VMEM((2, HEADS_PER_STEP, blk, blk), BF16),
            pltpu.VMEM((HEADS_PER_STEP, 1, blk), F32),
            pltpu.VMEM((2, HEADS_PER_STEP, 1, blk), F32),
            pltpu.VMEM((HEADS_PER_STEP, V_AUG, blk), F32),
        ],
        compiler_params=_params("parallel", "parallel", "arbitrary"),
        name="moba",
    )(mqv, mqv, mk, bias_tab, far_bias)


def _outproj_kernel(x_ref, lru_ref, gla_ref, moba_ref, wl_ref, wg_ref, wm_ref, g_ref, o_ref):
    y = jnp.dot(lru_ref[...], wl_ref[...], preferred_element_type=F32)
    y = y + jnp.dot(gla_ref[...], wg_ref[...], preferred_element_type=F32)
    y = y + jnp.dot(moba_ref[...], wm_ref[...], preferred_element_type=F32)
    o_ref[...] = x_ref[...] + _rms(y, g_ref[...])


def _outproj(x, y_lru, y_gla, y_moba, w_out, g, tt):
    t, d = x.shape
    wl = w_out[:LRU_W]
    wg = w_out[LRU_W:LRU_W + GLA_V_W]
    wm = w_out[LRU_W + GLA_V_W:]
    row = lambda w: pl.BlockSpec((tt, w), lambda i: (i, 0))
    return pl.pallas_call(
        _outproj_kernel,
        grid=(t // tt,),
        in_specs=[row(d), row(LRU_W), row(GLA_V_W), row(MOBA_W),
                  _const_spec(wl.shape), _const_spec(wg.shape), _const_spec(wm.shape),
                  _const_spec((1, d))],
        out_specs=row(d),
        out_shape=jax.ShapeDtypeStruct((t, d), F32),
        compiler_params=_params("parallel"),
        name="outproj",
    )(x, y_lru, y_gla, y_moba, wl, wg, wm, g)


def _ffn_kernel(x_ref, gpre_ref, wg_ref, wu_ref, wd_ref, gpost_ref, o_ref):
    x = x_ref[...]
    h = _rms(x, gpre_ref[...]).astype(BF16)
    a = jax.nn.silu(jnp.dot(h, wg_ref[...], preferred_element_type=F32))
    a = (a * jnp.dot(h, wu_ref[...], preferred_element_type=F32)).astype(BF16)
    f = jnp.dot(a, wd_ref[...], preferred_element_type=F32)
    o_ref[...] = x + _rms(f, gpost_ref[...])


def _ffn(x, gpre, wg, wu, wd, gpost, tt):
    t, d = x.shape
    dff = wg.shape[1]
    once = dict(pipeline_mode=pl.Buffered(1))
    return pl.pallas_call(
        _ffn_kernel,
        grid=(t // tt,),
        in_specs=[
            pl.BlockSpec((tt, d), lambda i: (i, 0)),
            _const_spec((1, d)),
            pl.BlockSpec((d, dff), lambda i: (0, 0), **once),
            pl.BlockSpec((d, dff), lambda i: (0, 0), **once),
            pl.BlockSpec((dff, d), lambda i: (0, 0), **once),
            _const_spec((1, d)),
        ],
        out_specs=pl.BlockSpec((tt, d), lambda i: (i, 0)),
        out_shape=jax.ShapeDtypeStruct((t, d), F32),
        compiler_params=_params("parallel"),
        name="ffn",
    )(x, gpre, wg, wu, wd, gpost)


def _block_diag(w):
    g, n, _ = w.shape
    eye = jnp.eye(g, dtype=w.dtype)
    return (eye[:, None, :, None] * w[:, :, None, :]).reshape(g * n, g * n)


def _split_w_in(w_in):
    sizes = (LRU_W, LRU_W, GLA_QK_W, GLA_QK_W, GLA_V_W, GLA_RANK, GLA_V_W, MOBA_W, MOBA_W, MOBA_W)
    offs = np.cumsum((0,) + sizes)
    col = lambda i: w_in[:, offs[i]:offs[i + 1]]
    lru_x, lru_g, gq, gk, gv, g_lr, g_out, mq, mk, mv = (col(i) for i in range(len(sizes)))
    g_lr = jnp.pad(g_lr, ((0, 0), (0, LANES - GLA_RANK)))
    w_nat = jnp.concatenate([lru_x, lru_g, gq, gk, gv, g_out, g_lr, mk], axis=1).astype(BF16)
    w_t = jnp.concatenate([mq, mv], axis=1).T.astype(BF16)
    return w_nat, w_t


def kernel(x, pre_mix_norm, post_mix_norm, pre_ffn_norm, post_ffn_norm, w_in, w_out, lru_conv_w, lru_conv_b, lru_wa, lru_ba, lru_wx, lru_bx, lru_lambda, gla_gate_w2, gla_gate_b, gla_norm, rel_bias, w_ffn_gate, w_ffn_up, w_ffn_down):
    b, s, d = x.shape
    assert s % MOBA_BLOCK == 0 and d == LRU_W + GLA_V_W + MOBA_W
    depth = w_in.shape[0]
    ts_proj = min(512, s)
    ts_seq = min(256, s)
    tt = min(512, b * s)
    row = lambda v: v.reshape(1, -1).astype(F32)

    bias_tab = _bias_tables(rel_bias)
    for l in range(depth):
        w_nat, w_t = _split_w_in(w_in[l])
        lru_in, gla_in, mk, mqv = _inproj(x, row(pre_mix_norm[l]), w_nat, w_t, ts_proj)
        y_lru = _lru(lru_in, lru_conv_w[l].astype(F32), row(lru_conv_b[l]),
                     _block_diag(lru_wa[l]).astype(BF16), row(lru_ba[l]),
                     _block_diag(lru_wx[l]).astype(BF16), row(lru_bx[l]),
                     row(lru_lambda[l]), ts_seq)
        w2 = jnp.pad(gla_gate_w2[l], ((0, LANES - GLA_RANK), (0, 0))).astype(BF16)
        y_gla = _gla(gla_in, w2, row(gla_gate_b[l]), row(gla_norm[l]), ts_seq)
        y_moba = _moba(mqv, mk, bias_tab, rel_bias[REL_BUCKETS - 1].astype(F32))
        x2 = _outproj(x.reshape(b * s, d), y_lru.reshape(b * s, -1), y_gla.reshape(b * s, -1),
                      y_moba.reshape(b * s, -1), w_out[l].astype(BF16), row(post_mix_norm[l]), tt)
        x2 = _ffn(x2, row(pre_ffn_norm[l]), w_ffn_gate[l].astype(BF16), w_ffn_up[l].astype(BF16),
                  w_ffn_down[l].astype(BF16), row(post_ffn_norm[l]), tt)
        x = x2.reshape(b, s, d)
    return x
```

```python
import functools
import math

import numpy as np
import jax
import jax.numpy as jnp
from jax import lax
from jax.experimental import pallas as pl
from jax.experimental.pallas import tpu as pltpu

F32 = jnp.float32
BF16 = jnp.bfloat16

LRU_W = 256
LRU_BLOCKS = 4
LRU_CONV = 4
LRU_C = 8.0
GLA_HEADS = 4
GLA_DV = 64
GLA_DK = 32
GLA_RANK = 16
GLA_TAU = 16.0
GLA_CHUNK = 64
GLA_QK_W = GLA_HEADS * GLA_DK
GLA_V_W = GLA_HEADS * GLA_DV
MOBA_HEADS = 8
MOBA_DH = 64
MOBA_BLOCK = 256
MOBA_TOPK = 3
MOBA_W = MOBA_HEADS * MOBA_DH
REL_BUCKETS = 32
REL_MAX_DIST = 128
RMS_EPS = 1e-6
NEG_INF = -1e30
LOG2E = math.log2(math.e)

LANES = 128
SUBLANES = 8
VMEM_LIMIT = 56 * 1024 * 1024

GLA_IN_W = GLA_QK_W * 2 + GLA_V_W * 2 + LANES
HEADS_PER_STEP = LANES // MOBA_DH

NT_DIMS = (((1,), (1,)), ((), ()))
TN_DIMS = (((0,), (0,)), ((), ()))


def _rms(x, g):
    return x * lax.rsqrt(jnp.mean(x * x, axis=-1, keepdims=True) + RMS_EPS) * g


def _split_hi_lo(x):
    hi = x.astype(BF16)
    return hi, (x - hi.astype(F32)).astype(BF16)


def _params(*sem):
    return pltpu.CompilerParams(dimension_semantics=sem, vmem_limit_bytes=VMEM_LIMIT)


def _const_spec(shape):
    return pl.BlockSpec(shape, lambda *_: (0,) * len(shape))


def _inproj_kernel(x_ref, g_ref, wn_ref, wt_ref, lru_ref, gla_ref, mk_ref, mqv_ref):
    h = _rms(x_ref[0], g_ref[...]).astype(BF16)
    pn = jnp.dot(h, wn_ref[...], preferred_element_type=F32)
    lru_ref[0] = pn[:, :2 * LRU_W]
    gla_ref[0] = pn[:, 2 * LRU_W:2 * LRU_W + GLA_IN_W]
    mk_ref[0] = pn[:, 2 * LRU_W + GLA_IN_W:].astype(BF16)
    pt = lax.dot_general(wt_ref[...], h, NT_DIMS, preferred_element_type=F32)
    mqv_ref[0, :MOBA_W, :] = (pt[:MOBA_W] * (MOBA_DH ** -0.5 * LOG2E)).astype(BF16)
    mqv_ref[0, MOBA_W:, :] = pt[MOBA_W:].astype(BF16)


def _inproj(x, g, wn, wt, ts):
    b, s, d = x.shape
    nn = wn.shape[1]
    return pl.pallas_call(
        _inproj_kernel,
        grid=(b, s // ts),
        in_specs=[
            pl.BlockSpec((1, ts, d), lambda i, j: (i, j, 0)),
            _const_spec((1, d)),
            _const_spec((d, nn)),
            _const_spec((2 * MOBA_W, d)),
        ],
        out_specs=[
            pl.BlockSpec((1, ts, 2 * LRU_W), lambda i, j: (i, j, 0)),
            pl.BlockSpec((1, ts, GLA_IN_W), lambda i, j: (i, j, 0)),
            pl.BlockSpec((1, ts, MOBA_W), lambda i, j: (i, j, 0)),
            pl.BlockSpec((1, 2 * MOBA_W, ts), lambda i, j: (i, 0, j)),
        ],
        out_shape=[
            jax.ShapeDtypeStruct((b, s, 2 * LRU_W), F32),
            jax.ShapeDtypeStruct((b, s, GLA_IN_W), F32),
            jax.ShapeDtypeStruct((b, s, MOBA_W), BF16),
            jax.ShapeDtypeStruct((b, 2 * MOBA_W, s), BF16),
        ],
        compiler_params=_params("parallel", "parallel"),
        name="inproj",
    )(x, g, wn, wt)


def _shift_rows(x, prev_tail, k, row8):
    sh = pltpu.roll(x, k, axis=0)
    top = jnp.where(row8 < k, pltpu.roll(prev_tail, k, axis=0), sh[:SUBLANES])
    return jnp.concatenate([top, sh[SUBLANES:]], axis=0)


def _lru_kernel(in_ref, cw_ref, cb_ref, wa_ref, ba_ref, wx_ref, bx_ref, lam_ref, y_ref,
                h_sc, tail_sc):
    j = pl.program_id(1)
    ts = in_ref.shape[1]

    @pl.when(j == 0)
    def _():
        h_sc[...] = jnp.zeros_like(h_sc)
        tail_sc[...] = jnp.zeros_like(tail_sc)

    xb = in_ref[0, :, :LRU_W]
    gb = in_ref[0, :, LRU_W:]
    row = lax.broadcasted_iota(jnp.int32, (ts, LRU_W), 0)
    row8 = lax.broadcasted_iota(jnp.int32, (SUBLANES, LRU_W), 0)
    tail = tail_sc[...]
    xc = xb * cw_ref[LRU_CONV - 1:LRU_CONV, :] + cb_ref[...]
    for k in range(1, LRU_CONV):
        xc = xc + _shift_rows(xb, tail, k, row8) * cw_ref[LRU_CONV - 1 - k:LRU_CONV - k, :]
    tail_sc[...] = xb[ts - SUBLANES:]

    xcb = xc.astype(BF16)
    r = jax.nn.sigmoid(jnp.dot(xcb, wa_ref[...], preferred_element_type=F32) + ba_ref[...])
    i = jax.nn.sigmoid(jnp.dot(xcb, wx_ref[...], preferred_element_type=F32) + bx_ref[...])
    log_a = -LRU_C * r * jax.nn.softplus(-lam_ref[...])
    a = jnp.exp(log_a)
    mult = jnp.sqrt(-jnp.tanh(log_a) * (a * a + 1.0))
    mult = jnp.where(row + j * ts == 0, 1.0, mult)
    u = mult * (i * xc)

    d = 1
    while d < ts:
        keep = row >= d
        a_sh = jnp.where(keep, pltpu.roll(a, d, axis=0), 1.0)
        u_sh = jnp.where(keep, pltpu.roll(u, d, axis=0), 0.0)
        u = a * u_sh + u
        a = a * a_sh
        d *= 2
    h = u + a * h_sc[...]
    h_sc[...] = h[ts - 1:ts]
    y_ref[0] = (h * jax.nn.gelu(gb)).astype(y_ref.dtype)


def _lru(lru_in, cw, cb, wa, ba, wx, bx, lam, ts):
    b, s, _ = lru_in.shape
    return pl.pallas_call(
        _lru_kernel,
        grid=(b, s // ts),
        in_specs=[
            pl.BlockSpec((1, ts, 2 * LRU_W), lambda i, j: (i, j, 0)),
            _const_spec((LRU_CONV, LRU_W)),
            _const_spec((1, LRU_W)),
            _const_spec((LRU_W, LRU_W)),
            _const_spec((1, LRU_W)),
            _const_spec((LRU_W, LRU_W)),
            _const_spec((1, LRU_W)),
            _const_spec((1, LRU_W)),
        ],
        out_specs=pl.BlockSpec((1, ts, LRU_W), lambda i, j: (i, j, 0)),
        out_shape=jax.ShapeDtypeStruct((b, s, LRU_W), BF16),
        scratch_shapes=[pltpu.VMEM((1, LRU_W), F32), pltpu.VMEM((SUBLANES, LRU_W), F32)],
        compiler_params=_params("parallel", "arbitrary"),
        name="rglru",
    )(lru_in, cw, cb, wa, ba, wx, bx, lam)


def _gla_kernel(in_ref, w2_ref, gb_ref, gain_ref, y_ref, st_sc):
    ts = in_ref.shape[1]
    c = GLA_CHUNK

    @pl.when(pl.program_id(1) == 0)
    def _():
        st_sc[...] = jnp.zeros_like(st_sc)

    q = in_ref[0, :, :GLA_QK_W]
    k = in_ref[0, :, GLA_QK_W:2 * GLA_QK_W]
    v = in_ref[0, :, 2 * GLA_QK_W:2 * GLA_QK_W + GLA_V_W]
    g_out = in_ref[0, :, 2 * GLA_QK_W + GLA_V_W:2 * GLA_QK_W + 2 * GLA_V_W]
    g_lr = in_ref[0, :, 2 * GLA_QK_W + 2 * GLA_V_W:]

    gate = jnp.dot(g_lr.astype(BF16), w2_ref[...], preferred_element_type=F32) + gb_ref[...]
    log_alpha = jax.nn.log_sigmoid(gate) / GLA_TAU

    ri = lax.broadcasted_iota(jnp.int32, (ts, ts), 0)
    ci = lax.broadcasted_iota(jnp.int32, (ts, ts), 1)
    in_chunk = (ri // c == ci // c) & (ci <= ri)
    tri = jnp.where(in_chunk, 1.0, 0.0).astype(BF16)
    la_hi, la_lo = _split_hi_lo(log_alpha)
    cum = (jnp.dot(tri, la_hi, preferred_element_type=F32)
           + jnp.dot(tri, la_lo, preferred_element_type=F32))

    qk_lane = lax.broadcasted_iota(jnp.int32, (1, GLA_QK_W), 1) // GLA_DK
    v_lane = lax.broadcasted_iota(jnp.int32, (1, GLA_V_W), 1) // GLA_DV
    st_row = lax.broadcasted_iota(jnp.int32, (GLA_V_W, GLA_QK_W), 0) // GLA_DV
    st_col = lax.broadcasted_iota(jnp.int32, (GLA_V_W, GLA_QK_W), 1) // GLA_DK
    same_head = st_row == st_col
    chunks = range(ts // c)
    rows = [slice(n * c, (n + 1) * c) for n in chunks]
    lasts = [cum[(n + 1) * c - 1:(n + 1) * c] for n in chunks]
    last_full = jnp.concatenate([jnp.broadcast_to(l, (c, GLA_QK_W)) for l in lasts], axis=0)

    q_d = (q * (GLA_DK ** -0.5) * jnp.exp(cum)).astype(BF16)
    k_d = (k * jnp.exp(-cum)).astype(BF16)
    k_l = (k * jnp.exp(last_full - cum)).astype(BF16)
    v_b = v.astype(BF16)

    o = jnp.zeros((ts, GLA_V_W), F32)
    for hh in range(GLA_HEADS):
        q_h = jnp.where(qk_lane == hh, q_d, jnp.zeros_like(q_d))
        attn = lax.dot_general(q_h, k_d, NT_DIMS, preferred_element_type=F32)
        attn = jnp.where(in_chunk, attn, 0.0).astype(BF16)
        v_h = jnp.where(v_lane == hh, v_b, jnp.zeros_like(v_b))
        o = o + jnp.dot(attn, v_h, preferred_element_type=F32)

    st = st_sc[...]
    inter = []
    for n in chunks:
        inter.append(lax.dot_general(q_d[rows[n]], st.astype(BF16), NT_DIMS,
                                     preferred_element_type=F32))
        kv = lax.dot_general(v_b[rows[n]], k_l[rows[n]], TN_DIMS, preferred_element_type=F32)
        st = st * jnp.exp(lasts[n]) + jnp.where(same_head, kv, 0.0)
    st_sc[...] = st
    o = o + jnp.concatenate(inter, axis=0)

    hr = lax.broadcasted_iota(jnp.int32, (GLA_V_W, GLA_V_W), 0) // GLA_DV
    hc = lax.broadcasted_iota(jnp.int32, (GLA_V_W, GLA_V_W), 1) // GLA_DV
    head_mean = jnp.where(hr == hc, 1.0 / GLA_DV, 0.0).astype(BF16)
    sq_hi, sq_lo = _split_hi_lo(o * o)
    ms = (jnp.dot(sq_hi, head_mean, preferred_element_type=F32)
          + jnp.dot(sq_lo, head_mean, preferred_element_type=F32))
    y = o * lax.rsqrt(ms + RMS_EPS) * gain_ref[...] * jax.nn.silu(g_out)
    y_ref[0] = y.astype(y_ref.dtype)


def _gla(gla_in, w2, gb, gain, ts):
    b, s, _ = gla_in.shape
    return pl.pallas_call(
        _gla_kernel,
        grid=(b, s // ts),
        in_specs=[
            pl.BlockSpec((1, ts, GLA_IN_W), lambda i, j: (i, j, 0)),
            _const_spec((LANES, GLA_QK_W)),
            _const_spec((1, GLA_QK_W)),
            _const_spec((1, GLA_V_W)),
        ],
        out_specs=pl.BlockSpec((1, ts, GLA_V_W), lambda i, j: (i, j, 0)),
        out_shape=jax.ShapeDtypeStruct((b, s, GLA_V_W), BF16),
        scratch_shapes=[pltpu.VMEM((GLA_V_W, GLA_QK_W), F32)],
        compiler_params=_params("parallel", "arbitrary"),
        name="gla",
    )(gla_in, w2, gb, gain)


def _t5_bucket_np(rel):
    n = np.maximum(rel, 0)
    max_exact = REL_BUCKETS // 2
    nf = np.maximum(n, 1).astype(np.float32)
    large = max_exact + (np.log(nf / np.float32(max_exact)) / np.float32(math.log(REL_MAX_DIST / max_exact))
                         * np.float32(REL_BUCKETS - max_exact)).astype(np.int32)
    large = np.minimum(large, REL_BUCKETS - 1)
    return np.where(n < max_exact, n, large).astype(np.int32)


FAR_DIST = int(np.argmax(_t5_bucket_np(np.arange(4 * REL_MAX_DIST)) == REL_BUCKETS - 1))
assert np.all(_t5_bucket_np(np.arange(FAR_DIST, 1 << 16)) == REL_BUCKETS - 1)
assert FAR_DIST <= MOBA_BLOCK

N_TABLES = 2
V_AUG = MOBA_DH + 16
GATE_COLS = 1024


def _bias_kernel(bkt_ref, rb_ref, out_ref):
    h = pl.program_id(0)
    for t in range(2):
        bkt = bkt_ref[t]
        acc = jnp.zeros(bkt.shape, F32)
        for bb in range(REL_BUCKETS):
            acc = jnp.where(bkt == bb, rb_ref[bb, h], acc)
        out_ref[0, t] = jnp.where(bkt < 0, NEG_INF, acc * LOG2E)


def _bias_tables(rel_bias):
    ko = np.arange(MOBA_BLOCK)[:, None]
    qo = np.arange(MOBA_BLOCK)[None, :]
    own = np.where(qo >= ko, _t5_bucket_np(qo - ko), -1)
    bkt = np.stack([own, _t5_bucket_np(qo - ko + MOBA_BLOCK)]).astype(np.int32)
    return pl.pallas_call(
        _bias_kernel,
        grid=(MOBA_HEADS,),
        in_specs=[
            _const_spec((2, MOBA_BLOCK, MOBA_BLOCK)),
            pl.BlockSpec(memory_space=pltpu.SMEM),
        ],
        out_specs=pl.BlockSpec((1, N_TABLES, MOBA_BLOCK, MOBA_BLOCK), lambda h: (h, 0, 0, 0)),
        out_shape=jax.ShapeDtypeStruct((MOBA_HEADS, N_TABLES, MOBA_BLOCK, MOBA_BLOCK), F32),
        compiler_params=_params("parallel"),
        name="moba_bias",
    )(jnp.asarray(bkt), rel_bias.astype(F32))


PIPE = 8


def _moba_schedules(nb):
    near = [(n, n, n, 0) for n in range(nb)] + [(n, n - 1, n - 1, 1) for n in range(1, nb)]
    far = [(n, j, j, 0) for j in range(nb - 2) for n in range(j + 2, nb)]

    def table(tiles):
        length = PIPE + PIPE * (-(-len(tiles) // PIPE))
        rows = tiles + [(0, 0, nb, 0)] * (length - len(tiles))
        return np.asarray(rows, np.int32).T.copy(), len(tiles)

    return table(near), table(far)


def _moba_kernel(near_ref, far_ref, farb_ref, q_ref, v_ref, k_ref, bias_ref, y_ref,
                 vaug_sc, mb_sc, s_sc, cmax_sc, p_sc, alpha_sc, m_sc, acc_sc, *, n_near, n_far):
    hp = pl.program_id(1)
    blk = MOBA_BLOCK
    seq = k_ref.shape[1]
    nb = seq // blk
    heads = range(HEADS_PER_STEP)
    lag = PIPE // 2

    ones_row = lax.broadcasted_iota(jnp.int32, (V_AUG - MOBA_DH, seq), 0) == 0
    kmean = jnp.concatenate(
        [jnp.mean(k_ref[0, jb * blk:(jb + 1) * blk, :].astype(F32), axis=0, keepdims=True)
         for jb in range(nb)], axis=0)
    lane_head = lax.broadcasted_iota(jnp.int32, (nb, LANES), 1) // MOBA_DH
    for hh in heads:
        vaug_sc[hh, :MOBA_DH, :] = v_ref[0, hh * MOBA_DH:(hh + 1) * MOBA_DH, :]
        vaug_sc[hh, MOBA_DH:, :] = jnp.where(ones_row, 1.0, 0.0).astype(BF16)
        mb_sc[hh, nb:, :] = jnp.full((mb_sc.shape[1] - nb, seq), NEG_INF, F32)
        m_sc[hh] = jnp.full(m_sc.shape[1:], NEG_INF, F32)
        acc_sc[hh] = jnp.zeros(acc_sc.shape[1:], F32)
        km_hi, km_lo = _split_hi_lo(jnp.where(lane_head == hh, kmean, 0.0))
        far_bias = farb_ref[hp * HEADS_PER_STEP + hh] * LOG2E
        for c0 in range(0, seq, GATE_COLS):
            cw = min(GATE_COLS, seq - c0)
            qc = q_ref[0, :, c0:c0 + cw]
            g = (jnp.dot(km_hi, qc, preferred_element_type=F32)
                 + jnp.dot(km_lo, qc, preferred_element_type=F32))
            key_blk = lax.broadcasted_iota(jnp.int32, (nb, cw), 0)
            qry_blk = (lax.broadcasted_iota(jnp.int32, (nb, cw), 1) + c0) // blk
            g = jnp.where(key_blk < qry_blk, g, -jnp.inf)
            rank = jnp.zeros((nb, cw), F32)
            for jb in range(nb):
                gj = g[jb:jb + 1, :]
                rank = rank + jnp.where(gj > g, 1.0, 0.0)
                rank = rank + jnp.where(gj == g, jnp.where(key_blk > jb, 1.0, 0.0), 0.0)
            keep = ((key_blk < qry_blk) & (rank < MOBA_TOPK)) | (key_blk == qry_blk)
            mb_sc[hh, :nb, c0:c0 + cw] = jnp.where(
                keep, jnp.where(key_blk < qry_blk - 1, far_bias, 0.0), NEG_INF)

    feat = lax.broadcasted_iota(jnp.int32, (LANES, blk), 0) // MOBA_DH

    def run_pass(sched_ref, n_tiles, with_table):
        def col(idx, r):
            return pl.multiple_of(sched_ref[r, idx] * blk, blk)

        def logits(idx, slot):
            qb = q_ref[0, :, pl.ds(col(idx, 0), blk)]
            qz = jnp.concatenate(
                [jnp.where(feat == hh, qb, jnp.zeros_like(qb)) for hh in heads], axis=1)
            s = jnp.dot(k_ref[0, pl.ds(col(idx, 1), blk), :], qz, preferred_element_type=F32)
            s_sc[slot] = s
            if not with_table:
                cmax_sc[slot] = jnp.max(s.reshape(blk // SUBLANES, SUBLANES, s.shape[1]), axis=0)

        def softmax_update(idx, slot):
            n = sched_ref[0, idx]
            qcol = col(idx, 0)
            row = sched_ref[2, idx]
            for hh in heads:
                cols = slice(hh * blk, (hh + 1) * blk)
                if with_table:
                    s = s_sc[slot, :, cols] + bias_ref[hh, sched_ref[3, idx]]
                    cm = jnp.max(s, axis=0, keepdims=True)
                else:
                    s = s_sc[slot, :, cols]
                    cm = jnp.max(cmax_sc[slot, :, cols], axis=0, keepdims=True)
                mb = mb_sc[hh, pl.ds(row, 1), pl.ds(qcol, blk)]
                m_old = m_sc[hh, n]
                m_new = jnp.maximum(m_old, cm + mb)
                alpha_sc[slot, hh] = jnp.exp2(m_old - m_new)
                m_sc[hh, n] = m_new
                p_sc[slot, hh] = jnp.exp2(s + (mb - m_new)).astype(BF16)

        def values(idx, slot):
            n = sched_ref[0, idx]
            kcol = col(idx, 1)
            for hh in heads:
                pv = jnp.dot(vaug_sc[hh, :, pl.ds(kcol, blk)], p_sc[slot, hh],
                             preferred_element_type=F32)
                acc_sc[hh, n] = alpha_sc[slot, hh] * acc_sc[hh, n] + pv

        for i in range(PIPE):
            logits(i, i)
            if i >= lag:
                softmax_update(i - lag, i - lag)

        def body(d, carry):
            for u in range(PIPE):
                i = PIPE + PIPE * d + u
                logits(i, u)
                softmax_update(i - lag, (u + lag) % PIPE)
                values(i - PIPE, u)
            return carry

        lax.fori_loop(0, -(-n_tiles // PIPE), body, 0)

    run_pass(near_ref, n_near, True)
    run_pass(far_ref, n_far, False)

    for n in range(nb):
        out_t = jnp.concatenate(
            [acc_sc[hh, n, :MOBA_DH, :] * (1.0 / acc_sc[hh, n, MOBA_DH:MOBA_DH + 1, :])
             for hh in heads], axis=0)
        y_ref[0, n * blk:(n + 1) * blk, :] = out_t.T.astype(y_ref.dtype)


def _moba(mqv, mk, bias_tab, far_bias):
    b, s, _ = mk.shape
    blk = MOBA_BLOCK
    nb = s // blk
    npair = MOBA_HEADS // HEADS_PER_STEP
    (near, n_near), (far, n_far) = _moba_schedules(nb)
    smem = pl.BlockSpec(memory_space=pltpu.SMEM)
    return pl.pallas_call(
        functools.partial(_moba_kernel, n_near=n_near, n_far=n_far),
        grid=(b, npair),
        in_specs=[
            smem, smem, smem,
            pl.BlockSpec((1, LANES, s), lambda i, p: (i, p, 0)),
            pl.BlockSpec((1, LANES, s), lambda i, p: (i, npair + p, 0)),
            pl.BlockSpec((1, s, LANES), lambda i, p: (i, 0, p)),
            pl.BlockSpec((HEADS_PER_STEP, N_TABLES, blk, blk), lambda i, p: (p, 0, 0, 0)),
        ],
        out_specs=pl.BlockSpec((1, s, LANES), lambda i, p: (i, 0, p)),
        out_shape=jax.ShapeDtypeStruct((b, s, MOBA_W), BF16),
        scratch_shapes=[
            pltpu.VMEM((HEADS_PER_STEP, V_AUG, s), BF16),
            pltpu.VMEM((HEADS_PER_STEP, nb + SUBLANES, s), F32),
            pltpu.VMEM((PIPE, blk, HEADS_PER_STEP * blk), F32),
            pltpu.VMEM((PIPE, SUBLANES, HEADS_PER_STEP * blk), F32),
            pltpu.VMEM((PIPE, HEADS_PER_STEP, blk, blk), BF16),
            pltpu.VMEM((PIPE, HEADS_PER_STEP, 1, blk), F32),
            pltpu.VMEM((HEADS_PER_STEP, nb, 1, blk), F32),
            pltpu.VMEM((HEADS_PER_STEP, nb, V_AUG, blk), F32),
        ],
        compiler_params=_params("parallel", "parallel"),
        name="moba",
    )(jnp.asarray(near), jnp.asarray(far), far_bias, mqv, mqv, mk, bias_tab)


def _outproj_kernel(x_ref, lru_ref, gla_ref, moba_ref, wl_ref, wg_ref, wm_ref, g_ref, o_ref):
    y = jnp.dot(lru_ref[...], wl_ref[...], preferred_element_type=F32)
    y = y + jnp.dot(gla_ref[...], wg_ref[...], preferred_element_type=F32)
    y = y + jnp.dot(moba_ref[...], wm_ref[...], preferred_element_type=F32)
    o_ref[...] = x_ref[...] + _rms(y, g_ref[...])


def _outproj(x, y_lru, y_gla, y_moba, w_out, g, tt):
    t, d = x.shape
    wl = w_out[:LRU_W]
    wg = w_out[LRU_W:LRU_W + GLA_V_W]
    wm = w_out[LRU_W + GLA_V_W:]
    row = lambda w: pl.BlockSpec((tt, w), lambda i: (i, 0))
    return pl.pallas_call(
        _outproj_kernel,
        grid=(t // tt,),
        in_specs=[row(d), row(LRU_W), row(GLA_V_W), row(MOBA_W),
                  _const_spec(wl.shape), _const_spec(wg.shape), _const_spec(wm.shape),
                  _const_spec((1, d))],
        out_specs=row(d),
        out_shape=jax.ShapeDtypeStruct((t, d), F32),
        compiler_params=_params("parallel"),
        name="outproj",
    )(x, y_lru, y_gla, y_moba, wl, wg, wm, g)


def _ffn_kernel(x_ref, gpre_ref, wg_ref, wu_ref, wd_ref, gpost_ref, o_ref):
    x = x_ref[...]
    h = _rms(x, gpre_ref[...]).astype(BF16)
    a = jax.nn.silu(jnp.dot(h, wg_ref[...], preferred_element_type=F32))
    a = (a * jnp.dot(h, wu_ref[...], preferred_element_type=F32)).astype(BF16)
    f = jnp.dot(a, wd_ref[...], preferred_element_type=F32)
    o_ref[...] = x + _rms(f, gpost_ref[...])


def _ffn(x, gpre, wg, wu, wd, gpost, tt):
    t, d = x.shape
    dff = wg.shape[1]
    once = dict(pipeline_mode=pl.Buffered(1))
    return pl.pallas_call(
        _ffn_kernel,
        grid=(t // tt,),
        in_specs=[
            pl.BlockSpec((tt, d), lambda i: (i, 0)),
            _const_spec((1, d)),
            pl.BlockSpec((d, dff), lambda i: (0, 0), **once),
            pl.BlockSpec((d, dff), lambda i: (0, 0), **once),
            pl.BlockSpec((dff, d), lambda i: (0, 0), **once),
            _const_spec((1, d)),
        ],
        out_specs=pl.BlockSpec((tt, d), lambda i: (i, 0)),
        out_shape=jax.ShapeDtypeStruct((t, d), F32),
        compiler_params=_params("parallel"),
        name="ffn",
    )(x, gpre, wg, wu, wd, gpost)


def _block_diag(w):
    g, n, _ = w.shape
    eye = jnp.eye(g, dtype=w.dtype)
    return (eye[:, None, :, None] * w[:, :, None, :]).reshape(g * n, g * n)


def _split_w_in(w_in):
    sizes = (LRU_W, LRU_W, GLA_QK_W, GLA_QK_W, GLA_V_W, GLA_RANK, GLA_V_W, MOBA_W, MOBA_W, MOBA_W)
    offs = np.cumsum((0,) + sizes)
    col = lambda i: w_in[:, offs[i]:offs[i + 1]]
    lru_x, lru_g, gq, gk, gv, g_lr, g_out, mq, mk, mv = (col(i) for i in range(len(sizes)))
    g_lr = jnp.pad(g_lr, ((0, 0), (0, LANES - GLA_RANK)))
    w_nat = jnp.concatenate([lru_x, lru_g, gq, gk, gv, g_out, g_lr, mk], axis=1).astype(BF16)
    w_t = jnp.concatenate([mq, mv], axis=1).T.astype(BF16)
    return w_nat, w_t


def kernel(x, pre_mix_norm, post_mix_norm, pre_ffn_norm, post_ffn_norm, w_in, w_out, lru_conv_w, lru_conv_b, lru_wa, lru_ba, lru_wx, lru_bx, lru_lambda, gla_gate_w2, gla_gate_b, gla_norm, rel_bias, w_ffn_gate, w_ffn_up, w_ffn_down):
    b, s, d = x.shape
    assert s % MOBA_BLOCK == 0 and d == LRU_W + GLA_V_W + MOBA_W
    depth = w_in.shape[0]
    ts_proj = min(512, s)
    ts_seq = min(256, s)
    tt = min(512, b * s)
    row = lambda v: v.reshape(1, -1).astype(F32)

    bias_tab = _bias_tables(rel_bias)
    for l in range(depth):
        w_nat, w_t = _split_w_in(w_in[l])
        lru_in, gla_in, mk, mqv = _inproj(x, row(pre_mix_norm[l]), w_nat, w_t, ts_proj)
        y_lru = _lru(lru_in, lru_conv_w[l].astype(F32), row(lru_conv_b[l]),
                     _block_diag(lru_wa[l]).astype(BF16), row(lru_ba[l]),
                     _block_diag(lru_wx[l]).astype(BF16), row(lru_bx[l]),
                     row(lru_lambda[l]), ts_seq)
        w2 = jnp.pad(gla_gate_w2[l], ((0, LANES - GLA_RANK), (0, 0))).astype(BF16)
        y_gla = _gla(gla_in, w2, row(gla_gate_b[l]), row(gla_norm[l]), ts_seq)
        y_moba = _moba(mqv, mk, bias_tab, rel_bias[REL_BUCKETS - 1].astype(F32))
        x2 = _outproj(x.reshape(b * s, d), y_lru.reshape(b * s, -1), y_gla.reshape(b * s, -1),
                      y_moba.reshape(b * s, -1), w_out[l].astype(BF16), row(post_mix_norm[l]), tt)
        x2 = _ffn(x2, row(pre_ffn_norm[l]), w_ffn_gate[l].astype(BF16), w_ffn_up[l].astype(BF16),
                  w_ffn_down[l].astype(BF16), row(post_ffn_norm[l]), tt)
        x = x2.reshape(b, s, d)
    return x
```

```python
import functools
import math

import numpy as np
import jax
import jax.numpy as jnp
from jax import lax
from jax.experimental import pallas as pl
from jax.experimental.pallas import tpu as pltpu

F32 = jnp.float32
BF16 = jnp.bfloat16

LRU_W = 256
LRU_BLOCKS = 4
LRU_CONV = 4
LRU_C = 8.0
GLA_HEADS = 4
GLA_DV = 64
GLA_DK = 32
GLA_RANK = 16
GLA_TAU = 16.0
GLA_CHUNK = 64
GLA_QK_W = GLA_HEADS * GLA_DK
GLA_V_W = GLA_HEADS * GLA_DV
MOBA_HEADS = 8
MOBA_DH = 64
MOBA_BLOCK = 256
MOBA_TOPK = 3
MOBA_W = MOBA_HEADS * MOBA_DH
REL_BUCKETS = 32
REL_MAX_DIST = 128
RMS_EPS = 1e-6
NEG_INF = -1e30
LOG2E = math.log2(math.e)

LANES = 128
SUBLANES = 8
VMEM_LIMIT = 56 * 1024 * 1024

GLA_IN_W = GLA_QK_W * 2 + GLA_V_W * 2 + LANES
HEADS_PER_STEP = LANES // MOBA_DH

NT_DIMS = (((1,), (1,)), ((), ()))
TN_DIMS = (((0,), (0,)), ((), ()))


def _rms(x, g):
    return x * lax.rsqrt(jnp.mean(x * x, axis=-1, keepdims=True) + RMS_EPS) * g


def _split_hi_lo(x):
    hi = x.astype(BF16)
    return hi, (x - hi.astype(F32)).astype(BF16)


def _params(*sem):
    return pltpu.CompilerParams(dimension_semantics=sem, vmem_limit_bytes=VMEM_LIMIT)


def _const_spec(shape):
    return pl.BlockSpec(shape, lambda *_: (0,) * len(shape))


def _inproj_kernel(x_ref, g_ref, wn_ref, wt_ref, lru_ref, gla_ref, mk_ref, mqv_ref):
    h = _rms(x_ref[0], g_ref[...]).astype(BF16)
    pn = jnp.dot(h, wn_ref[...], preferred_element_type=F32)
    lru_ref[0] = pn[:, :2 * LRU_W]
    gla_ref[0] = pn[:, 2 * LRU_W:2 * LRU_W + GLA_IN_W]
    mk_ref[0] = pn[:, 2 * LRU_W + GLA_IN_W:].astype(BF16)
    pt = lax.dot_general(wt_ref[...], h, NT_DIMS, preferred_element_type=F32)
    mqv_ref[0, :MOBA_W, :] = (pt[:MOBA_W] * (MOBA_DH ** -0.5 * LOG2E)).astype(BF16)
    mqv_ref[0, MOBA_W:, :] = pt[MOBA_W:].astype(BF16)


def _inproj(x, g, wn, wt, ts):
    b, s, d = x.shape
    nn = wn.shape[1]
    return pl.pallas_call(
        _inproj_kernel,
        grid=(b, s // ts),
        in_specs=[
            pl.BlockSpec((1, ts, d), lambda i, j: (i, j, 0)),
            _const_spec((1, d)),
            _const_spec((d, nn)),
            _const_spec((2 * MOBA_W, d)),
        ],
        out_specs=[
            pl.BlockSpec((1, ts, 2 * LRU_W), lambda i, j: (i, j, 0)),
            pl.BlockSpec((1, ts, GLA_IN_W), lambda i, j: (i, j, 0)),
            pl.BlockSpec((1, ts, MOBA_W), lambda i, j: (i, j, 0)),
            pl.BlockSpec((1, 2 * MOBA_W, ts), lambda i, j: (i, 0, j)),
        ],
        out_shape=[
            jax.ShapeDtypeStruct((b, s, 2 * LRU_W), F32),
            jax.ShapeDtypeStruct((b, s, GLA_IN_W), F32),
            jax.ShapeDtypeStruct((b, s, MOBA_W), BF16),
            jax.ShapeDtypeStruct((b, 2 * MOBA_W, s), BF16),
        ],
        compiler_params=_params("parallel", "parallel"),
        name="inproj",
    )(x, g, wn, wt)


def _shift_rows(x, prev_tail, k, row8):
    sh = pltpu.roll(x, k, axis=0)
    top = jnp.where(row8 < k, pltpu.roll(prev_tail, k, axis=0), sh[:SUBLANES])
    return jnp.concatenate([top, sh[SUBLANES:]], axis=0)


def _lru_kernel(in_ref, cw_ref, cb_ref, wa_ref, ba_ref, wx_ref, bx_ref, lam_ref, y_ref,
                h_sc, tail_sc):
    j = pl.program_id(1)
    ts = in_ref.shape[1]

    @pl.when(j == 0)
    def _():
        h_sc[...] = jnp.zeros_like(h_sc)
        tail_sc[...] = jnp.zeros_like(tail_sc)

    row = lax.broadcasted_iota(jnp.int32, (ts, LRU_W), 0)
    row8 = lax.broadcasted_iota(jnp.int32, (SUBLANES, LRU_W), 0)
    for bb in range(in_ref.shape[0]):
        xb = in_ref[bb, :, :LRU_W]
        gb = in_ref[bb, :, LRU_W:]
        tail = tail_sc[bb]
        xc = xb * cw_ref[LRU_CONV - 1:LRU_CONV, :] + cb_ref[...]
        for k in range(1, LRU_CONV):
            xc = xc + _shift_rows(xb, tail, k, row8) * cw_ref[LRU_CONV - 1 - k:LRU_CONV - k, :]
        tail_sc[bb] = xb[ts - SUBLANES:]

        xcb = xc.astype(BF16)
        r = jax.nn.sigmoid(jnp.dot(xcb, wa_ref[...], preferred_element_type=F32) + ba_ref[...])
        i = jax.nn.sigmoid(jnp.dot(xcb, wx_ref[...], preferred_element_type=F32) + bx_ref[...])
        log_a = -LRU_C * r * jax.nn.softplus(-lam_ref[...])
        a = jnp.exp(log_a)
        mult = jnp.sqrt(-jnp.tanh(log_a) * (a * a + 1.0))
        mult = jnp.where(row + j * ts == 0, 1.0, mult)
        u = mult * (i * xc)

        d = 1
        while d < ts:
            keep = row >= d
            a_sh = jnp.where(keep, pltpu.roll(a, d, axis=0), 1.0)
            u_sh = jnp.where(keep, pltpu.roll(u, d, axis=0), 0.0)
            u = a * u_sh + u
            a = a * a_sh
            d *= 2
        h = u + a * h_sc[bb]
        h_sc[bb] = h[ts - 1:ts]
        y_ref[bb] = (h * jax.nn.gelu(gb)).astype(y_ref.dtype)


def _lru(lru_in, cw, cb, wa, ba, wx, bx, lam, ts, nbat):
    b, s, _ = lru_in.shape
    return pl.pallas_call(
        _lru_kernel,
        grid=(b // nbat, s // ts),
        in_specs=[
            pl.BlockSpec((nbat, ts, 2 * LRU_W), lambda i, j: (i, j, 0)),
            _const_spec((LRU_CONV, LRU_W)),
            _const_spec((1, LRU_W)),
            _const_spec((LRU_W, LRU_W)),
            _const_spec((1, LRU_W)),
            _const_spec((LRU_W, LRU_W)),
            _const_spec((1, LRU_W)),
            _const_spec((1, LRU_W)),
        ],
        out_specs=pl.BlockSpec((nbat, ts, LRU_W), lambda i, j: (i, j, 0)),
        out_shape=jax.ShapeDtypeStruct((b, s, LRU_W), BF16),
        scratch_shapes=[pltpu.VMEM((nbat, 1, LRU_W), F32),
                        pltpu.VMEM((nbat, SUBLANES, LRU_W), F32)],
        compiler_params=_params("parallel", "arbitrary"),
        name="rglru",
    )(lru_in, cw, cb, wa, ba, wx, bx, lam)


def _gla_kernel(in_ref, w2_ref, gb_ref, gain_ref, y_ref, st_sc):
    ts = in_ref.shape[1]
    c = GLA_CHUNK

    @pl.when(pl.program_id(1) == 0)
    def _():
        st_sc[...] = jnp.zeros_like(st_sc)

    ri = lax.broadcasted_iota(jnp.int32, (ts, ts), 0)
    ci = lax.broadcasted_iota(jnp.int32, (ts, ts), 1)
    in_chunk = (ri // c == ci // c) & (ci <= ri)
    tri = jnp.where(in_chunk, 1.0, 0.0).astype(BF16)
    qk_lane = lax.broadcasted_iota(jnp.int32, (1, GLA_QK_W), 1) // GLA_DK
    v_lane = lax.broadcasted_iota(jnp.int32, (1, GLA_V_W), 1) // GLA_DV
    st_row = lax.broadcasted_iota(jnp.int32, (GLA_V_W, GLA_QK_W), 0) // GLA_DV
    st_col = lax.broadcasted_iota(jnp.int32, (GLA_V_W, GLA_QK_W), 1) // GLA_DK
    same_head = st_row == st_col
    hr = lax.broadcasted_iota(jnp.int32, (GLA_V_W, GLA_V_W), 0) // GLA_DV
    hc = lax.broadcasted_iota(jnp.int32, (GLA_V_W, GLA_V_W), 1) // GLA_DV
    head_mean = jnp.where(hr == hc, 1.0 / GLA_DV, 0.0).astype(BF16)
    chunks = range(ts // c)
    rows = [slice(n * c, (n + 1) * c) for n in chunks]

    for bb in range(in_ref.shape[0]):
        q = in_ref[bb, :, :GLA_QK_W]
        k = in_ref[bb, :, GLA_QK_W:2 * GLA_QK_W]
        v = in_ref[bb, :, 2 * GLA_QK_W:2 * GLA_QK_W + GLA_V_W]
        g_out = in_ref[bb, :, 2 * GLA_QK_W + GLA_V_W:2 * GLA_QK_W + 2 * GLA_V_W]
        g_lr = in_ref[bb, :, 2 * GLA_QK_W + 2 * GLA_V_W:]

        gate = jnp.dot(g_lr.astype(BF16), w2_ref[...], preferred_element_type=F32) + gb_ref[...]
        log_alpha = jax.nn.log_sigmoid(gate) / GLA_TAU
        la_hi, la_lo = _split_hi_lo(log_alpha)
        cum = (jnp.dot(tri, la_hi, preferred_element_type=F32)
               + jnp.dot(tri, la_lo, preferred_element_type=F32))
        lasts = [cum[(n + 1) * c - 1:(n + 1) * c] for n in chunks]
        last_full = jnp.concatenate([jnp.broadcast_to(l, (c, GLA_QK_W)) for l in lasts], axis=0)

        q_d = (q * (GLA_DK ** -0.5) * jnp.exp(cum)).astype(BF16)
        k_d = (k * jnp.exp(-cum)).astype(BF16)
        k_l = (k * jnp.exp(last_full - cum)).astype(BF16)
        v_b = v.astype(BF16)

        o = jnp.zeros((ts, GLA_V_W), F32)
        for hh in range(GLA_HEADS):
            q_h = jnp.where(qk_lane == hh, q_d, jnp.zeros_like(q_d))
            attn = lax.dot_general(q_h, k_d, NT_DIMS, preferred_element_type=F32)
            attn = jnp.where(in_chunk, attn, 0.0).astype(BF16)
            v_h = jnp.where(v_lane == hh, v_b, jnp.zeros_like(v_b))
            o = o + jnp.dot(attn, v_h, preferred_element_type=F32)

        st = st_sc[bb]
        inter = []
        for n in chunks:
            inter.append(lax.dot_general(q_d[rows[n]], st.astype(BF16), NT_DIMS,
                                         preferred_element_type=F32))
            kv = lax.dot_general(v_b[rows[n]], k_l[rows[n]], TN_DIMS,
                                 preferred_element_type=F32)
            st = st * jnp.exp(lasts[n]) + jnp.where(same_head, kv, 0.0)
        st_sc[bb] = st
        o = o + jnp.concatenate(inter, axis=0)

        sq_hi, sq_lo = _split_hi_lo(o * o)
        ms = (jnp.dot(sq_hi, head_mean, preferred_element_type=F32)
              + jnp.dot(sq_lo, head_mean, preferred_element_type=F32))
        y = o * lax.rsqrt(ms + RMS_EPS) * gain_ref[...] * jax.nn.silu(g_out)
        y_ref[bb] = y.astype(y_ref.dtype)


def _gla(gla_in, w2, gb, gain, ts, nbat):
    b, s, _ = gla_in.shape
    return pl.pallas_call(
        _gla_kernel,
        grid=(b // nbat, s // ts),
        in_specs=[
            pl.BlockSpec((nbat, ts, GLA_IN_W), lambda i, j: (i, j, 0)),
            _const_spec((LANES, GLA_QK_W)),
            _const_spec((1, GLA_QK_W)),
            _const_spec((1, GLA_V_W)),
        ],
        out_specs=pl.BlockSpec((nbat, ts, GLA_V_W), lambda i, j: (i, j, 0)),
        out_shape=jax.ShapeDtypeStruct((b, s, GLA_V_W), BF16),
        scratch_shapes=[pltpu.VMEM((nbat, GLA_V_W, GLA_QK_W), F32)],
        compiler_params=_params("parallel", "arbitrary"),
        name="gla",
    )(gla_in, w2, gb, gain)


def _t5_bucket_np(rel):
    n = np.maximum(rel, 0)
    max_exact = REL_BUCKETS // 2
    nf = np.maximum(n, 1).astype(np.float32)
    large = max_exact + (np.log(nf / np.float32(max_exact)) / np.float32(math.log(REL_MAX_DIST / max_exact))
                         * np.float32(REL_BUCKETS - max_exact)).astype(np.int32)
    large = np.minimum(large, REL_BUCKETS - 1)
    return np.where(n < max_exact, n, large).astype(np.int32)


FAR_DIST = int(np.argmax(_t5_bucket_np(np.arange(4 * REL_MAX_DIST)) == REL_BUCKETS - 1))
assert np.all(_t5_bucket_np(np.arange(FAR_DIST, 1 << 16)) == REL_BUCKETS - 1)
assert FAR_DIST <= MOBA_BLOCK

N_TABLES = 2
V_AUG = MOBA_DH + 16
GATE_COLS = 1024


def _bias_kernel(bkt_ref, rb_ref, out_ref):
    h = pl.program_id(0)
    for t in range(2):
        bkt = bkt_ref[t]
        acc = jnp.zeros(bkt.shape, F32)
        for bb in range(REL_BUCKETS):
            acc = jnp.where(bkt == bb, rb_ref[bb, h], acc)
        out_ref[0, t] = jnp.where(bkt < 0, NEG_INF, acc * LOG2E)


def _bias_tables(rel_bias):
    ko = np.arange(MOBA_BLOCK)[:, None]
    qo = np.arange(MOBA_BLOCK)[None, :]
    own = np.where(qo >= ko, _t5_bucket_np(qo - ko), -1)
    bkt = np.stack([own, _t5_bucket_np(qo - ko + MOBA_BLOCK)]).astype(np.int32)
    return pl.pallas_call(
        _bias_kernel,
        grid=(MOBA_HEADS,),
        in_specs=[
            _const_spec((2, MOBA_BLOCK, MOBA_BLOCK)),
            pl.BlockSpec(memory_space=pltpu.SMEM),
        ],
        out_specs=pl.BlockSpec((1, N_TABLES, MOBA_BLOCK, MOBA_BLOCK), lambda h: (h, 0, 0, 0)),
        out_shape=jax.ShapeDtypeStruct((MOBA_HEADS, N_TABLES, MOBA_BLOCK, MOBA_BLOCK), F32),
        compiler_params=_params("parallel"),
        name="moba_bias",
    )(jnp.asarray(bkt), rel_bias.astype(F32))


PIPE = 8


def _moba_schedules(nb):
    near = [(n, n, n, 0) for n in range(nb)] + [(n, n - 1, n - 1, 1) for n in range(1, nb)]
    far = [(n, j, j, 0) for j in range(nb - 2) for n in range(j + 2, nb)]

    def table(tiles):
        length = PIPE + PIPE * (-(-len(tiles) // PIPE))
        rows = tiles + [(0, 0, nb, 0)] * (length - len(tiles))
        return np.asarray(rows, np.int32).T.copy(), len(tiles)

    return table(near), table(far)


def _moba_kernel(near_ref, far_ref, farb_ref, q_ref, v_ref, k_ref, bias_ref, y_ref,
                 vaug_sc, mb_sc, s_sc, cmax_sc, p_sc, alpha_sc, m_sc, acc_sc, *, n_near, n_far):
    hp = pl.program_id(1)
    blk = MOBA_BLOCK
    seq = k_ref.shape[1]
    nb = seq // blk
    heads = range(HEADS_PER_STEP)
    lag = PIPE // 2

    ones_row = lax.broadcasted_iota(jnp.int32, (V_AUG - MOBA_DH, seq), 0) == 0
    kmean = jnp.concatenate(
        [jnp.mean(k_ref[0, jb * blk:(jb + 1) * blk, :].astype(F32), axis=0, keepdims=True)
         for jb in range(nb)], axis=0)
    lane_head = lax.broadcasted_iota(jnp.int32, (nb, LANES), 1) // MOBA_DH
    for hh in heads:
        vaug_sc[hh, :MOBA_DH, :] = v_ref[0, hh * MOBA_DH:(hh + 1) * MOBA_DH, :]
        vaug_sc[hh, MOBA_DH:, :] = jnp.where(ones_row, 1.0, 0.0).astype(BF16)
        mb_sc[hh, nb:, :] = jnp.full((mb_sc.shape[1] - nb, seq), NEG_INF, F32)
        m_sc[hh] = jnp.full(m_sc.shape[1:], NEG_INF, F32)
        acc_sc[hh] = jnp.zeros(acc_sc.shape[1:], F32)
        km_hi, km_lo = _split_hi_lo(jnp.where(lane_head == hh, kmean, 0.0))
        far_bias = farb_ref[hp * HEADS_PER_STEP + hh] * LOG2E
        for c0 in range(0, seq, GATE_COLS):
            cw = min(GATE_COLS, seq - c0)
            qc = q_ref[0, :, c0:c0 + cw]
            g = (jnp.dot(km_hi, qc, preferred_element_type=F32)
                 + jnp.dot(km_lo, qc, preferred_element_type=F32))
            key_blk = lax.broadcasted_iota(jnp.int32, (nb, cw), 0)
            qry_blk = (lax.broadcasted_iota(jnp.int32, (nb, cw), 1) + c0) // blk
            g = jnp.where(key_blk < qry_blk, g, -jnp.inf)
            key_f = key_blk.astype(F32)
            picked = jnp.zeros((nb, cw), F32)
            for _ in range(MOBA_TOPK):
                best = jnp.max(g, axis=0, keepdims=True)
                first = jnp.min(jnp.where(g == best, key_f, float(nb)), axis=0, keepdims=True)
                found = jnp.where(best > -jnp.inf, 1.0, 0.0)
                picked = picked + jnp.where(key_f == first, found, 0.0)
                g = jnp.where(key_f == first, -jnp.inf, g)
            mb_sc[hh, :nb, c0:c0 + cw] = jnp.where(
                picked > 0.0, jnp.where(key_blk < qry_blk - 1, far_bias, 0.0),
                jnp.where(key_blk == qry_blk, 0.0, NEG_INF))

    feat = lax.broadcasted_iota(jnp.int32, (LANES, blk), 0) // MOBA_DH

    def run_pass(sched_ref, n_tiles, with_table):
        def col(idx, r):
            return pl.multiple_of(sched_ref[r, idx] * blk, blk)

        def logits(idx, slot):
            qb = q_ref[0, :, pl.ds(col(idx, 0), blk)]
            qz = jnp.concatenate(
                [jnp.where(feat == hh, qb, jnp.zeros_like(qb)) for hh in heads], axis=1)
            s = jnp.dot(k_ref[0, pl.ds(col(idx, 1), blk), :], qz, preferred_element_type=F32)
            if with_table:
                tab = sched_ref[3, idx]
                s = s + jnp.concatenate([bias_ref[hh, tab] for hh in heads], axis=1)
            s_sc[slot] = s
            cmax_sc[slot] = jnp.max(s.reshape(blk // SUBLANES, SUBLANES, s.shape[1]), axis=0)

        def softmax_update(idx, slot):
            n = sched_ref[0, idx]
            qcol = col(idx, 0)
            row = sched_ref[2, idx]
            for hh in heads:
                cols = slice(hh * blk, (hh + 1) * blk)
                s = s_sc[slot, :, cols]
                cm = jnp.max(cmax_sc[slot, :, cols], axis=0, keepdims=True)
                mb = mb_sc[hh, pl.ds(row, 1), pl.ds(qcol, blk)]
                m_old = m_sc[hh, n]
                m_new = jnp.maximum(m_old, cm + mb)
                alpha_sc[slot, hh] = jnp.exp2(m_old - m_new)
                m_sc[hh, n] = m_new
                p_sc[slot, hh] = jnp.exp2(s + (mb - m_new)).astype(BF16)

        def values(idx, slot):
            n = sched_ref[0, idx]
            kcol = col(idx, 1)
            for hh in heads:
                pv = jnp.dot(vaug_sc[hh, :, pl.ds(kcol, blk)], p_sc[slot, hh],
                             preferred_element_type=F32)
                acc_sc[hh, n] = alpha_sc[slot, hh] * acc_sc[hh, n] + pv

        for i in range(PIPE):
            logits(i, i)
            if i >= lag:
                softmax_update(i - lag, i - lag)

        def body(d, carry):
            for u in range(PIPE):
                i = PIPE + PIPE * d + u
                logits(i, u)
                softmax_update(i - lag, (u + lag) % PIPE)
                values(i - PIPE, u)
            return carry

        lax.fori_loop(0, -(-n_tiles // PIPE), body, 0)

    run_pass(near_ref, n_near, True)
    run_pass(far_ref, n_far, False)

    for n in range(nb):
        out_t = jnp.concatenate(
            [acc_sc[hh, n, :MOBA_DH, :] * (1.0 / acc_sc[hh, n, MOBA_DH:MOBA_DH + 1, :])
             for hh in heads], axis=0)
        y_ref[0, n * blk:(n + 1) * blk, :] = out_t.T.astype(y_ref.dtype)


def _moba(mqv, mk, bias_tab, far_bias):
    b, s, _ = mk.shape
    blk = MOBA_BLOCK
    nb = s // blk
    npair = MOBA_HEADS // HEADS_PER_STEP
    (near, n_near), (far, n_far) = _moba_schedules(nb)
    smem = pl.BlockSpec(memory_space=pltpu.SMEM)
    return pl.pallas_call(
        functools.partial(_moba_kernel, n_near=n_near, n_far=n_far),
        grid=(b, npair),
        in_specs=[
            smem, smem, smem,
            pl.BlockSpec((1, LANES, s), lambda i, p: (i, p, 0)),
            pl.BlockSpec((1, LANES, s), lambda i, p: (i, npair + p, 0)),
            pl.BlockSpec((1, s, LANES), lambda i, p: (i, 0, p)),
            pl.BlockSpec((HEADS_PER_STEP, N_TABLES, blk, blk), lambda i, p: (p, 0, 0, 0)),
        ],
        out_specs=pl.BlockSpec((1, s, LANES), lambda i, p: (i, 0, p)),
        out_shape=jax.ShapeDtypeStruct((b, s, MOBA_W), BF16),
        scratch_shapes=[
            pltpu.VMEM((HEADS_PER_STEP, V_AUG, s), BF16),
            pltpu.VMEM((HEADS_PER_STEP, nb + SUBLANES, s), F32),
            pltpu.VMEM((PIPE, blk, HEADS_PER_STEP * blk), F32),
            pltpu.VMEM((PIPE, SUBLANES, HEADS_PER_STEP * blk), F32),
            pltpu.VMEM((PIPE, HEADS_PER_STEP, blk, blk), BF16),
            pltpu.VMEM((PIPE, HEADS_PER_STEP, 1, blk), F32),
            pltpu.VMEM((HEADS_PER_STEP, nb, 1, blk), F32),
            pltpu.VMEM((HEADS_PER_STEP, nb, V_AUG, blk), F32),
        ],
        compiler_params=_params("parallel", "parallel"),
        name="moba",
    )(jnp.asarray(near), jnp.asarray(far), far_bias, mqv, mqv, mk, bias_tab)


def _mix_ffn_kernel(x_ref, lru_ref, gla_ref, moba_ref, wl_ref, wgl_ref, wm_ref, gmix_ref,
                    gpre_ref, wg_ref, wu_ref, wd_ref, gpost_ref, o_ref):
    y = jnp.dot(lru_ref[...], wl_ref[...], preferred_element_type=F32)
    y = y + jnp.dot(gla_ref[...], wgl_ref[...], preferred_element_type=F32)
    y = y + jnp.dot(moba_ref[...], wm_ref[...], preferred_element_type=F32)
    x = x_ref[...] + _rms(y, gmix_ref[...])
    h = _rms(x, gpre_ref[...]).astype(BF16)
    a = jax.nn.silu(jnp.dot(h, wg_ref[...], preferred_element_type=F32))
    a = (a * jnp.dot(h, wu_ref[...], preferred_element_type=F32)).astype(BF16)
    f = jnp.dot(a, wd_ref[...], preferred_element_type=F32)
    o_ref[...] = x + _rms(f, gpost_ref[...])


def _mix_ffn(x, y_lru, y_gla, y_moba, w_out, gmix, gpre, wg, wu, wd, gpost, tt):
    t, d = x.shape
    wl = w_out[:LRU_W]
    wgl = w_out[LRU_W:LRU_W + GLA_V_W]
    wm = w_out[LRU_W + GLA_V_W:]
    row = lambda w: pl.BlockSpec((tt, w), lambda i: (i, 0))
    once = lambda shape: pl.BlockSpec(shape, lambda i: (0, 0), pipeline_mode=pl.Buffered(1))
    return pl.pallas_call(
        _mix_ffn_kernel,
        grid=(t // tt,),
        in_specs=[row(d), row(LRU_W), row(GLA_V_W), row(MOBA_W),
                  once(wl.shape), once(wgl.shape), once(wm.shape), _const_spec((1, d)),
                  _const_spec((1, d)), once(wg.shape), once(wu.shape), once(wd.shape),
                  _const_spec((1, d))],
        out_specs=row(d),
        out_shape=jax.ShapeDtypeStruct((t, d), F32),
        compiler_params=_params("parallel"),
        name="mix_ffn",
    )(x, y_lru, y_gla, y_moba, wl, wgl, wm, gmix, gpre, wg, wu, wd, gpost)


def _block_diag(w):
    g, n, _ = w.shape
    eye = jnp.eye(g, dtype=w.dtype)
    return (eye[:, None, :, None] * w[:, :, None, :]).reshape(g * n, g * n)


def _split_w_in(w_in):
    sizes = (LRU_W, LRU_W, GLA_QK_W, GLA_QK_W, GLA_V_W, GLA_RANK, GLA_V_W, MOBA_W, MOBA_W, MOBA_W)
    offs = np.cumsum((0,) + sizes)
    col = lambda i: w_in[:, offs[i]:offs[i + 1]]
    lru_x, lru_g, gq, gk, gv, g_lr, g_out, mq, mk, mv = (col(i) for i in range(len(sizes)))
    g_lr = jnp.pad(g_lr, ((0, 0), (0, LANES - GLA_RANK)))
    w_nat = jnp.concatenate([lru_x, lru_g, gq, gk, gv, g_out, g_lr, mk], axis=1).astype(BF16)
    w_t = jnp.concatenate([mq, mv], axis=1).T.astype(BF16)
    return w_nat, w_t


def kernel(x, pre_mix_norm, post_mix_norm, pre_ffn_norm, post_ffn_norm, w_in, w_out, lru_conv_w, lru_conv_b, lru_wa, lru_ba, lru_wx, lru_bx, lru_lambda, gla_gate_w2, gla_gate_b, gla_norm, rel_bias, w_ffn_gate, w_ffn_up, w_ffn_down):
    b, s, d = x.shape
    assert s % MOBA_BLOCK == 0 and d == LRU_W + GLA_V_W + MOBA_W
    depth = w_in.shape[0]
    ts_proj = min(512, s)
    ts_seq = min(256, s)
    nbat = 4 if b % 4 == 0 else 1
    tt = min(512, b * s)
    row = lambda v: v.reshape(1, -1).astype(F32)

    bias_tab = _bias_tables(rel_bias)
    for l in range(depth):
        w_nat, w_t = _split_w_in(w_in[l])
        lru_in, gla_in, mk, mqv = _inproj(x, row(pre_mix_norm[l]), w_nat, w_t, ts_proj)
        y_lru = _lru(lru_in, lru_conv_w[l].astype(F32), row(lru_conv_b[l]),
                     _block_diag(lru_wa[l]).astype(BF16), row(lru_ba[l]),
                     _block_diag(lru_wx[l]).astype(BF16), row(lru_bx[l]),
                     row(lru_lambda[l]), ts_seq, nbat)
        w2 = jnp.pad(gla_gate_w2[l], ((0, LANES - GLA_RANK), (0, 0))).astype(BF16)
        y_gla = _gla(gla_in, w2, row(gla_gate_b[l]), row(gla_norm[l]), ts_seq, nbat)
        y_moba = _moba(mqv, mk, bias_tab, rel_bias[REL_BUCKETS - 1].astype(F32))
        x2 = _mix_ffn(x.reshape(b * s, d), y_lru.reshape(b * s, -1), y_gla.reshape(b * s, -1),
                      y_moba.reshape(b * s, -1), w_out[l].astype(BF16), row(post_mix_norm[l]),
                      row(pre_ffn_norm[l]), w_ffn_gate[l].astype(BF16), w_ffn_up[l].astype(BF16),
                      w_ffn_down[l].astype(BF16), row(post_ffn_norm[l]), tt)
        x = x2.reshape(b, s, d)
    return x
```

```python
import functools
import math

import numpy as np
import jax
import jax.numpy as jnp
from jax import lax
from jax.experimental import pallas as pl
from jax.experimental.pallas import tpu as pltpu

F32 = jnp.float32
BF16 = jnp.bfloat16

LRU_W = 256
LRU_BLOCKS = 4
LRU_CONV = 4
LRU_C = 8.0
GLA_HEADS = 4
GLA_DV = 64
GLA_DK = 32
GLA_RANK = 16
GLA_TAU = 16.0
GLA_CHUNK = 64
GLA_QK_W = GLA_HEADS * GLA_DK
GLA_V_W = GLA_HEADS * GLA_DV
MOBA_HEADS = 8
MOBA_DH = 64
MOBA_BLOCK = 256
MOBA_TOPK = 3
MOBA_W = MOBA_HEADS * MOBA_DH
REL_BUCKETS = 32
REL_MAX_DIST = 128
RMS_EPS = 1e-6
NEG_INF = -1e30
LOG2E = math.log2(math.e)

LANES = 128
SUBLANES = 8
VMEM_LIMIT = 56 * 1024 * 1024

GLA_IN_W = GLA_QK_W * 2 + GLA_V_W * 2 + LANES
HEADS_PER_STEP = LANES // MOBA_DH

NT_DIMS = (((1,), (1,)), ((), ()))
TN_DIMS = (((0,), (0,)), ((), ()))


def _rms(x, g):
    return x * lax.rsqrt(jnp.mean(x * x, axis=-1, keepdims=True) + RMS_EPS) * g


def _split_hi_lo(x):
    hi = x.astype(BF16)
    return hi, (x - hi.astype(F32)).astype(BF16)


def _params(*sem):
    return pltpu.CompilerParams(dimension_semantics=sem, vmem_limit_bytes=VMEM_LIMIT)


def _const_spec(shape):
    return pl.BlockSpec(shape, lambda *_: (0,) * len(shape))


def _inproj_kernel(x_ref, g_ref, wn_ref, wt_ref, lru_ref, gla_ref, mk_ref, mqv_ref):
    h = _rms(x_ref[0], g_ref[...]).astype(BF16)
    pn = jnp.dot(h, wn_ref[...], preferred_element_type=F32)
    lru_ref[0] = pn[:, :2 * LRU_W]
    gla_ref[0] = pn[:, 2 * LRU_W:2 * LRU_W + GLA_IN_W]
    mk_ref[0] = pn[:, 2 * LRU_W + GLA_IN_W:].astype(BF16)
    pt = lax.dot_general(wt_ref[...], h, NT_DIMS, preferred_element_type=F32)
    mqv_ref[0, :MOBA_W, :] = (pt[:MOBA_W] * (MOBA_DH ** -0.5 * LOG2E)).astype(BF16)
    mqv_ref[0, MOBA_W:, :] = pt[MOBA_W:].astype(BF16)


def _inproj(x, g, wn, wt, ts):
    b, s, d = x.shape
    nn = wn.shape[1]
    return pl.pallas_call(
        _inproj_kernel,
        grid=(b, s // ts),
        in_specs=[
            pl.BlockSpec((1, ts, d), lambda i, j: (i, j, 0)),
            _const_spec((1, d)),
            _const_spec((d, nn)),
            _const_spec((2 * MOBA_W, d)),
        ],
        out_specs=[
            pl.BlockSpec((1, ts, 2 * LRU_W), lambda i, j: (i, j, 0)),
            pl.BlockSpec((1, ts, GLA_IN_W), lambda i, j: (i, j, 0)),
            pl.BlockSpec((1, ts, MOBA_W), lambda i, j: (i, j, 0)),
            pl.BlockSpec((1, 2 * MOBA_W, ts), lambda i, j: (i, 0, j)),
        ],
        out_shape=[
            jax.ShapeDtypeStruct((b, s, 2 * LRU_W), F32),
            jax.ShapeDtypeStruct((b, s, GLA_IN_W), F32),
            jax.ShapeDtypeStruct((b, s, MOBA_W), BF16),
            jax.ShapeDtypeStruct((b, 2 * MOBA_W, s), BF16),
        ],
        compiler_params=_params("parallel", "parallel"),
        name="inproj",
    )(x, g, wn, wt)


def _shift_rows(x, prev_tail, k, row8):
    sh = pltpu.roll(x, k, axis=0)
    top = jnp.where(row8 < k, pltpu.roll(prev_tail, k, axis=0), sh[:SUBLANES])
    return jnp.concatenate([top, sh[SUBLANES:]], axis=0)


def _lru_kernel(in_ref, cw_ref, cb_ref, wa_ref, ba_ref, wx_ref, bx_ref, lam_ref, y_ref,
                h_sc, tail_sc):
    j = pl.program_id(1)
    ts = in_ref.shape[1]

    @pl.when(j == 0)
    def _():
        h_sc[...] = jnp.zeros_like(h_sc)
        tail_sc[...] = jnp.zeros_like(tail_sc)

    row = lax.broadcasted_iota(jnp.int32, (ts, LRU_W), 0)
    row8 = lax.broadcasted_iota(jnp.int32, (SUBLANES, LRU_W), 0)
    for bb in range(in_ref.shape[0]):
        xb = in_ref[bb, :, :LRU_W]
        gb = in_ref[bb, :, LRU_W:]
        tail = tail_sc[bb]
        xc = xb * cw_ref[LRU_CONV - 1:LRU_CONV, :] + cb_ref[...]
        for k in range(1, LRU_CONV):
            xc = xc + _shift_rows(xb, tail, k, row8) * cw_ref[LRU_CONV - 1 - k:LRU_CONV - k, :]
        tail_sc[bb] = xb[ts - SUBLANES:]

        xcb = xc.astype(BF16)
        r = jax.nn.sigmoid(jnp.dot(xcb, wa_ref[...], preferred_element_type=F32) + ba_ref[...])
        i = jax.nn.sigmoid(jnp.dot(xcb, wx_ref[...], preferred_element_type=F32) + bx_ref[...])
        log_a = -LRU_C * r * jax.nn.softplus(-lam_ref[...])
        a = jnp.exp(log_a)
        mult = jnp.sqrt(-jnp.tanh(log_a) * (a * a + 1.0))
        mult = jnp.where(row + j * ts == 0, 1.0, mult)
        u = mult * (i * xc)

        d = 1
        while d < ts:
            keep = row >= d
            a_sh = jnp.where(keep, pltpu.roll(a, d, axis=0), 1.0)
            u_sh = jnp.where(keep, pltpu.roll(u, d, axis=0), 0.0)
            u = a * u_sh + u
            a = a * a_sh
            d *= 2
        h = u + a * h_sc[bb]
        h_sc[bb] = h[ts - 1:ts]
        y_ref[bb] = (h * jax.nn.gelu(gb)).astype(y_ref.dtype)


def _lru(lru_in, cw, cb, wa, ba, wx, bx, lam, ts, nbat):
    b, s, _ = lru_in.shape
    return pl.pallas_call(
        _lru_kernel,
        grid=(b // nbat, s // ts),
        in_specs=[
            pl.BlockSpec((nbat, ts, 2 * LRU_W), lambda i, j: (i, j, 0)),
            _const_spec((LRU_CONV, LRU_W)),
            _const_spec((1, LRU_W)),
            _const_spec((LRU_W, LRU_W)),
            _const_spec((1, LRU_W)),
            _const_spec((LRU_W, LRU_W)),
            _const_spec((1, LRU_W)),
            _const_spec((1, LRU_W)),
        ],
        out_specs=pl.BlockSpec((nbat, ts, LRU_W), lambda i, j: (i, j, 0)),
        out_shape=jax.ShapeDtypeStruct((b, s, LRU_W), BF16),
        scratch_shapes=[pltpu.VMEM((nbat, 1, LRU_W), F32),
                        pltpu.VMEM((nbat, SUBLANES, LRU_W), F32)],
        compiler_params=_params("parallel", "arbitrary"),
        name="rglru",
    )(lru_in, cw, cb, wa, ba, wx, bx, lam)


def _gla_kernel(in_ref, w2_ref, gb_ref, gain_ref, y_ref, st_sc):
    ts = in_ref.shape[1]
    c = GLA_CHUNK

    @pl.when(pl.program_id(1) == 0)
    def _():
        st_sc[...] = jnp.zeros_like(st_sc)

    ri = lax.broadcasted_iota(jnp.int32, (ts, ts), 0)
    ci = lax.broadcasted_iota(jnp.int32, (ts, ts), 1)
    in_chunk = (ri // c == ci // c) & (ci <= ri)
    tri = jnp.where(in_chunk, 1.0, 0.0).astype(BF16)
    qk_lane = lax.broadcasted_iota(jnp.int32, (1, GLA_QK_W), 1) // GLA_DK
    v_lane = lax.broadcasted_iota(jnp.int32, (1, GLA_V_W), 1) // GLA_DV
    st_row = lax.broadcasted_iota(jnp.int32, (GLA_V_W, GLA_QK_W), 0) // GLA_DV
    st_col = lax.broadcasted_iota(jnp.int32, (GLA_V_W, GLA_QK_W), 1) // GLA_DK
    same_head = st_row == st_col
    hr = lax.broadcasted_iota(jnp.int32, (GLA_V_W, GLA_V_W), 0) // GLA_DV
    hc = lax.broadcasted_iota(jnp.int32, (GLA_V_W, GLA_V_W), 1) // GLA_DV
    head_mean = jnp.where(hr == hc, 1.0 / GLA_DV, 0.0).astype(BF16)
    chunks = range(ts // c)
    rows = [slice(n * c, (n + 1) * c) for n in chunks]

    for bb in range(in_ref.shape[0]):
        q = in_ref[bb, :, :GLA_QK_W]
        k = in_ref[bb, :, GLA_QK_W:2 * GLA_QK_W]
        v = in_ref[bb, :, 2 * GLA_QK_W:2 * GLA_QK_W + GLA_V_W]
        g_out = in_ref[bb, :, 2 * GLA_QK_W + GLA_V_W:2 * GLA_QK_W + 2 * GLA_V_W]
        g_lr = in_ref[bb, :, 2 * GLA_QK_W + 2 * GLA_V_W:]

        gate = jnp.dot(g_lr.astype(BF16), w2_ref[...], preferred_element_type=F32) + gb_ref[...]
        log_alpha = jax.nn.log_sigmoid(gate) / GLA_TAU
        la_hi, la_lo = _split_hi_lo(log_alpha)
        cum = (jnp.dot(tri, la_hi, preferred_element_type=F32)
               + jnp.dot(tri, la_lo, preferred_element_type=F32))
        lasts = [cum[(n + 1) * c - 1:(n + 1) * c] for n in chunks]
        last_full = jnp.concatenate([jnp.broadcast_to(l, (c, GLA_QK_W)) for l in lasts], axis=0)

        q_d = (q * (GLA_DK ** -0.5) * jnp.exp(cum)).astype(BF16)
        k_d = (k * jnp.exp(-cum)).astype(BF16)
        k_l = (k * jnp.exp(last_full - cum)).astype(BF16)
        v_b = v.astype(BF16)

        o = jnp.zeros((ts, GLA_V_W), F32)
        for hh in range(GLA_HEADS):
            q_h = jnp.where(qk_lane == hh, q_d, jnp.zeros_like(q_d))
            attn = lax.dot_general(q_h, k_d, NT_DIMS, preferred_element_type=F32)
            attn = jnp.where(in_chunk, attn, 0.0).astype(BF16)
            v_h = jnp.where(v_lane == hh, v_b, jnp.zeros_like(v_b))
            o = o + jnp.dot(attn, v_h, preferred_element_type=F32)

        st = st_sc[bb]
        inter = []
        for n in chunks:
            inter.append(lax.dot_general(q_d[rows[n]], st.astype(BF16), NT_DIMS,
                                         preferred_element_type=F32))
            kv = lax.dot_general(v_b[rows[n]], k_l[rows[n]], TN_DIMS,
                                 preferred_element_type=F32)
            st = st * jnp.exp(lasts[n]) + jnp.where(same_head, kv, 0.0)
        st_sc[bb] = st
        o = o + jnp.concatenate(inter, axis=0)

        sq_hi, sq_lo = _split_hi_lo(o * o)
        ms = (jnp.dot(sq_hi, head_mean, preferred_element_type=F32)
              + jnp.dot(sq_lo, head_mean, preferred_element_type=F32))
        y = o * lax.rsqrt(ms + RMS_EPS) * gain_ref[...] * jax.nn.silu(g_out)
        y_ref[bb] = y.astype(y_ref.dtype)


def _gla(gla_in, w2, gb, gain, ts, nbat):
    b, s, _ = gla_in.shape
    return pl.pallas_call(
        _gla_kernel,
        grid=(b // nbat, s // ts),
        in_specs=[
            pl.BlockSpec((nbat, ts, GLA_IN_W), lambda i, j: (i, j, 0)),
            _const_spec((LANES, GLA_QK_W)),
            _const_spec((1, GLA_QK_W)),
            _const_spec((1, GLA_V_W)),
        ],
        out_specs=pl.BlockSpec((nbat, ts, GLA_V_W), lambda i, j: (i, j, 0)),
        out_shape=jax.ShapeDtypeStruct((b, s, GLA_V_W), BF16),
        scratch_shapes=[pltpu.VMEM((nbat, GLA_V_W, GLA_QK_W), F32)],
        compiler_params=_params("parallel", "arbitrary"),
        name="gla",
    )(gla_in, w2, gb, gain)


def _t5_bucket_np(rel):
    n = np.maximum(rel, 0)
    max_exact = REL_BUCKETS // 2
    nf = np.maximum(n, 1).astype(np.float32)
    large = max_exact + (np.log(nf / np.float32(max_exact)) / np.float32(math.log(REL_MAX_DIST / max_exact))
                         * np.float32(REL_BUCKETS - max_exact)).astype(np.int32)
    large = np.minimum(large, REL_BUCKETS - 1)
    return np.where(n < max_exact, n, large).astype(np.int32)


FAR_DIST = int(np.argmax(_t5_bucket_np(np.arange(4 * REL_MAX_DIST)) == REL_BUCKETS - 1))
assert np.all(_t5_bucket_np(np.arange(FAR_DIST, 1 << 16)) == REL_BUCKETS - 1)
assert FAR_DIST <= MOBA_BLOCK

N_TABLES = 2
V_AUG = MOBA_DH + 16
GATE_COLS = 1024


def _bias_kernel(bkt_ref, rb_ref, out_ref):
    h = pl.program_id(0)
    for t in range(2):
        bkt = bkt_ref[t]
        acc = jnp.zeros(bkt.shape, F32)
        for bb in range(REL_BUCKETS):
            acc = jnp.where(bkt == bb, rb_ref[bb, h], acc)
        out_ref[0, t] = jnp.where(bkt < 0, NEG_INF, acc * LOG2E)


def _bias_tables(rel_bias):
    ko = np.arange(MOBA_BLOCK)[:, None]
    qo = np.arange(MOBA_BLOCK)[None, :]
    own = np.where(qo >= ko, _t5_bucket_np(qo - ko), -1)
    bkt = np.stack([own, _t5_bucket_np(qo - ko + MOBA_BLOCK)]).astype(np.int32)
    return pl.pallas_call(
        _bias_kernel,
        grid=(MOBA_HEADS,),
        in_specs=[
            _const_spec((2, MOBA_BLOCK, MOBA_BLOCK)),
            pl.BlockSpec(memory_space=pltpu.SMEM),
        ],
        out_specs=pl.BlockSpec((1, N_TABLES, MOBA_BLOCK, MOBA_BLOCK), lambda h: (h, 0, 0, 0)),
        out_shape=jax.ShapeDtypeStruct((MOBA_HEADS, N_TABLES, MOBA_BLOCK, MOBA_BLOCK), F32),
        compiler_params=_params("parallel"),
        name="moba_bias",
    )(jnp.asarray(bkt), rel_bias.astype(F32))


PIPE = 8
LAG = PIPE // 2


def _pipe_trips(n_tiles):
    return -(-(n_tiles + LAG - PIPE) // PIPE)


def _moba_schedules(nb):
    near = [(n, n, n, 0) for n in range(nb)] + [(n, n - 1, n - 1, 1) for n in range(1, nb)]
    far = [(n, j, j, 0) for j in range(nb - 2) for n in range(j + 2, nb)]

    def table(tiles):
        length = PIPE + PIPE * _pipe_trips(len(tiles))
        rows = tiles + [(0, 0, nb, 0)] * (length - len(tiles))
        return np.asarray(rows, np.int32).T.copy(), len(tiles)

    return table(near), table(far)


def _moba_kernel(near_ref, far_ref, farb_ref, q_ref, v_ref, k_ref, bias_ref, y_ref,
                 vaug_sc, mb_sc, s_sc, cmax_sc, m_sc, acc_sc, *, n_near, n_far):
    hp = pl.program_id(1)
    blk = MOBA_BLOCK
    seq = k_ref.shape[1]
    nb = seq // blk
    heads = range(HEADS_PER_STEP)

    ones_row = lax.broadcasted_iota(jnp.int32, (V_AUG - MOBA_DH, seq), 0) == 0
    kmean = jnp.concatenate(
        [jnp.mean(k_ref[0, jb * blk:(jb + 1) * blk, :].astype(F32), axis=0, keepdims=True)
         for jb in range(nb)], axis=0)
    lane_head = lax.broadcasted_iota(jnp.int32, (nb, LANES), 1) // MOBA_DH
    for hh in heads:
        vaug_sc[hh, :MOBA_DH, :] = v_ref[0, hh * MOBA_DH:(hh + 1) * MOBA_DH, :]
        vaug_sc[hh, MOBA_DH:, :] = jnp.where(ones_row, 1.0, 0.0).astype(BF16)
        mb_sc[hh, nb:, :] = jnp.full((mb_sc.shape[1] - nb, seq), NEG_INF, F32)
        m_sc[hh] = jnp.full(m_sc.shape[1:], NEG_INF, F32)
        acc_sc[hh] = jnp.zeros(acc_sc.shape[1:], F32)
        km_hi, km_lo = _split_hi_lo(jnp.where(lane_head == hh, kmean, 0.0))
        far_bias = farb_ref[hp * HEADS_PER_STEP + hh] * LOG2E
        for c0 in range(0, seq, GATE_COLS):
            cw = min(GATE_COLS, seq - c0)
            qc = q_ref[0, :, c0:c0 + cw]
            g = (jnp.dot(km_hi, qc, preferred_element_type=F32)
                 + jnp.dot(km_lo, qc, preferred_element_type=F32))
            key_blk = lax.broadcasted_iota(jnp.int32, (nb, cw), 0)
            qry_blk = (lax.broadcasted_iota(jnp.int32, (nb, cw), 1) + c0) // blk
            g = jnp.where(key_blk < qry_blk, g, -jnp.inf)
            key_f = key_blk.astype(F32)
            picked = jnp.zeros((nb, cw), F32)
            for _ in range(MOBA_TOPK):
                best = jnp.max(g, axis=0, keepdims=True)
                first = jnp.min(jnp.where(g == best, key_f, float(nb)), axis=0, keepdims=True)
                found = jnp.where(best > -jnp.inf, 1.0, 0.0)
                picked = picked + jnp.where(key_f == first, found, 0.0)
                g = jnp.where(key_f == first, -jnp.inf, g)
            mb_sc[hh, :nb, c0:c0 + cw] = jnp.where(
                picked > 0.0, jnp.where(key_blk < qry_blk - 1, far_bias, 0.0),
                jnp.where(key_blk == qry_blk, 0.0, NEG_INF))

    feat = lax.broadcasted_iota(jnp.int32, (LANES, blk), 0) // MOBA_DH

    def run_pass(sched_ref, n_tiles, with_table):
        def col(idx, r):
            return pl.multiple_of(sched_ref[r, idx] * blk, blk)

        def logits(idx, slot):
            qb = q_ref[0, :, pl.ds(col(idx, 0), blk)]
            qz = jnp.concatenate(
                [jnp.where(feat == hh, qb, jnp.zeros_like(qb)) for hh in heads], axis=1)
            s = jnp.dot(k_ref[0, pl.ds(col(idx, 1), blk), :], qz, preferred_element_type=F32)
            if with_table:
                tab = sched_ref[3, idx]
                s = s + jnp.concatenate([bias_ref[hh, tab] for hh in heads], axis=1)
            s_sc[slot] = s
            cmax_sc[slot] = jnp.max(s.reshape(blk // SUBLANES, SUBLANES, s.shape[1]), axis=0)

        def softmax_values(idx, slot):
            n = sched_ref[0, idx]
            qcol = col(idx, 0)
            kcol = col(idx, 1)
            row = sched_ref[2, idx]
            for hh in heads:
                cols = slice(hh * blk, (hh + 1) * blk)
                cm = jnp.max(cmax_sc[slot, :, cols], axis=0, keepdims=True)
                mb = mb_sc[hh, pl.ds(row, 1), pl.ds(qcol, blk)]
                m_old = m_sc[hh, n]
                m_new = jnp.maximum(m_old, cm + mb)
                m_sc[hh, n] = m_new
                p = jnp.exp2(s_sc[slot, :, cols] + (mb - m_new)).astype(BF16)
                pv = jnp.dot(vaug_sc[hh, :, pl.ds(kcol, blk)], p, preferred_element_type=F32)
                acc_sc[hh, n] = jnp.exp2(m_old - m_new) * acc_sc[hh, n] + pv

        for i in range(PIPE):
            logits(i, i)
            if i >= LAG:
                softmax_values(i - LAG, i - LAG)

        def body(d, carry):
            for u in range(PIPE):
                i = PIPE + PIPE * d + u
                logits(i, u)
                softmax_values(i - LAG, (u + LAG) % PIPE)
            return carry

        lax.fori_loop(0, _pipe_trips(n_tiles), body, 0)

    run_pass(near_ref, n_near, True)
    run_pass(far_ref, n_far, False)

    for n in range(nb):
        out_t = jnp.concatenate(
            [acc_sc[hh, n, :MOBA_DH, :] * (1.0 / acc_sc[hh, n, MOBA_DH:MOBA_DH + 1, :])
             for hh in heads], axis=0)
        y_ref[0, n * blk:(n + 1) * blk, :] = out_t.T.astype(y_ref.dtype)


def _moba(mqv, mk, bias_tab, far_bias):
    b, s, _ = mk.shape
    blk = MOBA_BLOCK
    nb = s // blk
    npair = MOBA_HEADS // HEADS_PER_STEP
    (near, n_near), (far, n_far) = _moba_schedules(nb)
    smem = pl.BlockSpec(memory_space=pltpu.SMEM)
    return pl.pallas_call(
        functools.partial(_moba_kernel, n_near=n_near, n_far=n_far),
        grid=(b, npair),
        in_specs=[
            smem, smem, smem,
            pl.BlockSpec((1, LANES, s), lambda i, p: (i, p, 0)),
            pl.BlockSpec((1, LANES, s), lambda i, p: (i, npair + p, 0)),
            pl.BlockSpec((1, s, LANES), lambda i, p: (i, 0, p)),
            pl.BlockSpec((HEADS_PER_STEP, N_TABLES, blk, blk), lambda i, p: (p, 0, 0, 0)),
        ],
        out_specs=pl.BlockSpec((1, s, LANES), lambda i, p: (i, 0, p)),
        out_shape=jax.ShapeDtypeStruct((b, s, MOBA_W), BF16),
        scratch_shapes=[
            pltpu.VMEM((HEADS_PER_STEP, V_AUG, s), BF16),
            pltpu.VMEM((HEADS_PER_STEP, nb + SUBLANES, s), F32),
            pltpu.VMEM((PIPE, blk, HEADS_PER_STEP * blk), F32),
            pltpu.VMEM((PIPE, SUBLANES, HEADS_PER_STEP * blk), F32),
            pltpu.VMEM((HEADS_PER_STEP, nb, 1, blk), F32),
            pltpu.VMEM((HEADS_PER_STEP, nb, V_AUG, blk), F32),
        ],
        compiler_params=_params("parallel", "parallel"),
        name="moba",
    )(jnp.asarray(near), jnp.asarray(far), far_bias, mqv, mqv, mk, bias_tab)


def _mix_ffn_kernel(x_ref, lru_ref, gla_ref, moba_ref, wl_ref, wgl_ref, wm_ref, gmix_ref,
                    gpre_ref, wg_ref, wu_ref, wd_ref, gpost_ref, o_ref):
    y = jnp.dot(lru_ref[...], wl_ref[...], preferred_element_type=F32)
    y = y + jnp.dot(gla_ref[...], wgl_ref[...], preferred_element_type=F32)
    y = y + jnp.dot(moba_ref[...], wm_ref[...], preferred_element_type=F32)
    x = x_ref[...] + _rms(y, gmix_ref[...])
    h = _rms(x, gpre_ref[...]).astype(BF16)
    a = jax.nn.silu(jnp.dot(h, wg_ref[...], preferred_element_type=F32))
    a = (a * jnp.dot(h, wu_ref[...], preferred_element_type=F32)).astype(BF16)
    f = jnp.dot(a, wd_ref[...], preferred_element_type=F32)
    o_ref[...] = x + _rms(f, gpost_ref[...])


def _mix_ffn(x, y_lru, y_gla, y_moba, w_out, gmix, gpre, wg, wu, wd, gpost, tt):
    t, d = x.shape
    wl = w_out[:LRU_W]
    wgl = w_out[LRU_W:LRU_W + GLA_V_W]
    wm = w_out[LRU_W + GLA_V_W:]
    row = lambda w: pl.BlockSpec((tt, w), lambda i: (i, 0))
    once = lambda shape: pl.BlockSpec(shape, lambda i: (0, 0), pipeline_mode=pl.Buffered(1))
    return pl.pallas_call(
        _mix_ffn_kernel,
        grid=(t // tt,),
        in_specs=[row(d), row(LRU_W), row(GLA_V_W), row(MOBA_W),
                  once(wl.shape), once(wgl.shape), once(wm.shape), _const_spec((1, d)),
                  _const_spec((1, d)), once(wg.shape), once(wu.shape), once(wd.shape),
                  _const_spec((1, d))],
        out_specs=row(d),
        out_shape=jax.ShapeDtypeStruct((t, d), F32),
        compiler_params=_params("parallel"),
        name="mix_ffn",
    )(x, y_lru, y_gla, y_moba, wl, wgl, wm, gmix, gpre, wg, wu, wd, gpost)


def _block_diag(w):
    g, n, _ = w.shape
    eye = jnp.eye(g, dtype=w.dtype)
    return (eye[:, None, :, None] * w[:, :, None, :]).reshape(g * n, g * n)


def _split_w_in(w_in):
    sizes = (LRU_W, LRU_W, GLA_QK_W, GLA_QK_W, GLA_V_W, GLA_RANK, GLA_V_W, MOBA_W, MOBA_W, MOBA_W)
    offs = np.cumsum((0,) + sizes)
    col = lambda i: w_in[:, offs[i]:offs[i + 1]]
    lru_x, lru_g, gq, gk, gv, g_lr, g_out, mq, mk, mv = (col(i) for i in range(len(sizes)))
    g_lr = jnp.pad(g_lr, ((0, 0), (0, LANES - GLA_RANK)))
    w_nat = jnp.concatenate([lru_x, lru_g, gq, gk, gv, g_out, g_lr, mk], axis=1).astype(BF16)
    w_t = jnp.concatenate([mq, mv], axis=1).T.astype(BF16)
    return w_nat, w_t


def kernel(x, pre_mix_norm, post_mix_norm, pre_ffn_norm, post_ffn_norm, w_in, w_out, lru_conv_w, lru_conv_b, lru_wa, lru_ba, lru_wx, lru_bx, lru_lambda, gla_gate_w2, gla_gate_b, gla_norm, rel_bias, w_ffn_gate, w_ffn_up, w_ffn_down):
    b, s, d = x.shape
    assert s % MOBA_BLOCK == 0 and d == LRU_W + GLA_V_W + MOBA_W
    depth = w_in.shape[0]
    ts_proj = min(512, s)
    ts_seq = min(256, s)
    nbat = 4 if b % 4 == 0 else 1
    tt = min(512, b * s)
    row = lambda v: v.reshape(1, -1).astype(F32)

    bias_tab = _bias_tables(rel_bias)
    for l in range(depth):
        w_nat, w_t = _split_w_in(w_in[l])
        lru_in, gla_in, mk, mqv = _inproj(x, row(pre_mix_norm[l]), w_nat, w_t, ts_proj)
        y_lru = _lru(lru_in, lru_conv_w[l].astype(F32), row(lru_conv_b[l]),
                     _block_diag(lru_wa[l]).astype(BF16), row(lru_ba[l]),
                     _block_diag(lru_wx[l]).astype(BF16), row(lru_bx[l]),
                     row(lru_lambda[l]), ts_seq, nbat)
        w2 = jnp.pad(gla_gate_w2[l], ((0, LANES - GLA_RANK), (0, 0))).astype(BF16)
        y_gla = _gla(gla_in, w2, row(gla_gate_b[l]), row(gla_norm[l]), ts_seq, nbat)
        y_moba = _moba(mqv, mk, bias_tab, rel_bias[REL_BUCKETS - 1].astype(F32))
        x2 = _mix_ffn(x.reshape(b * s, d), y_lru.reshape(b * s, -1), y_gla.reshape(b * s, -1),
                      y_moba.reshape(b * s, -1), w_out[l].astype(BF16), row(post_mix_norm[l]),
                      row(pre_ffn_norm[l]), w_ffn_gate[l].astype(BF16), w_ffn_up[l].astype(BF16),
                      w_ffn_down[l].astype(BF16), row(post_ffn_norm[l]), tt)
        x = x2.reshape(b, s, d)
    return x
```

```python
import functools
import math

import numpy as np
import jax
import jax.numpy as jnp
from jax import lax
from jax.experimental import pallas as pl
from jax.experimental.pallas import tpu as pltpu

F32 = jnp.float32
BF16 = jnp.bfloat16

LRU_W = 256
LRU_BLOCKS = 4
LRU_CONV = 4
LRU_C = 8.0
GLA_HEADS = 4
GLA_DV = 64
GLA_DK = 32
GLA_RANK = 16
GLA_TAU = 16.0
GLA_CHUNK = 64
GLA_QK_W = GLA_HEADS * GLA_DK
GLA_V_W = GLA_HEADS * GLA_DV
MOBA_HEADS = 8
MOBA_DH = 64
MOBA_BLOCK = 256
MOBA_TOPK = 3
MOBA_W = MOBA_HEADS * MOBA_DH
REL_BUCKETS = 32
REL_MAX_DIST = 128
RMS_EPS = 1e-6
NEG_INF = -1e30
LOG2E = math.log2(math.e)

LANES = 128
SUBLANES = 8
VMEM_LIMIT = 56 * 1024 * 1024

GLA_IN_W = GLA_QK_W * 2 + GLA_V_W * 2 + LANES
HEADS_PER_STEP = LANES // MOBA_DH

ROW_CHAINS = 2

NT_DIMS = (((1,), (1,)), ((), ()))
TN_DIMS = (((0,), (0,)), ((), ()))


def _rms(x, g):
    return x * lax.rsqrt(jnp.mean(x * x, axis=-1, keepdims=True) + RMS_EPS) * g


def _split_hi_lo(x):
    hi = x.astype(BF16)
    return hi, (x - hi.astype(F32)).astype(BF16)


def _params(*sem):
    return pltpu.CompilerParams(dimension_semantics=sem, vmem_limit_bytes=VMEM_LIMIT)


def _const_spec(shape):
    return pl.BlockSpec(shape, lambda *_: (0,) * len(shape))


def _round_robin(chains):
    done = object()
    while chains:
        chains = [c for c in chains if next(c, done) is not done]


def _inproj_kernel(x_ref, g_ref, wn_ref, wt_ref, lru_ref, gla_ref, mk_ref, mqv_ref):
    rows_per = x_ref.shape[1] // ROW_CHAINS

    def chain(c):
        rows = slice(c * rows_per, (c + 1) * rows_per)
        h = _rms(x_ref[0, rows], g_ref[...]).astype(BF16)
        yield
        pn = jnp.dot(h, wn_ref[...], preferred_element_type=F32)
        lru_ref[0, rows] = pn[:, :2 * LRU_W]
        gla_ref[0, rows] = pn[:, 2 * LRU_W:2 * LRU_W + GLA_IN_W]
        mk_ref[0, rows] = pn[:, 2 * LRU_W + GLA_IN_W:].astype(BF16)
        yield
        pt = lax.dot_general(wt_ref[...], h, NT_DIMS, preferred_element_type=F32)
        mqv_ref[0, :MOBA_W, rows] = (pt[:MOBA_W] * (MOBA_DH ** -0.5 * LOG2E)).astype(BF16)
        mqv_ref[0, MOBA_W:, rows] = pt[MOBA_W:].astype(BF16)

    _round_robin([chain(c) for c in range(ROW_CHAINS)])


def _inproj(x, g, wn, wt, ts):
    b, s, d = x.shape
    nn = wn.shape[1]
    return pl.pallas_call(
        _inproj_kernel,
        grid=(b, s // ts),
        in_specs=[
            pl.BlockSpec((1, ts, d), lambda i, j: (i, j, 0)),
            _const_spec((1, d)),
            _const_spec((d, nn)),
            _const_spec((2 * MOBA_W, d)),
        ],
        out_specs=[
            pl.BlockSpec((1, ts, 2 * LRU_W), lambda i, j: (i, j, 0)),
            pl.BlockSpec((1, ts, GLA_IN_W), lambda i, j: (i, j, 0)),
            pl.BlockSpec((1, ts, MOBA_W), lambda i, j: (i, j, 0)),
            pl.BlockSpec((1, 2 * MOBA_W, ts), lambda i, j: (i, 0, j)),
        ],
        out_shape=[
            jax.ShapeDtypeStruct((b, s, 2 * LRU_W), F32),
            jax.ShapeDtypeStruct((b, s, GLA_IN_W), F32),
            jax.ShapeDtypeStruct((b, s, MOBA_W), BF16),
            jax.ShapeDtypeStruct((b, 2 * MOBA_W, s), BF16),
        ],
        compiler_params=_params("parallel", "parallel"),
        name="inproj",
    )(x, g, wn, wt)


def _shift_rows(x, prev_tail, k, row8):
    sh = pltpu.roll(x, k, axis=0)
    top = jnp.where(row8 < k, pltpu.roll(prev_tail, k, axis=0), sh[:SUBLANES])
    return jnp.concatenate([top, sh[SUBLANES:]], axis=0)


def _lru_tile(in_ref, cw_ref, cb_ref, wa_ref, ba_ref, wx_ref, bx_ref, lam_ref, y_ref,
              h_sc, tail_sc):
    j = pl.program_id(1)
    ts = in_ref.shape[1]
    row = lax.broadcasted_iota(jnp.int32, (ts, LRU_W), 0)
    row8 = lax.broadcasted_iota(jnp.int32, (SUBLANES, LRU_W), 0)
    def sequence(bb):
        xb = in_ref[bb, :, :LRU_W]
        gb = in_ref[bb, :, LRU_W:]
        tail = tail_sc[bb]
        xc = xb * cw_ref[LRU_CONV - 1:LRU_CONV, :] + cb_ref[...]
        for k in range(1, LRU_CONV):
            xc = xc + _shift_rows(xb, tail, k, row8) * cw_ref[LRU_CONV - 1 - k:LRU_CONV - k, :]
        tail_sc[bb] = xb[ts - SUBLANES:]
        yield

        xcb = xc.astype(BF16)
        r = jax.nn.sigmoid(jnp.dot(xcb, wa_ref[...], preferred_element_type=F32) + ba_ref[...])
        i = jax.nn.sigmoid(jnp.dot(xcb, wx_ref[...], preferred_element_type=F32) + bx_ref[...])
        log_a = -LRU_C * r * jax.nn.softplus(-lam_ref[...])
        a = jnp.exp(log_a)
        mult = jnp.sqrt(-jnp.tanh(log_a) * (a * a + 1.0))
        mult = jnp.where(row + j * ts == 0, 1.0, mult)
        u = mult * (i * xc)
        yield

        d = 1
        while d < ts:
            keep = row >= d
            a_sh = jnp.where(keep, pltpu.roll(a, d, axis=0), 1.0)
            u_sh = jnp.where(keep, pltpu.roll(u, d, axis=0), 0.0)
            u = a * u_sh + u
            a = a * a_sh
            d *= 2
            yield
        h = u + a * h_sc[bb]
        h_sc[bb] = h[ts - 1:ts]
        y_ref[bb] = (h * jax.nn.gelu(gb)).astype(y_ref.dtype)

    return [sequence(bb) for bb in range(in_ref.shape[0])]


def _gla_tile(in_ref, w2_ref, gb_ref, gain_ref, y_ref, st_sc):
    ts = in_ref.shape[1]
    c = GLA_CHUNK
    ri = lax.broadcasted_iota(jnp.int32, (ts, ts), 0)
    ci = lax.broadcasted_iota(jnp.int32, (ts, ts), 1)
    in_chunk = (ri // c == ci // c) & (ci <= ri)
    tri = jnp.where(in_chunk, 1.0, 0.0).astype(BF16)
    qk_lane = lax.broadcasted_iota(jnp.int32, (1, GLA_QK_W), 1) // GLA_DK
    v_lane = lax.broadcasted_iota(jnp.int32, (1, GLA_V_W), 1) // GLA_DV
    st_row = lax.broadcasted_iota(jnp.int32, (GLA_V_W, GLA_QK_W), 0) // GLA_DV
    st_col = lax.broadcasted_iota(jnp.int32, (GLA_V_W, GLA_QK_W), 1) // GLA_DK
    same_head = st_row == st_col
    hr = lax.broadcasted_iota(jnp.int32, (GLA_V_W, GLA_V_W), 0) // GLA_DV
    hc = lax.broadcasted_iota(jnp.int32, (GLA_V_W, GLA_V_W), 1) // GLA_DV
    head_mean = jnp.where(hr == hc, 1.0 / GLA_DV, 0.0).astype(BF16)
    chunks = range(ts // c)
    rows = [slice(n * c, (n + 1) * c) for n in chunks]

    def sequence(bb):
        q = in_ref[bb, :, :GLA_QK_W]
        k = in_ref[bb, :, GLA_QK_W:2 * GLA_QK_W]
        v = in_ref[bb, :, 2 * GLA_QK_W:2 * GLA_QK_W + GLA_V_W]
        g_out = in_ref[bb, :, 2 * GLA_QK_W + GLA_V_W:2 * GLA_QK_W + 2 * GLA_V_W]
        g_lr = in_ref[bb, :, 2 * GLA_QK_W + 2 * GLA_V_W:]

        gate = jnp.dot(g_lr.astype(BF16), w2_ref[...], preferred_element_type=F32) + gb_ref[...]
        log_alpha = jax.nn.log_sigmoid(gate) / GLA_TAU
        yield
        la_hi, la_lo = _split_hi_lo(log_alpha)
        cum = (jnp.dot(tri, la_hi, preferred_element_type=F32)
               + jnp.dot(tri, la_lo, preferred_element_type=F32))
        lasts = [cum[(n + 1) * c - 1:(n + 1) * c] for n in chunks]
        last_full = jnp.concatenate([jnp.broadcast_to(l, (c, GLA_QK_W)) for l in lasts], axis=0)

        q_d = (q * (GLA_DK ** -0.5) * jnp.exp(cum)).astype(BF16)
        k_d = (k * jnp.exp(-cum)).astype(BF16)
        k_l = (k * jnp.exp(last_full - cum)).astype(BF16)
        v_b = v.astype(BF16)
        yield

        o = jnp.zeros((ts, GLA_V_W), F32)
        for hh in range(GLA_HEADS):
            q_h = jnp.where(qk_lane == hh, q_d, jnp.zeros_like(q_d))
            attn = lax.dot_general(q_h, k_d, NT_DIMS, preferred_element_type=F32)
            attn = jnp.where(in_chunk, attn, 0.0).astype(BF16)
            v_h = jnp.where(v_lane == hh, v_b, jnp.zeros_like(v_b))
            o = o + jnp.dot(attn, v_h, preferred_element_type=F32)
            yield

        st = st_sc[bb]
        inter = []
        for n in chunks:
            inter.append(lax.dot_general(q_d[rows[n]], st.astype(BF16), NT_DIMS,
                                         preferred_element_type=F32))
            kv = lax.dot_general(v_b[rows[n]], k_l[rows[n]], TN_DIMS,
                                 preferred_element_type=F32)
            st = st * jnp.exp(lasts[n]) + jnp.where(same_head, kv, 0.0)
            yield
        st_sc[bb] = st
        o = o + jnp.concatenate(inter, axis=0)

        sq_hi, sq_lo = _split_hi_lo(o * o)
        ms = (jnp.dot(sq_hi, head_mean, preferred_element_type=F32)
              + jnp.dot(sq_lo, head_mean, preferred_element_type=F32))
        y = o * lax.rsqrt(ms + RMS_EPS) * gain_ref[...] * jax.nn.silu(g_out)
        y_ref[bb] = y.astype(y_ref.dtype)

    return [sequence(bb) for bb in range(in_ref.shape[0])]


N_LRU_ARGS = 8
N_GLA_ARGS = 4


def _recurrent_kernel(*refs):
    lru_in = refs[:N_LRU_ARGS]
    gla_in = refs[N_LRU_ARGS:N_LRU_ARGS + N_GLA_ARGS]
    y_lru_ref, y_gla_ref, h_sc, tail_sc, st_sc = refs[N_LRU_ARGS + N_GLA_ARGS:]

    @pl.when(pl.program_id(1) == 0)
    def _():
        h_sc[...] = jnp.zeros_like(h_sc)
        tail_sc[...] = jnp.zeros_like(tail_sc)
        st_sc[...] = jnp.zeros_like(st_sc)

    _round_robin(_lru_tile(*lru_in, y_lru_ref, h_sc, tail_sc)
                 + _gla_tile(*gla_in, y_gla_ref, st_sc))


def _recurrent(lru_args, gla_args, ts, nbat):
    assert len(lru_args) == N_LRU_ARGS and len(gla_args) == N_GLA_ARGS
    b, s, _ = lru_args[0].shape
    tile = lambda w: pl.BlockSpec((nbat, ts, w), lambda i, j: (i, j, 0))
    const = lambda a: _const_spec(a.shape)
    return pl.pallas_call(
        _recurrent_kernel,
        grid=(b // nbat, s // ts),
        in_specs=([tile(2 * LRU_W)] + [const(a) for a in lru_args[1:]]
                  + [tile(GLA_IN_W)] + [const(a) for a in gla_args[1:]]),
        out_specs=[tile(LRU_W), tile(GLA_V_W)],
        out_shape=[jax.ShapeDtypeStruct((b, s, LRU_W), BF16),
                   jax.ShapeDtypeStruct((b, s, GLA_V_W), BF16)],
        scratch_shapes=[pltpu.VMEM((nbat, 1, LRU_W), F32),
                        pltpu.VMEM((nbat, SUBLANES, LRU_W), F32),
                        pltpu.VMEM((nbat, GLA_V_W, GLA_QK_W), F32)],
        compiler_params=_params("parallel", "arbitrary"),
        name="recurrent",
    )(*lru_args, *gla_args)


def _t5_bucket_np(rel):
    n = np.maximum(rel, 0)
    max_exact = REL_BUCKETS // 2
    nf = np.maximum(n, 1).astype(np.float32)
    large = max_exact + (np.log(nf / np.float32(max_exact)) / np.float32(math.log(REL_MAX_DIST / max_exact))
                         * np.float32(REL_BUCKETS - max_exact)).astype(np.int32)
    large = np.minimum(large, REL_BUCKETS - 1)
    return np.where(n < max_exact, n, large).astype(np.int32)


FAR_DIST = int(np.argmax(_t5_bucket_np(np.arange(4 * REL_MAX_DIST)) == REL_BUCKETS - 1))
assert np.all(_t5_bucket_np(np.arange(FAR_DIST, 1 << 16)) == REL_BUCKETS - 1)
assert FAR_DIST <= MOBA_BLOCK

N_TABLES = 2
V_AUG = MOBA_DH + 16
GATE_COLS = 1024


def _bias_kernel(bkt_ref, rb_ref, out_ref):
    h = pl.program_id(0)
    for t in range(2):
        bkt = bkt_ref[t]
        acc = jnp.zeros(bkt.shape, F32)
        for bb in range(REL_BUCKETS):
            acc = jnp.where(bkt == bb, rb_ref[bb, h], acc)
        out_ref[0, t] = jnp.where(bkt < 0, NEG_INF, acc * LOG2E)


def _bias_tables(rel_bias):
    ko = np.arange(MOBA_BLOCK)[:, None]
    qo = np.arange(MOBA_BLOCK)[None, :]
    own = np.where(qo >= ko, _t5_bucket_np(qo - ko), -1)
    bkt = np.stack([own, _t5_bucket_np(qo - ko + MOBA_BLOCK)]).astype(np.int32)
    return pl.pallas_call(
        _bias_kernel,
        grid=(MOBA_HEADS,),
        in_specs=[
            _const_spec((2, MOBA_BLOCK, MOBA_BLOCK)),
            pl.BlockSpec(memory_space=pltpu.SMEM),
        ],
        out_specs=pl.BlockSpec((1, N_TABLES, MOBA_BLOCK, MOBA_BLOCK), lambda h: (h, 0, 0, 0)),
        out_shape=jax.ShapeDtypeStruct((MOBA_HEADS, N_TABLES, MOBA_BLOCK, MOBA_BLOCK), F32),
        compiler_params=_params("parallel"),
        name="moba_bias",
    )(jnp.asarray(bkt), rel_bias.astype(F32))


PIPE = 8
LAG = PIPE // 2


def _pipe_trips(n_tiles):
    return -(-(n_tiles + LAG - PIPE) // PIPE)


def _moba_schedules(nb):
    near = [(n, n, n, 0) for n in range(nb)] + [(n, n - 1, n - 1, 1) for n in range(1, nb)]
    far = [(n, j, j, 0) for j in range(nb - 2) for n in range(j + 2, nb)]

    def table(tiles):
        length = PIPE + PIPE * _pipe_trips(len(tiles))
        rows = tiles + [(0, 0, nb, 0)] * (length - len(tiles))
        return np.asarray(rows, np.int32).T.copy(), len(tiles)

    return table(near), table(far)


def _moba_kernel(near_ref, far_ref, farb_ref, q_ref, v_ref, k_ref, bias_ref, y_ref,
                 vaug_sc, mb_sc, s_sc, cmax_sc, m_sc, acc_sc, *, n_near, n_far):
    hp = pl.program_id(1)
    blk = MOBA_BLOCK
    seq = k_ref.shape[1]
    nb = seq // blk
    heads = range(HEADS_PER_STEP)

    ones_row = lax.broadcasted_iota(jnp.int32, (V_AUG - MOBA_DH, seq), 0) == 0
    kmean = jnp.concatenate(
        [jnp.mean(k_ref[0, jb * blk:(jb + 1) * blk, :].astype(F32), axis=0, keepdims=True)
         for jb in range(nb)], axis=0)
    lane_head = lax.broadcasted_iota(jnp.int32, (nb, LANES), 1) // MOBA_DH
    for hh in heads:
        vaug_sc[hh, :MOBA_DH, :] = v_ref[0, hh * MOBA_DH:(hh + 1) * MOBA_DH, :]
        vaug_sc[hh, MOBA_DH:, :] = jnp.where(ones_row, 1.0, 0.0).astype(BF16)
        mb_sc[hh, nb:, :] = jnp.full((mb_sc.shape[1] - nb, seq), NEG_INF, F32)
        m_sc[hh] = jnp.full(m_sc.shape[1:], NEG_INF, F32)
        acc_sc[hh] = jnp.zeros(acc_sc.shape[1:], F32)
        km_hi, km_lo = _split_hi_lo(jnp.where(lane_head == hh, kmean, 0.0))
        far_bias = farb_ref[hp * HEADS_PER_STEP + hh] * LOG2E
        for c0 in range(0, seq, GATE_COLS):
            cw = min(GATE_COLS, seq - c0)
            qc = q_ref[0, :, c0:c0 + cw]
            g = (jnp.dot(km_hi, qc, preferred_element_type=F32)
                 + jnp.dot(km_lo, qc, preferred_element_type=F32))
            key_blk = lax.broadcasted_iota(jnp.int32, (nb, cw), 0)
            qry_blk = (lax.broadcasted_iota(jnp.int32, (nb, cw), 1) + c0) // blk
            g = jnp.where(key_blk < qry_blk, g, -jnp.inf)
            key_f = key_blk.astype(F32)
            picked = jnp.zeros((nb, cw), F32)
            for _ in range(MOBA_TOPK):
                best = jnp.max(g, axis=0, keepdims=True)
                first = jnp.min(jnp.where(g == best, key_f, float(nb)), axis=0, keepdims=True)
                found = jnp.where(best > -jnp.inf, 1.0, 0.0)
                picked = picked + jnp.where(key_f == first, found, 0.0)
                g = jnp.where(key_f == first, -jnp.inf, g)
            mb_sc[hh, :nb, c0:c0 + cw] = jnp.where(
                picked > 0.0, jnp.where(key_blk < qry_blk - 1, far_bias, 0.0),
                jnp.where(key_blk == qry_blk, 0.0, NEG_INF))

    feat = lax.broadcasted_iota(jnp.int32, (LANES, blk), 0) // MOBA_DH

    def run_pass(sched_ref, n_tiles, with_table):
        def col(idx, r):
            return pl.multiple_of(sched_ref[r, idx] * blk, blk)

        def logits(idx, slot):
            qb = q_ref[0, :, pl.ds(col(idx, 0), blk)]
            qz = jnp.concatenate(
                [jnp.where(feat == hh, qb, jnp.zeros_like(qb)) for hh in heads], axis=1)
            s = jnp.dot(k_ref[0, pl.ds(col(idx, 1), blk), :], qz, preferred_element_type=F32)
            if with_table:
                tab = sched_ref[3, idx]
                s = s + jnp.concatenate([bias_ref[hh, tab] for hh in heads], axis=1)
            s_sc[slot] = s
            cmax_sc[slot] = jnp.max(s.reshape(blk // SUBLANES, SUBLANES, s.shape[1]), axis=0)

        def softmax_values(idx, slot):
            n = sched_ref[0, idx]
            qcol = col(idx, 0)
            kcol = col(idx, 1)
            row = sched_ref[2, idx]
            for hh in heads:
                cols = slice(hh * blk, (hh + 1) * blk)
                cm = jnp.max(cmax_sc[slot, :, cols], axis=0, keepdims=True)
                mb = mb_sc[hh, pl.ds(row, 1), pl.ds(qcol, blk)]
                m_old = m_sc[hh, n]
                m_new = jnp.maximum(m_old, cm + mb)
                m_sc[hh, n] = m_new
                p = jnp.exp2(s_sc[slot, :, cols] + (mb - m_new)).astype(BF16)
                pv = jnp.dot(vaug_sc[hh, :, pl.ds(kcol, blk)], p, preferred_element_type=F32)
                acc_sc[hh, n] = jnp.exp2(m_old - m_new) * acc_sc[hh, n] + pv

        for i in range(PIPE):
            logits(i, i)
            if i >= LAG:
                softmax_values(i - LAG, (i - LAG) % PIPE)

        def body(d, carry):
            for u in range(PIPE):
                i = PIPE + PIPE * d + u
                logits(i, u)
                softmax_values(i - LAG, (u - LAG) % PIPE)
            return carry

        lax.fori_loop(0, _pipe_trips(n_tiles), body, 0)

    run_pass(near_ref, n_near, True)
    run_pass(far_ref, n_far, False)

    for n in range(nb):
        out_t = jnp.concatenate(
            [acc_sc[hh, n, :MOBA_DH, :] * (1.0 / acc_sc[hh, n, MOBA_DH:MOBA_DH + 1, :])
             for hh in heads], axis=0)
        y_ref[0, n * blk:(n + 1) * blk, :] = out_t.T.astype(y_ref.dtype)


def _moba(mqv, mk, bias_tab, far_bias):
    b, s, _ = mk.shape
    blk = MOBA_BLOCK
    nb = s // blk
    npair = MOBA_HEADS // HEADS_PER_STEP
    (near, n_near), (far, n_far) = _moba_schedules(nb)
    smem = pl.BlockSpec(memory_space=pltpu.SMEM)
    return pl.pallas_call(
        functools.partial(_moba_kernel, n_near=n_near, n_far=n_far),
        grid=(b, npair),
        in_specs=[
            smem, smem, smem,
            pl.BlockSpec((1, LANES, s), lambda i, p: (i, p, 0)),
            pl.BlockSpec((1, LANES, s), lambda i, p: (i, npair + p, 0)),
            pl.BlockSpec((1, s, LANES), lambda i, p: (i, 0, p)),
            pl.BlockSpec((HEADS_PER_STEP, N_TABLES, blk, blk), lambda i, p: (p, 0, 0, 0)),
        ],
        out_specs=pl.BlockSpec((1, s, LANES), lambda i, p: (i, 0, p)),
        out_shape=jax.ShapeDtypeStruct((b, s, MOBA_W), BF16),
        scratch_shapes=[
            pltpu.VMEM((HEADS_PER_STEP, V_AUG, s), BF16),
            pltpu.VMEM((HEADS_PER_STEP, nb + SUBLANES, s), F32),
            pltpu.VMEM((PIPE, blk, HEADS_PER_STEP * blk), F32),
            pltpu.VMEM((PIPE, SUBLANES, HEADS_PER_STEP * blk), F32),
            pltpu.VMEM((HEADS_PER_STEP, nb, 1, blk), F32),
            pltpu.VMEM((HEADS_PER_STEP, nb, V_AUG, blk), F32),
        ],
        compiler_params=_params("parallel", "parallel"),
        name="moba",
    )(jnp.asarray(near), jnp.asarray(far), far_bias, mqv, mqv, mk, bias_tab)


def _mix_ffn_kernel(x_ref, lru_ref, gla_ref, moba_ref, wl_ref, wgl_ref, wm_ref, gmix_ref,
                    gpre_ref, wg_ref, wu_ref, wd_ref, gpost_ref, o_ref):
    rows_per = x_ref.shape[0] // ROW_CHAINS

    def chain(c):
        rows = slice(c * rows_per, (c + 1) * rows_per)
        y = jnp.dot(lru_ref[rows], wl_ref[...], preferred_element_type=F32)
        y = y + jnp.dot(gla_ref[rows], wgl_ref[...], preferred_element_type=F32)
        y = y + jnp.dot(moba_ref[rows], wm_ref[...], preferred_element_type=F32)
        yield
        x = x_ref[rows] + _rms(y, gmix_ref[...])
        h = _rms(x, gpre_ref[...]).astype(BF16)
        yield
        a = jax.nn.silu(jnp.dot(h, wg_ref[...], preferred_element_type=F32))
        yield
        a = (a * jnp.dot(h, wu_ref[...], preferred_element_type=F32)).astype(BF16)
        yield
        f = jnp.dot(a, wd_ref[...], preferred_element_type=F32)
        yield
        o_ref[rows] = x + _rms(f, gpost_ref[...])

    _round_robin([chain(c) for c in range(ROW_CHAINS)])


def _mix_ffn(x, y_lru, y_gla, y_moba, w_out, gmix, gpre, wg, wu, wd, gpost, tt):
    t, d = x.shape
    wl = w_out[:LRU_W]
    wgl = w_out[LRU_W:LRU_W + GLA_V_W]
    wm = w_out[LRU_W + GLA_V_W:]
    row = lambda w: pl.BlockSpec((tt, w), lambda i: (i, 0))
    once = lambda shape: pl.BlockSpec(shape, lambda i: (0, 0), pipeline_mode=pl.Buffered(1))
    return pl.pallas_call(
        _mix_ffn_kernel,
        grid=(t // tt,),
        in_specs=[row(d), row(LRU_W), row(GLA_V_W), row(MOBA_W),
                  once(wl.shape), once(wgl.shape), once(wm.shape), _const_spec((1, d)),
                  _const_spec((1, d)), once(wg.shape), once(wu.shape), once(wd.shape),
                  _const_spec((1, d))],
        out_specs=row(d),
        out_shape=jax.ShapeDtypeStruct((t, d), F32),
        compiler_params=_params("parallel"),
        name="mix_ffn",
    )(x, y_lru, y_gla, y_moba, wl, wgl, wm, gmix, gpre, wg, wu, wd, gpost)


def _block_diag(w):
    g, n, _ = w.shape
    eye = jnp.eye(g, dtype=w.dtype)
    return (eye[:, None, :, None] * w[:, :, None, :]).reshape(g * n, g * n)


def _split_w_in(w_in):
    sizes = (LRU_W, LRU_W, GLA_QK_W, GLA_QK_W, GLA_V_W, GLA_RANK, GLA_V_W, MOBA_W, MOBA_W, MOBA_W)
    offs = np.cumsum((0,) + sizes)
    col = lambda i: w_in[:, offs[i]:offs[i + 1]]
    lru_x, lru_g, gq, gk, gv, g_lr, g_out, mq, mk, mv = (col(i) for i in range(len(sizes)))
    g_lr = jnp.pad(g_lr, ((0, 0), (0, LANES - GLA_RANK)))
    w_nat = jnp.concatenate([lru_x, lru_g, gq, gk, gv, g_out, g_lr, mk], axis=1).astype(BF16)
    w_t = jnp.concatenate([mq, mv], axis=1).T.astype(BF16)
    return w_nat, w_t


def kernel(x, pre_mix_norm, post_mix_norm, pre_ffn_norm, post_ffn_norm, w_in, w_out, lru_conv_w, lru_conv_b, lru_wa, lru_ba, lru_wx, lru_bx, lru_lambda, gla_gate_w2, gla_gate_b, gla_norm, rel_bias, w_ffn_gate, w_ffn_up, w_ffn_down):
    b, s, d = x.shape
    assert s % MOBA_BLOCK == 0 and d == LRU_W + GLA_V_W + MOBA_W
    depth = w_in.shape[0]
    ts_proj = min(512, s)
    ts_seq = min(256, s)
    nbat = 4 if b % 4 == 0 else 1
    tt = min(512, b * s)
    row = lambda v: v.reshape(1, -1).astype(F32)

    bias_tab = _bias_tables(rel_bias)
    for l in range(depth):
        w_nat, w_t = _split_w_in(w_in[l])
        lru_in, gla_in, mk, mqv = _inproj(x, row(pre_mix_norm[l]), w_nat, w_t, ts_proj)
        w2 = jnp.pad(gla_gate_w2[l], ((0, LANES - GLA_RANK), (0, 0))).astype(BF16)
        y_lru, y_gla = _recurrent(
            (lru_in, lru_conv_w[l].astype(F32), row(lru_conv_b[l]),
             _block_diag(lru_wa[l]).astype(BF16), row(lru_ba[l]),
             _block_diag(lru_wx[l]).astype(BF16), row(lru_bx[l]), row(lru_lambda[l])),
            (gla_in, w2, row(gla_gate_b[l]), row(gla_norm[l])), ts_seq, nbat)
        y_moba = _moba(mqv, mk, bias_tab, rel_bias[REL_BUCKETS - 1].astype(F32))
        x2 = _mix_ffn(x.reshape(b * s, d), y_lru.reshape(b * s, -1), y_gla.reshape(b * s, -1),
                      y_moba.reshape(b * s, -1), w_out[l].astype(BF16), row(post_mix_norm[l]),
                      row(pre_ffn_norm[l]), w_ffn_gate[l].astype(BF16), w_ffn_up[l].astype(BF16),
                      w_ffn_down[l].astype(BF16), row(post_ffn_norm[l]), tt)
        x = x2.reshape(b, s, d)
    return x
```

```python
import functools
import math

import numpy as np
import jax
import jax.numpy as jnp
from jax import lax
from jax.experimental import pallas as pl
from jax.experimental.pallas import tpu as pltpu

F32 = jnp.float32
BF16 = jnp.bfloat16

LRU_W = 256
LRU_BLOCKS = 4
LRU_CONV = 4
LRU_C = 8.0
GLA_HEADS = 4
GLA_DV = 64
GLA_DK = 32
GLA_RANK = 16
GLA_TAU = 16.0
GLA_CHUNK = 64
GLA_QK_W = GLA_HEADS * GLA_DK
GLA_V_W = GLA_HEADS * GLA_DV
MOBA_HEADS = 8
MOBA_DH = 64
MOBA_BLOCK = 256
MOBA_TOPK = 3
MOBA_W = MOBA_HEADS * MOBA_DH
REL_BUCKETS = 32
REL_MAX_DIST = 128
RMS_EPS = 1e-6
NEG_INF = -1e30
LOG2E = math.log2(math.e)

LANES = 128
SUBLANES = 8
VMEM_LIMIT = 56 * 1024 * 1024

GLA_IN_W = GLA_QK_W * 2 + GLA_V_W * 2 + LANES
HEADS_PER_STEP = LANES // MOBA_DH

ROW_CHAINS = 2
SEQ_TILE = MOBA_BLOCK
TOKEN_TILE = 512

NT_DIMS = (((1,), (1,)), ((), ()))
TN_DIMS = (((0,), (0,)), ((), ()))


def _rms(x, g):
    return x * lax.rsqrt(jnp.mean(x * x, axis=-1, keepdims=True) + RMS_EPS) * g


def _split_hi_lo(x):
    hi = x.astype(BF16)
    return hi, (x - hi.astype(F32)).astype(BF16)


def _params(*sem):
    return pltpu.CompilerParams(dimension_semantics=sem, vmem_limit_bytes=VMEM_LIMIT)


def _const_spec(shape):
    return pl.BlockSpec(shape, lambda *_: (0,) * len(shape))


def _round_robin(chains):
    done = object()
    while chains:
        chains = [c for c in chains if next(c, done) is not done]


PROJ_COLS = 512


def _project(bb, x_ref, g_ref, wn_ref, wt_ref, lru_sc, gla_sc, mk_ref, mqv_ref):
    h = _rms(x_ref[bb], g_ref[...]).astype(BF16)
    yield
    natural = [(dst, c, min(PROJ_COLS, dst.shape[-1] - c))
               for dst in (lru_sc, gla_sc, mk_ref) for c in range(0, dst.shape[-1], PROJ_COLS)]
    col = 0
    for dst, c0, w in natural:
        pn = jnp.dot(h, wn_ref[:, col:col + w], preferred_element_type=F32)
        dst[bb, :, c0:c0 + w] = pn.astype(dst.dtype)
        col += w
        yield
    for r0 in range(0, 2 * MOBA_W, PROJ_COLS):
        pt = lax.dot_general(wt_ref[r0:r0 + PROJ_COLS, :], h, NT_DIMS, preferred_element_type=F32)
        if r0 < MOBA_W:
            pt = pt * (MOBA_DH ** -0.5 * LOG2E)
        mqv_ref[bb, r0:r0 + PROJ_COLS, :] = pt.astype(BF16)
        yield


def _shift_rows(x, prev_tail, k, row8):
    sh = pltpu.roll(x, k, axis=0)
    top = jnp.where(row8 < k, pltpu.roll(prev_tail, k, axis=0), sh[:SUBLANES])
    return jnp.concatenate([top, sh[SUBLANES:]], axis=0)


def _lru_tile(in_ref, cw_ref, cb_ref, wa_ref, ba_ref, wx_ref, bx_ref, lam_ref, y_ref,
              h_sc, tail_sc):
    j = pl.program_id(1)
    ts = in_ref.shape[1]
    row = lax.broadcasted_iota(jnp.int32, (ts, LRU_W), 0)
    row8 = lax.broadcasted_iota(jnp.int32, (SUBLANES, LRU_W), 0)
    def sequence(bb):
        xb = in_ref[bb, :, :LRU_W]
        gb = in_ref[bb, :, LRU_W:]
        tail = tail_sc[bb]
        xc = xb * cw_ref[LRU_CONV - 1:LRU_CONV, :] + cb_ref[...]
        for k in range(1, LRU_CONV):
            xc = xc + _shift_rows(xb, tail, k, row8) * cw_ref[LRU_CONV - 1 - k:LRU_CONV - k, :]
        tail_sc[bb] = xb[ts - SUBLANES:]
        yield

        xcb = xc.astype(BF16)
        r = jax.nn.sigmoid(jnp.dot(xcb, wa_ref[...], preferred_element_type=F32) + ba_ref[...])
        yield
        i = jax.nn.sigmoid(jnp.dot(xcb, wx_ref[...], preferred_element_type=F32) + bx_ref[...])
        yield
        log_a = -LRU_C * r * jax.nn.softplus(-lam_ref[...])
        a = jnp.exp(log_a)
        yield
        mult = jnp.sqrt(-jnp.tanh(log_a) * (a * a + 1.0))
        mult = jnp.where(row + j * ts == 0, 1.0, mult)
        u = mult * (i * xc)
        yield

        d = 1
        while d < ts:
            keep = row >= d
            a_sh = jnp.where(keep, pltpu.roll(a, d, axis=0), 1.0)
            u_sh = jnp.where(keep, pltpu.roll(u, d, axis=0), 0.0)
            u = a * u_sh + u
            a = a * a_sh
            d *= 2
            yield
        h = u + a * h_sc[bb]
        h_sc[bb] = h[ts - 1:ts]
        y_ref[bb] = (h * jax.nn.gelu(gb)).astype(y_ref.dtype)

    return [sequence(bb) for bb in range(in_ref.shape[0])]


def _gla_tile(in_ref, w2_ref, gb_ref, gain_ref, y_ref, st_sc):
    ts = in_ref.shape[1]
    c = GLA_CHUNK
    ri = lax.broadcasted_iota(jnp.int32, (ts, ts), 0)
    ci = lax.broadcasted_iota(jnp.int32, (ts, ts), 1)
    in_chunk = (ri // c == ci // c) & (ci <= ri)
    tri = jnp.where(in_chunk, 1.0, 0.0).astype(BF16)
    qk_lane = lax.broadcasted_iota(jnp.int32, (1, GLA_QK_W), 1) // GLA_DK
    v_lane = lax.broadcasted_iota(jnp.int32, (1, GLA_V_W), 1) // GLA_DV
    st_row = lax.broadcasted_iota(jnp.int32, (GLA_V_W, GLA_QK_W), 0) // GLA_DV
    st_col = lax.broadcasted_iota(jnp.int32, (GLA_V_W, GLA_QK_W), 1) // GLA_DK
    same_head = st_row == st_col
    hr = lax.broadcasted_iota(jnp.int32, (GLA_V_W, GLA_V_W), 0) // GLA_DV
    hc = lax.broadcasted_iota(jnp.int32, (GLA_V_W, GLA_V_W), 1) // GLA_DV
    head_mean = jnp.where(hr == hc, 1.0 / GLA_DV, 0.0).astype(BF16)
    chunks = range(ts // c)
    rows = [slice(n * c, (n + 1) * c) for n in chunks]

    def sequence(bb):
        q = in_ref[bb, :, :GLA_QK_W]
        k = in_ref[bb, :, GLA_QK_W:2 * GLA_QK_W]
        v = in_ref[bb, :, 2 * GLA_QK_W:2 * GLA_QK_W + GLA_V_W]
        g_out = in_ref[bb, :, 2 * GLA_QK_W + GLA_V_W:2 * GLA_QK_W + 2 * GLA_V_W]
        g_lr = in_ref[bb, :, 2 * GLA_QK_W + 2 * GLA_V_W:]

        gate = jnp.dot(g_lr.astype(BF16), w2_ref[...], preferred_element_type=F32) + gb_ref[...]
        log_alpha = jax.nn.log_sigmoid(gate) / GLA_TAU
        yield
        la_hi, la_lo = _split_hi_lo(log_alpha)
        cum = (jnp.dot(tri, la_hi, preferred_element_type=F32)
               + jnp.dot(tri, la_lo, preferred_element_type=F32))
        lasts = [cum[(n + 1) * c - 1:(n + 1) * c] for n in chunks]
        last_full = jnp.concatenate([jnp.broadcast_to(l, (c, GLA_QK_W)) for l in lasts], axis=0)
        yield

        q_d = (q * (GLA_DK ** -0.5) * jnp.exp(cum)).astype(BF16)
        k_d = (k * jnp.exp(-cum)).astype(BF16)
        yield
        k_l = (k * jnp.exp(last_full - cum)).astype(BF16)
        v_b = v.astype(BF16)
        yield

        o = jnp.zeros((ts, GLA_V_W), F32)
        for hh in range(GLA_HEADS):
            q_h = jnp.where(qk_lane == hh, q_d, jnp.zeros_like(q_d))
            attn = lax.dot_general(q_h, k_d, NT_DIMS, preferred_element_type=F32)
            attn = jnp.where(in_chunk, attn, 0.0).astype(BF16)
            v_h = jnp.where(v_lane == hh, v_b, jnp.zeros_like(v_b))
            o = o + jnp.dot(attn, v_h, preferred_element_type=F32)
            yield

        st = st_sc[bb]
        inter = []
        for n in chunks:
            inter.append(lax.dot_general(q_d[rows[n]], st.astype(BF16), NT_DIMS,
                                         preferred_element_type=F32))
            kv = lax.dot_general(v_b[rows[n]], k_l[rows[n]], TN_DIMS,
                                 preferred_element_type=F32)
            st = st * jnp.exp(lasts[n]) + jnp.where(same_head, kv, 0.0)
            yield
        st_sc[bb] = st
        o = o + jnp.concatenate(inter, axis=0)

        sq_hi, sq_lo = _split_hi_lo(o * o)
        ms = (jnp.dot(sq_hi, head_mean, preferred_element_type=F32)
              + jnp.dot(sq_lo, head_mean, preferred_element_type=F32))
        y = o * lax.rsqrt(ms + RMS_EPS) * gain_ref[...] * jax.nn.silu(g_out)
        y_ref[bb] = y.astype(y_ref.dtype)

    return [sequence(bb) for bb in range(in_ref.shape[0])]


N_PROJ_ARGS = 4
N_LRU_ARGS = 7
N_GLA_ARGS = 3


def _alternate(*gens):
    gens, done = list(gens), object()
    while gens:
        gens = [g for g in gens if next(g, done) is not done]
        yield


def _delayed(gen, steps):
    for _ in range(steps):
        yield
    yield from gen


def _front_kernel(*refs):
    proj_in = refs[:N_PROJ_ARGS]
    lru_in = refs[N_PROJ_ARGS:N_PROJ_ARGS + N_LRU_ARGS]
    gla_in = refs[N_PROJ_ARGS + N_LRU_ARGS:N_PROJ_ARGS + N_LRU_ARGS + N_GLA_ARGS]
    (mk_ref, mqv_ref, y_lru_ref, y_gla_ref,
     lru_sc, gla_sc, h_sc, tail_sc, st_sc) = refs[N_PROJ_ARGS + N_LRU_ARGS + N_GLA_ARGS:]

    @pl.when(pl.program_id(1) == 0)
    def _():
        h_sc[...] = jnp.zeros_like(h_sc)
        tail_sc[...] = jnp.zeros_like(tail_sc)
        st_sc[...] = jnp.zeros_like(st_sc)

    lru = _lru_tile(lru_sc, *lru_in, y_lru_ref, h_sc, tail_sc)
    gla = _gla_tile(gla_sc, *gla_in, y_gla_ref, st_sc)
    proj = [_project(bb, *proj_in, lru_sc, gla_sc, mk_ref, mqv_ref) for bb in range(len(lru))]
    proj_steps = 1 + sum(-(-w // PROJ_COLS) for w in
                         (lru_sc.shape[-1], gla_sc.shape[-1], MOBA_W, 2 * MOBA_W))

    def sequence(bb):
        yield from proj[bb]
        yield from _alternate(lru[bb], gla[bb])

    _round_robin([_delayed(sequence(bb), bb * proj_steps) for bb in range(len(lru))])


def _front(x, g, wn, wt, lru_args, gla_args, ts, nbat):
    assert len(lru_args) == N_LRU_ARGS and len(gla_args) == N_GLA_ARGS
    b, s, d = x.shape
    tile = lambda w: pl.BlockSpec((nbat, ts, w), lambda i, j: (i, j, 0))
    consts = (g, wn, wt) + tuple(lru_args) + tuple(gla_args)
    return pl.pallas_call(
        _front_kernel,
        grid=(b // nbat, s // ts),
        in_specs=[tile(d)] + [_const_spec(a.shape) for a in consts],
        out_specs=[tile(MOBA_W),
                   pl.BlockSpec((nbat, 2 * MOBA_W, ts), lambda i, j: (i, 0, j)),
                   tile(LRU_W), tile(GLA_V_W)],
        out_shape=[jax.ShapeDtypeStruct((b, s, MOBA_W), BF16),
                   jax.ShapeDtypeStruct((b, 2 * MOBA_W, s), BF16),
                   jax.ShapeDtypeStruct((b, s, LRU_W), BF16),
                   jax.ShapeDtypeStruct((b, s, GLA_V_W), BF16)],
        scratch_shapes=[pltpu.VMEM((nbat, ts, 2 * LRU_W), F32),
                        pltpu.VMEM((nbat, ts, GLA_IN_W), F32),
                        pltpu.VMEM((nbat, 1, LRU_W), F32),
                        pltpu.VMEM((nbat, SUBLANES, LRU_W), F32),
                        pltpu.VMEM((nbat, GLA_V_W, GLA_QK_W), F32)],
        compiler_params=_params("parallel", "arbitrary"),
        name="front",
    )(x, *consts)


def _t5_bucket_np(rel):
    n = np.maximum(rel, 0)
    max_exact = REL_BUCKETS // 2
    nf = np.maximum(n, 1).astype(np.float32)
    large = max_exact + (np.log(nf / np.float32(max_exact)) / np.float32(math.log(REL_MAX_DIST / max_exact))
                         * np.float32(REL_BUCKETS - max_exact)).astype(np.int32)
    large = np.minimum(large, REL_BUCKETS - 1)
    return np.where(n < max_exact, n, large).astype(np.int32)


FAR_DIST = int(np.argmax(_t5_bucket_np(np.arange(4 * REL_MAX_DIST)) == REL_BUCKETS - 1))
assert np.all(_t5_bucket_np(np.arange(FAR_DIST, 1 << 16)) == REL_BUCKETS - 1)
assert FAR_DIST <= MOBA_BLOCK

N_TABLES = 2
V_AUG = MOBA_DH + 16
GATE_COLS = 1024


def _bias_kernel(bkt_ref, rb_ref, out_ref):
    h = pl.program_id(0)
    for t in range(2):
        bkt = bkt_ref[t]
        acc = jnp.zeros(bkt.shape, F32)
        for bb in range(REL_BUCKETS):
            acc = jnp.where(bkt == bb, rb_ref[bb, h], acc)
        out_ref[0, t] = jnp.where(bkt < 0, NEG_INF, acc * LOG2E)


def _bias_tables(rel_bias):
    ko = np.arange(MOBA_BLOCK)[:, None]
    qo = np.arange(MOBA_BLOCK)[None, :]
    own = np.where(qo >= ko, _t5_bucket_np(qo - ko), -1)
    bkt = np.stack([own, _t5_bucket_np(qo - ko + MOBA_BLOCK)]).astype(np.int32)
    return pl.pallas_call(
        _bias_kernel,
        grid=(MOBA_HEADS,),
        in_specs=[
            _const_spec((2, MOBA_BLOCK, MOBA_BLOCK)),
            pl.BlockSpec(memory_space=pltpu.SMEM),
        ],
        out_specs=pl.BlockSpec((1, N_TABLES, MOBA_BLOCK, MOBA_BLOCK), lambda h: (h, 0, 0, 0)),
        out_shape=jax.ShapeDtypeStruct((MOBA_HEADS, N_TABLES, MOBA_BLOCK, MOBA_BLOCK), F32),
        compiler_params=_params("parallel"),
        name="moba_bias",
    )(jnp.asarray(bkt), rel_bias.astype(F32))


PIPE = 8
LAG = PIPE // 2


def _pipe_trips(n_tiles):
    return -(-(n_tiles + LAG - PIPE) // PIPE)


def _moba_schedules(nb):
    near = [(n, n, n, 0) for n in range(nb)] + [(n, n - 1, n - 1, 1) for n in range(1, nb)]
    far = [(n, j, j, 0) for j in range(nb - 2) for n in range(j + 2, nb)]

    def table(tiles):
        length = PIPE + PIPE * _pipe_trips(len(tiles))
        rows = tiles + [(0, 0, nb, 0)] * (length - len(tiles))
        return np.asarray(rows, np.int32).T.copy(), len(tiles)

    return table(near), table(far)


def _moba_kernel(near_ref, far_ref, farb_ref, q_ref, v_ref, k_ref, bias_ref, y_ref,
                 vaug_sc, mb_sc, s_sc, cmax_sc, m_sc, acc_sc, *, n_near, n_far):
    hp = pl.program_id(1)
    blk = MOBA_BLOCK
    seq = k_ref.shape[1]
    nb = seq // blk
    heads = range(HEADS_PER_STEP)

    ones_row = lax.broadcasted_iota(jnp.int32, (V_AUG - MOBA_DH, seq), 0) == 0
    kmean = jnp.concatenate(
        [jnp.mean(k_ref[0, jb * blk:(jb + 1) * blk, :].astype(F32), axis=0, keepdims=True)
         for jb in range(nb)], axis=0)
    lane_head = lax.broadcasted_iota(jnp.int32, (nb, LANES), 1) // MOBA_DH
    for hh in heads:
        vaug_sc[hh, :MOBA_DH, :] = v_ref[0, hh * MOBA_DH:(hh + 1) * MOBA_DH, :]
        vaug_sc[hh, MOBA_DH:, :] = jnp.where(ones_row, 1.0, 0.0).astype(BF16)
        mb_sc[hh, nb:, :] = jnp.full((mb_sc.shape[1] - nb, seq), NEG_INF, F32)
        m_sc[hh] = jnp.full(m_sc.shape[1:], NEG_INF, F32)
        acc_sc[hh] = jnp.zeros(acc_sc.shape[1:], F32)
        km_hi, km_lo = _split_hi_lo(jnp.where(lane_head == hh, kmean, 0.0))
        far_bias = farb_ref[hp * HEADS_PER_STEP + hh] * LOG2E
        for c0 in range(0, seq, GATE_COLS):
            cw = min(GATE_COLS, seq - c0)
            qc = q_ref[0, :, c0:c0 + cw]
            g = (jnp.dot(km_hi, qc, preferred_element_type=F32)
                 + jnp.dot(km_lo, qc, preferred_element_type=F32))
            key_blk = lax.broadcasted_iota(jnp.int32, (nb, cw), 0)
            qry_blk = (lax.broadcasted_iota(jnp.int32, (nb, cw), 1) + c0) // blk
            g = jnp.where(key_blk < qry_blk, g, -jnp.inf)
            key_f = key_blk.astype(F32)
            picked = jnp.zeros((nb, cw), F32)
            for _ in range(MOBA_TOPK):
                best = jnp.max(g, axis=0, keepdims=True)
                first = jnp.min(jnp.where(g == best, key_f, float(nb)), axis=0, keepdims=True)
                found = jnp.where(best > -jnp.inf, 1.0, 0.0)
                picked = picked + jnp.where(key_f == first, found, 0.0)
                g = jnp.where(key_f == first, -jnp.inf, g)
            mb_sc[hh, :nb, c0:c0 + cw] = jnp.where(
                picked > 0.0, jnp.where(key_blk < qry_blk - 1, far_bias, 0.0),
                jnp.where(key_blk == qry_blk, 0.0, NEG_INF))

    feat = lax.broadcasted_iota(jnp.int32, (LANES, blk), 0) // MOBA_DH

    def run_pass(sched_ref, n_tiles, with_table):
        def col(idx, r):
            return pl.multiple_of(sched_ref[r, idx] * blk, blk)

        def logits(idx, slot):
            qb = q_ref[0, :, pl.ds(col(idx, 0), blk)]
            qz = jnp.concatenate(
                [jnp.where(feat == hh, qb, jnp.zeros_like(qb)) for hh in heads], axis=1)
            s = jnp.dot(k_ref[0, pl.ds(col(idx, 1), blk), :], qz, preferred_element_type=F32)
            if with_table:
                tab = sched_ref[3, idx]
                s = s + jnp.concatenate([bias_ref[hh, tab] for hh in heads], axis=1)
            s_sc[slot] = s
            cmax_sc[slot] = jnp.max(s.reshape(blk // SUBLANES, SUBLANES, s.shape[1]), axis=0)

        def softmax_values(idx, slot):
            n = sched_ref[0, idx]
            qcol = col(idx, 0)
            kcol = col(idx, 1)
            row = sched_ref[2, idx]
            for hh in heads:
                cols = slice(hh * blk, (hh + 1) * blk)
                cm = jnp.max(cmax_sc[slot, :, cols], axis=0, keepdims=True)
                mb = mb_sc[hh, pl.ds(row, 1), pl.ds(qcol, blk)]
                m_old = m_sc[hh, n]
                m_new = jnp.maximum(m_old, cm + mb)
                m_sc[hh, n] = m_new
                p = jnp.exp2(s_sc[slot, :, cols] + (mb - m_new)).astype(BF16)
                pv = jnp.dot(vaug_sc[hh, :, pl.ds(kcol, blk)], p, preferred_element_type=F32)
                acc_sc[hh, n] = jnp.exp2(m_old - m_new) * acc_sc[hh, n] + pv

        for i in range(PIPE):
            logits(i, i)
            if i >= LAG:
                softmax_values(i - LAG, (i - LAG) % PIPE)

        def body(d, carry):
            for u in range(PIPE):
                i = PIPE + PIPE * d + u
                logits(i, u)
                softmax_values(i - LAG, (u - LAG) % PIPE)
            return carry

        lax.fori_loop(0, _pipe_trips(n_tiles), body, 0)

    run_pass(near_ref, n_near, True)
    run_pass(far_ref, n_far, False)

    for n in range(nb):
        out_t = jnp.concatenate(
            [acc_sc[hh, n, :MOBA_DH, :] * (1.0 / acc_sc[hh, n, MOBA_DH:MOBA_DH + 1, :])
             for hh in heads], axis=0)
        y_ref[0, n * blk:(n + 1) * blk, :] = out_t.T.astype(y_ref.dtype)


def _moba(mqv, mk, bias_tab, far_bias):
    b, s, _ = mk.shape
    blk = MOBA_BLOCK
    nb = s // blk
    npair = MOBA_HEADS // HEADS_PER_STEP
    (near, n_near), (far, n_far) = _moba_schedules(nb)
    smem = pl.BlockSpec(memory_space=pltpu.SMEM)
    return pl.pallas_call(
        functools.partial(_moba_kernel, n_near=n_near, n_far=n_far),
        grid=(b, npair),
        in_specs=[
            smem, smem, smem,
            pl.BlockSpec((1, LANES, s), lambda i, p: (i, p, 0)),
            pl.BlockSpec((1, LANES, s), lambda i, p: (i, npair + p, 0)),
            pl.BlockSpec((1, s, LANES), lambda i, p: (i, 0, p)),
            pl.BlockSpec((HEADS_PER_STEP, N_TABLES, blk, blk), lambda i, p: (p, 0, 0, 0)),
        ],
        out_specs=pl.BlockSpec((1, s, LANES), lambda i, p: (i, 0, p)),
        out_shape=jax.ShapeDtypeStruct((b, s, MOBA_W), BF16),
        scratch_shapes=[
            pltpu.VMEM((HEADS_PER_STEP, V_AUG, s), BF16),
            pltpu.VMEM((HEADS_PER_STEP, nb + SUBLANES, s), F32),
            pltpu.VMEM((PIPE, blk, HEADS_PER_STEP * blk), F32),
            pltpu.VMEM((PIPE, SUBLANES, HEADS_PER_STEP * blk), F32),
            pltpu.VMEM((HEADS_PER_STEP, nb, 1, blk), F32),
            pltpu.VMEM((HEADS_PER_STEP, nb, V_AUG, blk), F32),
        ],
        compiler_params=_params("parallel", "parallel"),
        name="moba",
    )(jnp.asarray(near), jnp.asarray(far), far_bias, mqv, mqv, mk, bias_tab)


def _mix_ffn_kernel(x_ref, lru_ref, gla_ref, moba_ref, wl_ref, wgl_ref, wm_ref, gmix_ref,
                    gpre_ref, wg_ref, wu_ref, wd_ref, gpost_ref, o_ref):
    rows_per = x_ref.shape[0] // ROW_CHAINS

    def chain(c):
        rows = slice(c * rows_per, (c + 1) * rows_per)
        y = jnp.dot(lru_ref[rows], wl_ref[...], preferred_element_type=F32)
        y = y + jnp.dot(gla_ref[rows], wgl_ref[...], preferred_element_type=F32)
        y = y + jnp.dot(moba_ref[rows], wm_ref[...], preferred_element_type=F32)
        yield
        x = x_ref[rows] + _rms(y, gmix_ref[...])
        h = _rms(x, gpre_ref[...]).astype(BF16)
        yield
        a = jax.nn.silu(jnp.dot(h, wg_ref[...], preferred_element_type=F32))
        yield
        a = (a * jnp.dot(h, wu_ref[...], preferred_element_type=F32)).astype(BF16)
        yield
        f = jnp.dot(a, wd_ref[...], preferred_element_type=F32)
        yield
        o_ref[rows] = x + _rms(f, gpost_ref[...])

    _round_robin([chain(c) for c in range(ROW_CHAINS)])


def _mix_ffn(x, y_lru, y_gla, y_moba, w_out, gmix, gpre, wg, wu, wd, gpost, tt):
    t, d = x.shape
    wl = w_out[:LRU_W]
    wgl = w_out[LRU_W:LRU_W + GLA_V_W]
    wm = w_out[LRU_W + GLA_V_W:]
    row = lambda w: pl.BlockSpec((tt, w), lambda i: (i, 0))
    once = lambda shape: pl.BlockSpec(shape, lambda i: (0, 0), pipeline_mode=pl.Buffered(1))
    return pl.pallas_call(
        _mix_ffn_kernel,
        grid=(t // tt,),
        in_specs=[row(d), row(LRU_W), row(GLA_V_W), row(MOBA_W),
                  once(wl.shape), once(wgl.shape), once(wm.shape), _const_spec((1, d)),
                  _const_spec((1, d)), once(wg.shape), once(wu.shape), once(wd.shape),
                  _const_spec((1, d))],
        out_specs=row(d),
        out_shape=jax.ShapeDtypeStruct((t, d), F32),
        compiler_params=_params("parallel"),
        name="mix_ffn",
    )(x, y_lru, y_gla, y_moba, wl, wgl, wm, gmix, gpre, wg, wu, wd, gpost)


def _block_diag(w):
    g, n, _ = w.shape
    eye = jnp.eye(g, dtype=w.dtype)
    return (eye[:, None, :, None] * w[:, :, None, :]).reshape(g * n, g * n)


def _split_w_in(w_in):
    sizes = (LRU_W, LRU_W, GLA_QK_W, GLA_QK_W, GLA_V_W, GLA_RANK, GLA_V_W, MOBA_W, MOBA_W, MOBA_W)
    offs = np.cumsum((0,) + sizes)
    col = lambda i: w_in[:, offs[i]:offs[i + 1]]
    lru_x, lru_g, gq, gk, gv, g_lr, g_out, mq, mk, mv = (col(i) for i in range(len(sizes)))
    g_lr = jnp.pad(g_lr, ((0, 0), (0, LANES - GLA_RANK)))
    w_nat = jnp.concatenate([lru_x, lru_g, gq, gk, gv, g_out, g_lr, mk], axis=1).astype(BF16)
    w_t = jnp.concatenate([mq, mv], axis=1).T.astype(BF16)
    return w_nat, w_t


def kernel(x, pre_mix_norm, post_mix_norm, pre_ffn_norm, post_ffn_norm, w_in, w_out, lru_conv_w, lru_conv_b, lru_wa, lru_ba, lru_wx, lru_bx, lru_lambda, gla_gate_w2, gla_gate_b, gla_norm, rel_bias, w_ffn_gate, w_ffn_up, w_ffn_down):
    b, s, d = x.shape
    assert s % MOBA_BLOCK == 0 and d == LRU_W + GLA_V_W + MOBA_W
    assert lru_wa.shape[1] == LRU_BLOCKS and (b * s) % TOKEN_TILE == 0
    depth = w_in.shape[0]
    ts_seq = SEQ_TILE
    nbat = next(n for n in (8, 4, 2, 1) if b % n == 0)
    tt = TOKEN_TILE
    row = lambda v: v.reshape(1, -1).astype(F32)

    bias_tab = _bias_tables(rel_bias)
    for l in range(depth):
        w_nat, w_t = _split_w_in(w_in[l])
        w2 = jnp.pad(gla_gate_w2[l], ((0, LANES - GLA_RANK), (0, 0))).astype(BF16)
        mk, mqv, y_lru, y_gla = _front(
            x, row(pre_mix_norm[l]), w_nat, w_t,
            (lru_conv_w[l].astype(F32), row(lru_conv_b[l]),
             _block_diag(lru_wa[l]).astype(BF16), row(lru_ba[l]),
             _block_diag(lru_wx[l]).astype(BF16), row(lru_bx[l]), row(lru_lambda[l])),
            (w2, row(gla_gate_b[l]), row(gla_norm[l])), ts_seq, nbat)
        y_moba = _moba(mqv, mk, bias_tab, rel_bias[REL_BUCKETS - 1].astype(F32))
        x2 = _mix_ffn(x.reshape(b * s, d), y_lru.reshape(b * s, -1), y_gla.reshape(b * s, -1),
                      y_moba.reshape(b * s, -1), w_out[l].astype(BF16), row(post_mix_norm[l]),
                      row(pre_ffn_norm[l]), w_ffn_gate[l].astype(BF16), w_ffn_up[l].astype(BF16),
                      w_ffn_down[l].astype(BF16), row(post_ffn_norm[l]), tt)
        x = x2.reshape(b, s, d)
    return x
```

```python
import functools
import math

import numpy as np
import jax
import jax.numpy as jnp
from jax import lax
from jax.experimental import pallas as pl
from jax.experimental.pallas import tpu as pltpu

F32 = jnp.float32
BF16 = jnp.bfloat16

LRU_W = 256
LRU_BLOCKS = 4
LRU_CONV = 4
LRU_C = 8.0
GLA_HEADS = 4
GLA_DV = 64
GLA_DK = 32
GLA_RANK = 16
GLA_TAU = 16.0
GLA_CHUNK = 64
GLA_QK_W = GLA_HEADS * GLA_DK
GLA_V_W = GLA_HEADS * GLA_DV
MOBA_HEADS = 8
MOBA_DH = 64
MOBA_BLOCK = 256
MOBA_TOPK = 3
MOBA_W = MOBA_HEADS * MOBA_DH
REL_BUCKETS = 32
REL_MAX_DIST = 128
RMS_EPS = 1e-6
NEG_INF = -1e30
LOG2E = math.log2(math.e)

LANES = 128
SUBLANES = 8
VMEM_LIMIT = 56 * 1024 * 1024

GLA_IN_W = GLA_QK_W * 2 + GLA_V_W * 2 + LANES
HEADS_PER_STEP = LANES // MOBA_DH

ROW_CHAINS = 2
SEQ_TILE = MOBA_BLOCK
TOKEN_TILE = 512

NT_DIMS = (((1,), (1,)), ((), ()))
TN_DIMS = (((0,), (0,)), ((), ()))


def _rms(x, g):
    return x * lax.rsqrt(jnp.mean(x * x, axis=-1, keepdims=True) + RMS_EPS) * g


def _split_hi_lo(x):
    hi = x.astype(BF16)
    return hi, (x - hi.astype(F32)).astype(BF16)


def _params(*sem):
    return pltpu.CompilerParams(dimension_semantics=sem, vmem_limit_bytes=VMEM_LIMIT)


def _const_spec(shape):
    return pl.BlockSpec(shape, lambda *_: (0,) * len(shape))


def _round_robin(chains):
    done = object()
    while chains:
        chains = [c for c in chains if next(c, done) is not done]


PROJ_COLS = 512


def _project(bb, x_ref, g_ref, wn_ref, wt_ref, lru_sc, gla_sc, mk_ref, mqv_ref):
    h = _rms(x_ref[bb], g_ref[...]).astype(BF16)
    yield
    natural = [(dst, c, min(PROJ_COLS, dst.shape[-1] - c))
               for dst in (lru_sc, gla_sc, mk_ref) for c in range(0, dst.shape[-1], PROJ_COLS)]
    col = 0
    for dst, c0, w in natural:
        pn = jnp.dot(h, wn_ref[:, col:col + w], preferred_element_type=F32)
        dst[bb, :, c0:c0 + w] = pn.astype(dst.dtype)
        col += w
        yield
    for r0 in range(0, 2 * MOBA_W, PROJ_COLS):
        pt = lax.dot_general(wt_ref[r0:r0 + PROJ_COLS, :], h, NT_DIMS, preferred_element_type=F32)
        if r0 < MOBA_W:
            pt = pt * (MOBA_DH ** -0.5 * LOG2E)
        mqv_ref[bb, r0:r0 + PROJ_COLS, :] = pt.astype(BF16)
        yield


def _shift_rows(x, prev_tail, k, row8):
    sh = pltpu.roll(x, k, axis=0)
    top = jnp.where(row8 < k, pltpu.roll(prev_tail, k, axis=0), sh[:SUBLANES])
    return jnp.concatenate([top, sh[SUBLANES:]], axis=0)


def _lru_tile(in_ref, cw_ref, cb_ref, wa_ref, ba_ref, wx_ref, bx_ref, lam_ref, y_ref,
              h_sc, tail_sc):
    j = pl.program_id(1)
    ts = in_ref.shape[1]
    row = lax.broadcasted_iota(jnp.int32, (ts, LRU_W), 0)
    row8 = lax.broadcasted_iota(jnp.int32, (SUBLANES, LRU_W), 0)
    def sequence(bb):
        xb = in_ref[bb, :, :LRU_W]
        gb = in_ref[bb, :, LRU_W:]
        tail = tail_sc[bb]
        xc = xb * cw_ref[LRU_CONV - 1:LRU_CONV, :] + cb_ref[...]
        for k in range(1, LRU_CONV):
            xc = xc + _shift_rows(xb, tail, k, row8) * cw_ref[LRU_CONV - 1 - k:LRU_CONV - k, :]
        tail_sc[bb] = xb[ts - SUBLANES:]
        yield

        xcb = xc.astype(BF16)
        r = jax.nn.sigmoid(jnp.dot(xcb, wa_ref[...], preferred_element_type=F32) + ba_ref[...])
        yield
        i = jax.nn.sigmoid(jnp.dot(xcb, wx_ref[...], preferred_element_type=F32) + bx_ref[...])
        yield
        log_a = -LRU_C * r * jax.nn.softplus(-lam_ref[...])
        a = jnp.exp(log_a)
        yield
        mult = jnp.sqrt(-jnp.tanh(log_a) * (a * a + 1.0))
        mult = jnp.where(row + j * ts == 0, 1.0, mult)
        u = mult * (i * xc)
        yield

        d = 1
        while d < ts:
            keep = row >= d
            a_sh = jnp.where(keep, pltpu.roll(a, d, axis=0), 1.0)
            u_sh = jnp.where(keep, pltpu.roll(u, d, axis=0), 0.0)
            u = a * u_sh + u
            a = a * a_sh
            d *= 2
            yield
        h = u + a * h_sc[bb]
        h_sc[bb] = h[ts - 1:ts]
        y_ref[bb] = (h * jax.nn.gelu(gb)).astype(y_ref.dtype)

    return [sequence(bb) for bb in range(in_ref.shape[0])]


def _gla_tile(in_ref, w2_ref, gb_ref, gain_ref, y_ref, st_sc):
    ts = in_ref.shape[1]
    c = GLA_CHUNK
    ri = lax.broadcasted_iota(jnp.int32, (ts, ts), 0)
    ci = lax.broadcasted_iota(jnp.int32, (ts, ts), 1)
    in_chunk = (ri // c == ci // c) & (ci <= ri)
    tri = jnp.where(in_chunk, 1.0, 0.0).astype(BF16)
    qk_lane = lax.broadcasted_iota(jnp.int32, (1, GLA_QK_W), 1) // GLA_DK
    v_lane = lax.broadcasted_iota(jnp.int32, (1, GLA_V_W), 1) // GLA_DV
    st_row = lax.broadcasted_iota(jnp.int32, (GLA_V_W, GLA_QK_W), 0) // GLA_DV
    st_col = lax.broadcasted_iota(jnp.int32, (GLA_V_W, GLA_QK_W), 1) // GLA_DK
    same_head = st_row == st_col
    hr = lax.broadcasted_iota(jnp.int32, (GLA_V_W, GLA_V_W), 0) // GLA_DV
    hc = lax.broadcasted_iota(jnp.int32, (GLA_V_W, GLA_V_W), 1) // GLA_DV
    head_mean = jnp.where(hr == hc, 1.0 / GLA_DV, 0.0).astype(BF16)
    chunks = range(ts // c)
    rows = [slice(n * c, (n + 1) * c) for n in chunks]

    def sequence(bb):
        q = in_ref[bb, :, :GLA_QK_W]
        k = in_ref[bb, :, GLA_QK_W:2 * GLA_QK_W]
        v = in_ref[bb, :, 2 * GLA_QK_W:2 * GLA_QK_W + GLA_V_W]
        g_out = in_ref[bb, :, 2 * GLA_QK_W + GLA_V_W:2 * GLA_QK_W + 2 * GLA_V_W]
        g_lr = in_ref[bb, :, 2 * GLA_QK_W + 2 * GLA_V_W:]

        gate = jnp.dot(g_lr.astype(BF16), w2_ref[...], preferred_element_type=F32) + gb_ref[...]
        log_alpha = jax.nn.log_sigmoid(gate) / GLA_TAU
        yield
        la_hi, la_lo = _split_hi_lo(log_alpha)
        cum = (jnp.dot(tri, la_hi, preferred_element_type=F32)
               + jnp.dot(tri, la_lo, preferred_element_type=F32))
        lasts = [cum[(n + 1) * c - 1:(n + 1) * c] for n in chunks]
        last_full = jnp.concatenate([jnp.broadcast_to(l, (c, GLA_QK_W)) for l in lasts], axis=0)
        yield

        q_d = (q * (GLA_DK ** -0.5) * jnp.exp(cum)).astype(BF16)
        k_d = (k * jnp.exp(-cum)).astype(BF16)
        yield
        k_l = (k * jnp.exp(last_full - cum)).astype(BF16)
        v_b = v.astype(BF16)
        yield

        o = jnp.zeros((ts, GLA_V_W), F32)
        for hh in range(GLA_HEADS):
            q_h = jnp.where(qk_lane == hh, q_d, jnp.zeros_like(q_d))
            attn = lax.dot_general(q_h, k_d, NT_DIMS, preferred_element_type=F32)
            attn = jnp.where(in_chunk, attn, 0.0).astype(BF16)
            v_h = jnp.where(v_lane == hh, v_b, jnp.zeros_like(v_b))
            o = o + jnp.dot(attn, v_h, preferred_element_type=F32)
            yield

        st = st_sc[bb]
        inter = []
        for n in chunks:
            inter.append(lax.dot_general(q_d[rows[n]], st.astype(BF16), NT_DIMS,
                                         preferred_element_type=F32))
            kv = lax.dot_general(v_b[rows[n]], k_l[rows[n]], TN_DIMS,
                                 preferred_element_type=F32)
            st = st * jnp.exp(lasts[n]) + jnp.where(same_head, kv, 0.0)
            yield
        st_sc[bb] = st
        o = o + jnp.concatenate(inter, axis=0)

        sq_hi, sq_lo = _split_hi_lo(o * o)
        ms = (jnp.dot(sq_hi, head_mean, preferred_element_type=F32)
              + jnp.dot(sq_lo, head_mean, preferred_element_type=F32))
        y = o * lax.rsqrt(ms + RMS_EPS) * gain_ref[...] * jax.nn.silu(g_out)
        y_ref[bb] = y.astype(y_ref.dtype)

    return [sequence(bb) for bb in range(in_ref.shape[0])]


SEQ_STAGGER = 2
N_PROJ_ARGS = 4
N_LRU_ARGS = 7
N_GLA_ARGS = 3


def _alternate(*gens):
    gens, done = list(gens), object()
    while gens:
        gens = [g for g in gens if next(g, done) is not done]
        yield


def _delayed(gen, steps):
    for _ in range(steps):
        yield
    yield from gen


def _front_kernel(*refs):
    proj_in = refs[:N_PROJ_ARGS]
    lru_in = refs[N_PROJ_ARGS:N_PROJ_ARGS + N_LRU_ARGS]
    gla_in = refs[N_PROJ_ARGS + N_LRU_ARGS:N_PROJ_ARGS + N_LRU_ARGS + N_GLA_ARGS]
    (mk_ref, mqv_ref, y_lru_ref, y_gla_ref,
     lru_sc, gla_sc, h_sc, tail_sc, st_sc) = refs[N_PROJ_ARGS + N_LRU_ARGS + N_GLA_ARGS:]

    @pl.when(pl.program_id(1) == 0)
    def _():
        h_sc[...] = jnp.zeros_like(h_sc)
        tail_sc[...] = jnp.zeros_like(tail_sc)
        st_sc[...] = jnp.zeros_like(st_sc)

    lru = _lru_tile(lru_sc, *lru_in, y_lru_ref, h_sc, tail_sc)
    gla = _gla_tile(gla_sc, *gla_in, y_gla_ref, st_sc)
    proj = [_project(bb, *proj_in, lru_sc, gla_sc, mk_ref, mqv_ref) for bb in range(len(lru))]

    def sequence(bb):
        yield from proj[bb]
        yield from _alternate(lru[bb], gla[bb])

    _round_robin([_delayed(sequence(bb), bb * SEQ_STAGGER) for bb in range(len(lru))])


def _front(x, g, wn, wt, lru_args, gla_args, ts, nbat):
    assert len(lru_args) == N_LRU_ARGS and len(gla_args) == N_GLA_ARGS
    b, s, d = x.shape
    tile = lambda w: pl.BlockSpec((nbat, ts, w), lambda i, j: (i, j, 0))
    consts = (g, wn, wt) + tuple(lru_args) + tuple(gla_args)
    return pl.pallas_call(
        _front_kernel,
        grid=(b // nbat, s // ts),
        in_specs=[tile(d)] + [_const_spec(a.shape) for a in consts],
        out_specs=[tile(MOBA_W),
                   pl.BlockSpec((nbat, 2 * MOBA_W, ts), lambda i, j: (i, 0, j)),
                   tile(LRU_W), tile(GLA_V_W)],
        out_shape=[jax.ShapeDtypeStruct((b, s, MOBA_W), BF16),
                   jax.ShapeDtypeStruct((b, 2 * MOBA_W, s), BF16),
                   jax.ShapeDtypeStruct((b, s, LRU_W), BF16),
                   jax.ShapeDtypeStruct((b, s, GLA_V_W), BF16)],
        scratch_shapes=[pltpu.VMEM((nbat, ts, 2 * LRU_W), F32),
                        pltpu.VMEM((nbat, ts, GLA_IN_W), F32),
                        pltpu.VMEM((nbat, 1, LRU_W), F32),
                        pltpu.VMEM((nbat, SUBLANES, LRU_W), F32),
                        pltpu.VMEM((nbat, GLA_V_W, GLA_QK_W), F32)],
        compiler_params=_params("parallel", "arbitrary"),
        name="front",
    )(x, *consts)


def _t5_bucket_np(rel):
    n = np.maximum(rel, 0)
    max_exact = REL_BUCKETS // 2
    nf = np.maximum(n, 1).astype(np.float32)
    large = max_exact + (np.log(nf / np.float32(max_exact)) / np.float32(math.log(REL_MAX_DIST / max_exact))
                         * np.float32(REL_BUCKETS - max_exact)).astype(np.int32)
    large = np.minimum(large, REL_BUCKETS - 1)
    return np.where(n < max_exact, n, large).astype(np.int32)


FAR_DIST = int(np.argmax(_t5_bucket_np(np.arange(4 * REL_MAX_DIST)) == REL_BUCKETS - 1))
assert np.all(_t5_bucket_np(np.arange(FAR_DIST, 1 << 16)) == REL_BUCKETS - 1)
assert FAR_DIST <= MOBA_BLOCK

N_TABLES = 2
V_AUG = MOBA_DH + 16
GATE_COLS = 1024


def _bias_kernel(bkt_ref, rb_ref, out_ref):
    h = pl.program_id(0)
    for t in range(2):
        bkt = bkt_ref[t]
        acc = jnp.zeros(bkt.shape, F32)
        for bb in range(REL_BUCKETS):
            acc = jnp.where(bkt == bb, rb_ref[bb, h], acc)
        out_ref[0, t] = jnp.where(bkt < 0, NEG_INF, acc * LOG2E)


def _bias_tables(rel_bias):
    ko = np.arange(MOBA_BLOCK)[:, None]
    qo = np.arange(MOBA_BLOCK)[None, :]
    own = np.where(qo >= ko, _t5_bucket_np(qo - ko), -1)
    bkt = np.stack([own, _t5_bucket_np(qo - ko + MOBA_BLOCK)]).astype(np.int32)
    return pl.pallas_call(
        _bias_kernel,
        grid=(MOBA_HEADS,),
        in_specs=[
            _const_spec((2, MOBA_BLOCK, MOBA_BLOCK)),
            pl.BlockSpec(memory_space=pltpu.SMEM),
        ],
        out_specs=pl.BlockSpec((1, N_TABLES, MOBA_BLOCK, MOBA_BLOCK), lambda h: (h, 0, 0, 0)),
        out_shape=jax.ShapeDtypeStruct((MOBA_HEADS, N_TABLES, MOBA_BLOCK, MOBA_BLOCK), F32),
        compiler_params=_params("parallel"),
        name="moba_bias",
    )(jnp.asarray(bkt), rel_bias.astype(F32))


PIPE = 8
LAG = PIPE // 2


def _pipe_trips(n_tiles):
    return -(-(n_tiles + LAG - PIPE) // PIPE)


def _moba_schedules(nb):
    near = [(n, n, n, 0) for n in range(nb)] + [(n, n - 1, n - 1, 1) for n in range(1, nb)]
    far = [(n, j, j, 0) for j in range(nb - 2) for n in range(j + 2, nb)]

    def table(tiles):
        length = PIPE + PIPE * _pipe_trips(len(tiles))
        rows = tiles + [(0, 0, nb, 0)] * (length - len(tiles))
        return np.asarray(rows, np.int32).T.copy(), len(tiles)

    return table(near), table(far)


def _moba_kernel(near_ref, far_ref, farb_ref, q_ref, v_ref, k_ref, bias_ref, y_ref,
                 vaug_sc, mb_sc, qz_sc, s_sc, cmax_sc, m_sc, acc_sc, *, n_near, n_far):
    hp = pl.program_id(1)
    blk = MOBA_BLOCK
    seq = k_ref.shape[1]
    nb = seq // blk
    heads = range(HEADS_PER_STEP)

    ones_row = lax.broadcasted_iota(jnp.int32, (V_AUG - MOBA_DH, seq), 0) == 0
    kmean = jnp.concatenate(
        [jnp.mean(k_ref[0, jb * blk:(jb + 1) * blk, :].astype(F32), axis=0, keepdims=True)
         for jb in range(nb)], axis=0)
    lane_head = lax.broadcasted_iota(jnp.int32, (nb, LANES), 1) // MOBA_DH
    for hh in heads:
        vaug_sc[hh, :MOBA_DH, :] = v_ref[0, hh * MOBA_DH:(hh + 1) * MOBA_DH, :]
        vaug_sc[hh, MOBA_DH:, :] = jnp.where(ones_row, 1.0, 0.0).astype(BF16)
        mb_sc[hh, nb:, :] = jnp.full((mb_sc.shape[1] - nb, seq), NEG_INF, F32)
        m_sc[hh] = jnp.full(m_sc.shape[1:], NEG_INF, F32)
        acc_sc[hh] = jnp.zeros(acc_sc.shape[1:], F32)
        km_hi, km_lo = _split_hi_lo(jnp.where(lane_head == hh, kmean, 0.0))
        far_bias = farb_ref[hp * HEADS_PER_STEP + hh] * LOG2E
        for c0 in range(0, seq, GATE_COLS):
            cw = min(GATE_COLS, seq - c0)
            qc = q_ref[0, :, c0:c0 + cw]
            g = (jnp.dot(km_hi, qc, preferred_element_type=F32)
                 + jnp.dot(km_lo, qc, preferred_element_type=F32))
            key_blk = lax.broadcasted_iota(jnp.int32, (nb, cw), 0)
            qry_blk = (lax.broadcasted_iota(jnp.int32, (nb, cw), 1) + c0) // blk
            g = jnp.where(key_blk < qry_blk, g, -jnp.inf)
            key_f = key_blk.astype(F32)
            picked = jnp.zeros((nb, cw), F32)
            for _ in range(MOBA_TOPK):
                best = jnp.max(g, axis=0, keepdims=True)
                first = jnp.min(jnp.where(g == best, key_f, float(nb)), axis=0, keepdims=True)
                found = jnp.where(best > -jnp.inf, 1.0, 0.0)
                picked = picked + jnp.where(key_f == first, found, 0.0)
                g = jnp.where(key_f == first, -jnp.inf, g)
            mb_sc[hh, :nb, c0:c0 + cw] = jnp.where(
                picked > 0.0, jnp.where(key_blk < qry_blk - 1, far_bias, 0.0),
                jnp.where(key_blk == qry_blk, 0.0, NEG_INF))

    feat = lax.broadcasted_iota(jnp.int32, (LANES, blk), 0) // MOBA_DH
    for n in range(nb):
        qb = q_ref[0, :, n * blk:(n + 1) * blk]
        for hh in heads:
            c0 = (n * HEADS_PER_STEP + hh) * blk
            qz_sc[:, c0:c0 + blk] = jnp.where(feat == hh, qb, jnp.zeros_like(qb))

    def run_pass(sched_ref, n_tiles, with_table):
        def col(idx, r):
            return pl.multiple_of(sched_ref[r, idx] * blk, blk)

        def logits(idx, slot):
            qcols = HEADS_PER_STEP * blk
            qz = qz_sc[:, pl.ds(pl.multiple_of(sched_ref[0, idx] * qcols, qcols), qcols)]
            s = jnp.dot(k_ref[0, pl.ds(col(idx, 1), blk), :], qz, preferred_element_type=F32)
            if with_table:
                tab = sched_ref[3, idx]
                s = s + jnp.concatenate([bias_ref[hh, tab] for hh in heads], axis=1)
            s_sc[slot] = s
            cmax_sc[slot] = jnp.max(s.reshape(blk // SUBLANES, SUBLANES, s.shape[1]), axis=0)

        def softmax_values(idx, slot):
            n = sched_ref[0, idx]
            qcol = col(idx, 0)
            kcol = col(idx, 1)
            row = sched_ref[2, idx]
            for hh in heads:
                cols = slice(hh * blk, (hh + 1) * blk)
                cm = jnp.max(cmax_sc[slot, :, cols], axis=0, keepdims=True)
                mb = mb_sc[hh, pl.ds(row, 1), pl.ds(qcol, blk)]
                m_old = m_sc[hh, n]
                m_new = jnp.maximum(m_old, cm + mb)
                m_sc[hh, n] = m_new
                p = jnp.exp2(s_sc[slot, :, cols] + (mb - m_new)).astype(BF16)
                pv = jnp.dot(vaug_sc[hh, :, pl.ds(kcol, blk)], p, preferred_element_type=F32)
                acc_sc[hh, n] = jnp.exp2(m_old - m_new) * acc_sc[hh, n] + pv

        for i in range(PIPE):
            logits(i, i)
            if i >= LAG:
                softmax_values(i - LAG, (i - LAG) % PIPE)

        def body(d, carry):
            for u in range(PIPE):
                i = PIPE + PIPE * d + u
                logits(i, u)
                softmax_values(i - LAG, (u - LAG) % PIPE)
            return carry

        lax.fori_loop(0, _pipe_trips(n_tiles), body, 0)

    run_pass(near_ref, n_near, True)
    run_pass(far_ref, n_far, False)

    for n in range(nb):
        out_t = jnp.concatenate(
            [acc_sc[hh, n, :MOBA_DH, :] * (1.0 / acc_sc[hh, n, MOBA_DH:MOBA_DH + 1, :])
             for hh in heads], axis=0)
        y_ref[0, n * blk:(n + 1) * blk, :] = out_t.T.astype(y_ref.dtype)


def _moba(mqv, mk, bias_tab, far_bias):
    b, s, _ = mk.shape
    blk = MOBA_BLOCK
    nb = s // blk
    npair = MOBA_HEADS // HEADS_PER_STEP
    (near, n_near), (far, n_far) = _moba_schedules(nb)
    smem = pl.BlockSpec(memory_space=pltpu.SMEM)
    return pl.pallas_call(
        functools.partial(_moba_kernel, n_near=n_near, n_far=n_far),
        grid=(b, npair),
        in_specs=[
            smem, smem, smem,
            pl.BlockSpec((1, LANES, s), lambda i, p: (i, p, 0)),
            pl.BlockSpec((1, LANES, s), lambda i, p: (i, npair + p, 0)),
            pl.BlockSpec((1, s, LANES), lambda i, p: (i, 0, p)),
            pl.BlockSpec((HEADS_PER_STEP, N_TABLES, blk, blk), lambda i, p: (p, 0, 0, 0)),
        ],
        out_specs=pl.BlockSpec((1, s, LANES), lambda i, p: (i, 0, p)),
        out_shape=jax.ShapeDtypeStruct((b, s, MOBA_W), BF16),
        scratch_shapes=[
            pltpu.VMEM((HEADS_PER_STEP, V_AUG, s), BF16),
            pltpu.VMEM((HEADS_PER_STEP, nb + SUBLANES, s), F32),
            pltpu.VMEM((LANES, HEADS_PER_STEP * s), BF16),
            pltpu.VMEM((PIPE, blk, HEADS_PER_STEP * blk), F32),
            pltpu.VMEM((PIPE, SUBLANES, HEADS_PER_STEP * blk), F32),
            pltpu.VMEM((HEADS_PER_STEP, nb, 1, blk), F32),
            pltpu.VMEM((HEADS_PER_STEP, nb, V_AUG, blk), F32),
        ],
        compiler_params=_params("parallel", "parallel"),
        name="moba",
    )(jnp.asarray(near), jnp.asarray(far), far_bias, mqv, mqv, mk, bias_tab)


def _mix_ffn_kernel(x_ref, lru_ref, gla_ref, moba_ref, wl_ref, wgl_ref, wm_ref, gmix_ref,
                    gpre_ref, wg_ref, wu_ref, wd_ref, gpost_ref, o_ref):
    rows_per = x_ref.shape[0] // ROW_CHAINS

    def chain(c):
        rows = slice(c * rows_per, (c + 1) * rows_per)
        y = jnp.dot(lru_ref[rows], wl_ref[...], preferred_element_type=F32)
        y = y + jnp.dot(gla_ref[rows], wgl_ref[...], preferred_element_type=F32)
        y = y + jnp.dot(moba_ref[rows], wm_ref[...], preferred_element_type=F32)
        yield
        x = x_ref[rows] + _rms(y, gmix_ref[...])
        h = _rms(x, gpre_ref[...]).astype(BF16)
        yield
        a = jax.nn.silu(jnp.dot(h, wg_ref[...], preferred_element_type=F32))
        yield
        a = (a * jnp.dot(h, wu_ref[...], preferred_element_type=F32)).astype(BF16)
        yield
        f = jnp.dot(a, wd_ref[...], preferred_element_type=F32)
        yield
        o_ref[rows] = x + _rms(f, gpost_ref[...])

    _round_robin([chain(c) for c in range(ROW_CHAINS)])


def _mix_ffn(x, y_lru, y_gla, y_moba, w_out, gmix, gpre, wg, wu, wd, gpost, tt):
    t, d = x.shape
    wl = w_out[:LRU_W]
    wgl = w_out[LRU_W:LRU_W + GLA_V_W]
    wm = w_out[LRU_W + GLA_V_W:]
    row = lambda w: pl.BlockSpec((tt, w), lambda i: (i, 0))
    once = lambda shape: pl.BlockSpec(shape, lambda i: (0, 0), pipeline_mode=pl.Buffered(1))
    return pl.pallas_call(
        _mix_ffn_kernel,
        grid=(t // tt,),
        in_specs=[row(d), row(LRU_W), row(GLA_V_W), row(MOBA_W),
                  once(wl.shape), once(wgl.shape), once(wm.shape), _const_spec((1, d)),
                  _const_spec((1, d)), once(wg.shape), once(wu.shape), once(wd.shape),
                  _const_spec((1, d))],
        out_specs=row(d),
        out_shape=jax.ShapeDtypeStruct((t, d), F32),
        compiler_params=_params("parallel"),
        name="mix_ffn",
    )(x, y_lru, y_gla, y_moba, wl, wgl, wm, gmix, gpre, wg, wu, wd, gpost)


def _block_diag(w):
    g, n, _ = w.shape
    eye = jnp.eye(g, dtype=w.dtype)
    return (eye[:, None, :, None] * w[:, :, None, :]).reshape(g * n, g * n)


def _split_w_in(w_in):
    sizes = (LRU_W, LRU_W, GLA_QK_W, GLA_QK_W, GLA_V_W, GLA_RANK, GLA_V_W, MOBA_W, MOBA_W, MOBA_W)
    offs = np.cumsum((0,) + sizes)
    col = lambda i: w_in[:, offs[i]:offs[i + 1]]
    lru_x, lru_g, gq, gk, gv, g_lr, g_out, mq, mk, mv = (col(i) for i in range(len(sizes)))
    g_lr = jnp.pad(g_lr, ((0, 0), (0, LANES - GLA_RANK)))
    w_nat = jnp.concatenate([lru_x, lru_g, gq, gk, gv, g_out, g_lr, mk], axis=1).astype(BF16)
    w_t = jnp.concatenate([mq, mv], axis=1).T.astype(BF16)
    return w_nat, w_t


def kernel(x, pre_mix_norm, post_mix_norm, pre_ffn_norm, post_ffn_norm, w_in, w_out, lru_conv_w, lru_conv_b, lru_wa, lru_ba, lru_wx, lru_bx, lru_lambda, gla_gate_w2, gla_gate_b, gla_norm, rel_bias, w_ffn_gate, w_ffn_up, w_ffn_down):
    b, s, d = x.shape
    assert s % MOBA_BLOCK == 0 and d == LRU_W + GLA_V_W + MOBA_W
    assert lru_wa.shape[1] == LRU_BLOCKS and (b * s) % TOKEN_TILE == 0
    depth = w_in.shape[0]
    ts_seq = SEQ_TILE
    nbat = next(n for n in (8, 4, 2, 1) if b % n == 0)
    tt = TOKEN_TILE
    row = lambda v: v.reshape(1, -1).astype(F32)

    bias_tab = _bias_tables(rel_bias)
    for l in range(depth):
        w_nat, w_t = _split_w_in(w_in[l])
        w2 = jnp.pad(gla_gate_w2[l], ((0, LANES - GLA_RANK), (0, 0))).astype(BF16)
        mk, mqv, y_lru, y_gla = _front(
            x, row(pre_mix_norm[l]), w_nat, w_t,
            (lru_conv_w[l].astype(F32), row(lru_conv_b[l]),
             _block_diag(lru_wa[l]).astype(BF16), row(lru_ba[l]),
             _block_diag(lru_wx[l]).astype(BF16), row(lru_bx[l]), row(lru_lambda[l])),
            (w2, row(gla_gate_b[l]), row(gla_norm[l])), ts_seq, nbat)
        y_moba = _moba(mqv, mk, bias_tab, rel_bias[REL_BUCKETS - 1].astype(F32))
        x2 = _mix_ffn(x.reshape(b * s, d), y_lru.reshape(b * s, -1), y_gla.reshape(b * s, -1),
                      y_moba.reshape(b * s, -1), w_out[l].astype(BF16), row(post_mix_norm[l]),
                      row(pre_ffn_norm[l]), w_ffn_gate[l].astype(BF16), w_ffn_up[l].astype(BF16),
                      w_ffn_down[l].astype(BF16), row(post_ffn_norm[l]), tt)
        x = x2.reshape(b, s, d)
    return x
```

```python
import functools
import math

import numpy as np
import jax
import jax.numpy as jnp
from jax import lax
from jax.experimental import pallas as pl
from jax.experimental.pallas import tpu as pltpu

F32 = jnp.float32
BF16 = jnp.bfloat16

LRU_W = 256
LRU_BLOCKS = 4
LRU_CONV = 4
LRU_C = 8.0
GLA_HEADS = 4
GLA_DV = 64
GLA_DK = 32
GLA_RANK = 16
GLA_TAU = 16.0
GLA_CHUNK = 64
GLA_QK_W = GLA_HEADS * GLA_DK
GLA_V_W = GLA_HEADS * GLA_DV
MOBA_HEADS = 8
MOBA_DH = 64
MOBA_BLOCK = 256
MOBA_TOPK = 3
MOBA_W = MOBA_HEADS * MOBA_DH
REL_BUCKETS = 32
REL_MAX_DIST = 128
RMS_EPS = 1e-6
NEG_INF = -1e30
LOG2E = math.log2(math.e)

LANES = 128
SUBLANES = 8
VMEM_LIMIT = 56 * 1024 * 1024

GLA_IN_W = GLA_QK_W * 2 + GLA_V_W * 2 + LANES
HEADS_PER_STEP = LANES // MOBA_DH

ROW_CHAINS = 2
SEQ_TILE = MOBA_BLOCK
TOKEN_TILE = 512

NT_DIMS = (((1,), (1,)), ((), ()))
TN_DIMS = (((0,), (0,)), ((), ()))


def _rms(x, g):
    return x * lax.rsqrt(jnp.mean(x * x, axis=-1, keepdims=True) + RMS_EPS) * g


def _split_hi_lo(x):
    hi = x.astype(BF16)
    return hi, (x - hi.astype(F32)).astype(BF16)


def _params(*sem):
    return pltpu.CompilerParams(dimension_semantics=sem, vmem_limit_bytes=VMEM_LIMIT)


def _const_spec(shape):
    return pl.BlockSpec(shape, lambda *_: (0,) * len(shape))


def _round_robin(chains):
    done = object()
    while chains:
        chains = [c for c in chains if next(c, done) is not done]


PROJ_COLS = 512


def _project(bb, x_ref, g_ref, wn_ref, wt_ref, lru_sc, gla_sc, mk_ref, mqv_ref):
    h = _rms(x_ref[bb], g_ref[...]).astype(BF16)
    yield
    natural = [(dst, c, min(PROJ_COLS, dst.shape[-1] - c))
               for dst in (lru_sc, gla_sc, mk_ref) for c in range(0, dst.shape[-1], PROJ_COLS)]
    col = 0
    for dst, c0, w in natural:
        pn = jnp.dot(h, wn_ref[:, col:col + w], preferred_element_type=F32)
        dst[bb, :, c0:c0 + w] = pn.astype(dst.dtype)
        col += w
        yield
    for r0 in range(0, 2 * MOBA_W, PROJ_COLS):
        pt = lax.dot_general(wt_ref[r0:r0 + PROJ_COLS, :], h, NT_DIMS, preferred_element_type=F32)
        if r0 < MOBA_W:
            pt = pt * (MOBA_DH ** -0.5 * LOG2E)
        mqv_ref[bb, r0:r0 + PROJ_COLS, :] = pt.astype(BF16)
        yield


def _shift_rows(x, prev_tail, k, row8):
    sh = pltpu.roll(x, k, axis=0)
    top = jnp.where(row8 < k, pltpu.roll(prev_tail, k, axis=0), sh[:SUBLANES])
    return jnp.concatenate([top, sh[SUBLANES:]], axis=0)


def _lru_tile(in_ref, cw_ref, cb_ref, wa_ref, ba_ref, wx_ref, bx_ref, lam_ref, y_ref,
              h_sc, tail_sc):
    j = pl.program_id(1)
    ts = in_ref.shape[1]
    row = lax.broadcasted_iota(jnp.int32, (ts, LRU_W), 0)
    row8 = lax.broadcasted_iota(jnp.int32, (SUBLANES, LRU_W), 0)
    def sequence(bb):
        xb = in_ref[bb, :, :LRU_W]
        gb = in_ref[bb, :, LRU_W:]
        tail = tail_sc[bb]
        xc = xb * cw_ref[LRU_CONV - 1:LRU_CONV, :] + cb_ref[...]
        for k in range(1, LRU_CONV):
            xc = xc + _shift_rows(xb, tail, k, row8) * cw_ref[LRU_CONV - 1 - k:LRU_CONV - k, :]
        tail_sc[bb] = xb[ts - SUBLANES:]
        yield

        xcb = xc.astype(BF16)
        r = jax.nn.sigmoid(jnp.dot(xcb, wa_ref[...], preferred_element_type=F32) + ba_ref[...])
        yield
        i = jax.nn.sigmoid(jnp.dot(xcb, wx_ref[...], preferred_element_type=F32) + bx_ref[...])
        yield
        log_a = -LRU_C * r * jax.nn.softplus(-lam_ref[...])
        a = jnp.exp(log_a)
        yield
        mult = jnp.sqrt(-jnp.tanh(log_a) * (a * a + 1.0))
        mult = jnp.where(row + j * ts == 0, 1.0, mult)
        u = mult * (i * xc)
        yield

        d = 1
        while d < ts:
            keep = row >= d
            a_sh = jnp.where(keep, pltpu.roll(a, d, axis=0), 1.0)
            u_sh = jnp.where(keep, pltpu.roll(u, d, axis=0), 0.0)
            u = a * u_sh + u
            a = a * a_sh
            d *= 2
            yield
        h = u + a * h_sc[bb]
        h_sc[bb] = h[ts - 1:ts]
        y_ref[bb] = (h * jax.nn.gelu(gb)).astype(y_ref.dtype)

    return [sequence(bb) for bb in range(in_ref.shape[0])]


def _gla_tile(in_ref, w2_ref, gb_ref, gain_ref, y_ref, st_sc):
    ts = in_ref.shape[1]
    c = GLA_CHUNK
    ri = lax.broadcasted_iota(jnp.int32, (ts, ts), 0)
    ci = lax.broadcasted_iota(jnp.int32, (ts, ts), 1)
    in_chunk = (ri // c == ci // c) & (ci <= ri)
    tri = jnp.where(in_chunk, 1.0, 0.0).astype(BF16)
    qk_lane = lax.broadcasted_iota(jnp.int32, (1, GLA_QK_W), 1) // GLA_DK
    v_lane = lax.broadcasted_iota(jnp.int32, (1, GLA_V_W), 1) // GLA_DV
    st_row = lax.broadcasted_iota(jnp.int32, (GLA_V_W, GLA_QK_W), 0) // GLA_DV
    st_col = lax.broadcasted_iota(jnp.int32, (GLA_V_W, GLA_QK_W), 1) // GLA_DK
    same_head = st_row == st_col
    hr = lax.broadcasted_iota(jnp.int32, (GLA_V_W, GLA_V_W), 0) // GLA_DV
    hc = lax.broadcasted_iota(jnp.int32, (GLA_V_W, GLA_V_W), 1) // GLA_DV
    head_mean = jnp.where(hr == hc, 1.0 / GLA_DV, 0.0).astype(BF16)
    chunks = range(ts // c)
    rows = [slice(n * c, (n + 1) * c) for n in chunks]

    def sequence(bb):
        q = in_ref[bb, :, :GLA_QK_W]
        k = in_ref[bb, :, GLA_QK_W:2 * GLA_QK_W]
        v = in_ref[bb, :, 2 * GLA_QK_W:2 * GLA_QK_W + GLA_V_W]
        g_out = in_ref[bb, :, 2 * GLA_QK_W + GLA_V_W:2 * GLA_QK_W + 2 * GLA_V_W]
        g_lr = in_ref[bb, :, 2 * GLA_QK_W + 2 * GLA_V_W:]

        gate = jnp.dot(g_lr.astype(BF16), w2_ref[...], preferred_element_type=F32) + gb_ref[...]
        log_alpha = jax.nn.log_sigmoid(gate) / GLA_TAU
        yield
        la_hi, la_lo = _split_hi_lo(log_alpha)
        cum = (jnp.dot(tri, la_hi, preferred_element_type=F32)
               + jnp.dot(tri, la_lo, preferred_element_type=F32))
        lasts = [cum[(n + 1) * c - 1:(n + 1) * c] for n in chunks]
        last_full = jnp.concatenate([jnp.broadcast_to(l, (c, GLA_QK_W)) for l in lasts], axis=0)
        yield

        q_d = (q * (GLA_DK ** -0.5) * jnp.exp(cum)).astype(BF16)
        k_d = (k * jnp.exp(-cum)).astype(BF16)
        yield
        k_l = (k * jnp.exp(last_full - cum)).astype(BF16)
        v_b = v.astype(BF16)
        yield

        o = jnp.zeros((ts, GLA_V_W), F32)
        for hh in range(GLA_HEADS):
            q_h = jnp.where(qk_lane == hh, q_d, jnp.zeros_like(q_d))
            attn = lax.dot_general(q_h, k_d, NT_DIMS, preferred_element_type=F32)
            attn = jnp.where(in_chunk, attn, 0.0).astype(BF16)
            v_h = jnp.where(v_lane == hh, v_b, jnp.zeros_like(v_b))
            o = o + jnp.dot(attn, v_h, preferred_element_type=F32)
            yield

        st = st_sc[bb]
        inter = []
        for n in chunks:
            inter.append(lax.dot_general(q_d[rows[n]], st.astype(BF16), NT_DIMS,
                                         preferred_element_type=F32))
            kv = lax.dot_general(v_b[rows[n]], k_l[rows[n]], TN_DIMS,
                                 preferred_element_type=F32)
            st = st * jnp.exp(lasts[n]) + jnp.where(same_head, kv, 0.0)
            yield
        st_sc[bb] = st
        o = o + jnp.concatenate(inter, axis=0)

        sq_hi, sq_lo = _split_hi_lo(o * o)
        ms = (jnp.dot(sq_hi, head_mean, preferred_element_type=F32)
              + jnp.dot(sq_lo, head_mean, preferred_element_type=F32))
        y = o * lax.rsqrt(ms + RMS_EPS) * gain_ref[...] * jax.nn.silu(g_out)
        y_ref[bb] = y.astype(y_ref.dtype)

    return [sequence(bb) for bb in range(in_ref.shape[0])]


SEQ_STAGGER = 2
N_PROJ_ARGS = 4
N_LRU_ARGS = 7
N_GLA_ARGS = 3


def _alternate(*gens):
    gens, done = list(gens), object()
    while gens:
        gens = [g for g in gens if next(g, done) is not done]
        yield


def _delayed(gen, steps):
    for _ in range(steps):
        yield
    yield from gen


def _front_kernel(*refs):
    proj_in = refs[:N_PROJ_ARGS]
    lru_in = refs[N_PROJ_ARGS:N_PROJ_ARGS + N_LRU_ARGS]
    gla_in = refs[N_PROJ_ARGS + N_LRU_ARGS:N_PROJ_ARGS + N_LRU_ARGS + N_GLA_ARGS]
    (mk_ref, mqv_ref, y_lru_ref, y_gla_ref,
     lru_sc, gla_sc, h_sc, tail_sc, st_sc) = refs[N_PROJ_ARGS + N_LRU_ARGS + N_GLA_ARGS:]

    @pl.when(pl.program_id(1) == 0)
    def _():
        h_sc[...] = jnp.zeros_like(h_sc)
        tail_sc[...] = jnp.zeros_like(tail_sc)
        st_sc[...] = jnp.zeros_like(st_sc)

    lru = _lru_tile(lru_sc, *lru_in, y_lru_ref, h_sc, tail_sc)
    gla = _gla_tile(gla_sc, *gla_in, y_gla_ref, st_sc)
    proj = [_project(bb, *proj_in, lru_sc, gla_sc, mk_ref, mqv_ref) for bb in range(len(lru))]

    def sequence(bb):
        yield from proj[bb]
        yield from _alternate(lru[bb], gla[bb])

    _round_robin([_delayed(sequence(bb), bb * SEQ_STAGGER) for bb in range(len(lru))])


def _front(x, g, wn, wt, lru_args, gla_args, ts, nbat):
    assert len(lru_args) == N_LRU_ARGS and len(gla_args) == N_GLA_ARGS
    b, s, d = x.shape
    tile = lambda w: pl.BlockSpec((nbat, ts, w), lambda i, j: (i, j, 0))
    consts = (g, wn, wt) + tuple(lru_args) + tuple(gla_args)
    return pl.pallas_call(
        _front_kernel,
        grid=(b // nbat, s // ts),
        in_specs=[tile(d)] + [_const_spec(a.shape) for a in consts],
        out_specs=[tile(MOBA_W),
                   pl.BlockSpec((nbat, 2 * MOBA_W, ts), lambda i, j: (i, 0, j)),
                   tile(LRU_W), tile(GLA_V_W)],
        out_shape=[jax.ShapeDtypeStruct((b, s, MOBA_W), BF16),
                   jax.ShapeDtypeStruct((b, 2 * MOBA_W, s), BF16),
                   jax.ShapeDtypeStruct((b, s, LRU_W), BF16),
                   jax.ShapeDtypeStruct((b, s, GLA_V_W), BF16)],
        scratch_shapes=[pltpu.VMEM((nbat, ts, 2 * LRU_W), F32),
                        pltpu.VMEM((nbat, ts, GLA_IN_W), F32),
                        pltpu.VMEM((nbat, 1, LRU_W), F32),
                        pltpu.VMEM((nbat, SUBLANES, LRU_W), F32),
                        pltpu.VMEM((nbat, GLA_V_W, GLA_QK_W), F32)],
        compiler_params=_params("parallel", "arbitrary"),
        name="front",
    )(x, *consts)


def _t5_bucket_np(rel):
    n = np.maximum(rel, 0)
    max_exact = REL_BUCKETS // 2
    nf = np.maximum(n, 1).astype(np.float32)
    large = max_exact + (np.log(nf / np.float32(max_exact)) / np.float32(math.log(REL_MAX_DIST / max_exact))
                         * np.float32(REL_BUCKETS - max_exact)).astype(np.int32)
    large = np.minimum(large, REL_BUCKETS - 1)
    return np.where(n < max_exact, n, large).astype(np.int32)


FAR_DIST = int(np.argmax(_t5_bucket_np(np.arange(4 * REL_MAX_DIST)) == REL_BUCKETS - 1))
assert np.all(_t5_bucket_np(np.arange(FAR_DIST, 1 << 16)) == REL_BUCKETS - 1)
assert FAR_DIST <= MOBA_BLOCK

N_TABLES = 2
V_AUG = MOBA_DH + 16
GATE_COLS = 1024


def _bias_kernel(bkt_ref, rb_ref, out_ref):
    h = pl.program_id(0)
    for t in range(2):
        bkt = bkt_ref[t]
        acc = jnp.zeros(bkt.shape, F32)
        for bb in range(REL_BUCKETS):
            acc = jnp.where(bkt == bb, rb_ref[bb, h], acc)
        out_ref[0, t] = jnp.where(bkt < 0, NEG_INF, acc * LOG2E)


def _bias_tables(rel_bias):
    ko = np.arange(MOBA_BLOCK)[:, None]
    qo = np.arange(MOBA_BLOCK)[None, :]
    own = np.where(qo >= ko, _t5_bucket_np(qo - ko), -1)
    bkt = np.stack([own, _t5_bucket_np(qo - ko + MOBA_BLOCK)]).astype(np.int32)
    return pl.pallas_call(
        _bias_kernel,
        grid=(MOBA_HEADS,),
        in_specs=[
            _const_spec((2, MOBA_BLOCK, MOBA_BLOCK)),
            pl.BlockSpec(memory_space=pltpu.SMEM),
        ],
        out_specs=pl.BlockSpec((1, N_TABLES, MOBA_BLOCK, MOBA_BLOCK), lambda h: (h, 0, 0, 0)),
        out_shape=jax.ShapeDtypeStruct((MOBA_HEADS, N_TABLES, MOBA_BLOCK, MOBA_BLOCK), F32),
        compiler_params=_params("parallel"),
        name="moba_bias",
    )(jnp.asarray(bkt), rel_bias.astype(F32))


PIPE = 8
LAG = PIPE // 2


def _pipe_trips(n_tiles):
    return -(-(n_tiles + LAG - PIPE) // PIPE)


def _moba_schedules(nb):
    near = [(n, n, n, 0) for n in range(nb)] + [(n, n - 1, n - 1, 1) for n in range(1, nb)]
    far = [(n, j, j, 0) for j in range(nb - 2) for n in range(j + 2, nb)]

    def table(tiles):
        length = PIPE + PIPE * _pipe_trips(len(tiles))
        rows = tiles + [(0, 0, nb, 0)] * (length - len(tiles))
        return np.asarray(rows, np.int32).T.copy(), len(tiles)

    return table(near), table(far)


def _moba_kernel(near_ref, far_ref, farb_ref, q_ref, v_ref, k_ref, bias_ref, y_ref,
                 vaug_sc, mb_sc, qz_sc, s_sc, cmax_sc, m_sc, acc_sc, *, n_near, n_far):
    hp = pl.program_id(0)
    blk = MOBA_BLOCK
    seq = k_ref.shape[1]
    nb = seq // blk
    heads = range(HEADS_PER_STEP)

    ones_row = lax.broadcasted_iota(jnp.int32, (V_AUG - MOBA_DH, seq), 0) == 0
    kmean = jnp.concatenate(
        [jnp.mean(k_ref[0, jb * blk:(jb + 1) * blk, :].astype(F32), axis=0, keepdims=True)
         for jb in range(nb)], axis=0)
    lane_head = lax.broadcasted_iota(jnp.int32, (nb, LANES), 1) // MOBA_DH
    for hh in heads:
        vaug_sc[hh, :MOBA_DH, :] = v_ref[0, hh * MOBA_DH:(hh + 1) * MOBA_DH, :]
        vaug_sc[hh, MOBA_DH:, :] = jnp.where(ones_row, 1.0, 0.0).astype(BF16)
        mb_sc[hh, nb:, :] = jnp.full((mb_sc.shape[1] - nb, seq), NEG_INF, F32)
        m_sc[hh] = jnp.full(m_sc.shape[1:], NEG_INF, F32)
        acc_sc[hh] = jnp.zeros(acc_sc.shape[1:], F32)
        km_hi, km_lo = _split_hi_lo(jnp.where(lane_head == hh, kmean, 0.0))
        far_bias = farb_ref[hp * HEADS_PER_STEP + hh] * LOG2E
        for c0 in range(0, seq, GATE_COLS):
            cw = min(GATE_COLS, seq - c0)
            qc = q_ref[0, :, c0:c0 + cw]
            g = (jnp.dot(km_hi, qc, preferred_element_type=F32)
                 + jnp.dot(km_lo, qc, preferred_element_type=F32))
            key_blk = lax.broadcasted_iota(jnp.int32, (nb, cw), 0)
            qry_blk = (lax.broadcasted_iota(jnp.int32, (nb, cw), 1) + c0) // blk
            g = jnp.where(key_blk < qry_blk, g, -jnp.inf)
            key_f = key_blk.astype(F32)
            picked = jnp.zeros((nb, cw), F32)
            for _ in range(MOBA_TOPK):
                best = jnp.max(g, axis=0, keepdims=True)
                first = jnp.min(jnp.where(g == best, key_f, float(nb)), axis=0, keepdims=True)
                found = jnp.where(best > -jnp.inf, 1.0, 0.0)
                picked = picked + jnp.where(key_f == first, found, 0.0)
                g = jnp.where(key_f == first, -jnp.inf, g)
            mb_sc[hh, :nb, c0:c0 + cw] = jnp.where(
                picked > 0.0, jnp.where(key_blk < qry_blk - 1, far_bias, 0.0),
                jnp.where(key_blk == qry_blk, 0.0, NEG_INF))

    feat = lax.broadcasted_iota(jnp.int32, (LANES, blk), 0) // MOBA_DH
    for n in range(nb):
        qb = q_ref[0, :, n * blk:(n + 1) * blk]
        for hh in heads:
            c0 = (n * HEADS_PER_STEP + hh) * blk
            qz_sc[:, c0:c0 + blk] = jnp.where(feat == hh, qb, jnp.zeros_like(qb))

    def run_pass(sched_ref, n_tiles, with_table):
        def col(idx, r):
            return pl.multiple_of(sched_ref[r, idx] * blk, blk)

        def logits(idx, slot):
            qcols = HEADS_PER_STEP * blk
            qz = qz_sc[:, pl.ds(pl.multiple_of(sched_ref[0, idx] * qcols, qcols), qcols)]
            s = jnp.dot(k_ref[0, pl.ds(col(idx, 1), blk), :], qz, preferred_element_type=F32)
            if with_table:
                tab = sched_ref[3, idx]
                s = s + jnp.concatenate([bias_ref[hh, tab] for hh in heads], axis=1)
            s_sc[slot] = s
            cmax_sc[slot] = jnp.max(s.reshape(blk // SUBLANES, SUBLANES, s.shape[1]), axis=0)

        def softmax_values(idx, slot):
            n = sched_ref[0, idx]
            qcol = col(idx, 0)
            kcol = col(idx, 1)
            row = sched_ref[2, idx]
            for hh in heads:
                cols = slice(hh * blk, (hh + 1) * blk)
                cm = jnp.max(cmax_sc[slot, :, cols], axis=0, keepdims=True)
                mb = mb_sc[hh, pl.ds(row, 1), pl.ds(qcol, blk)]
                m_old = m_sc[hh, n]
                m_new = jnp.maximum(m_old, cm + mb)
                m_sc[hh, n] = m_new
                p = jnp.exp2(s_sc[slot, :, cols] + (mb - m_new)).astype(BF16)
                pv = jnp.dot(vaug_sc[hh, :, pl.ds(kcol, blk)], p, preferred_element_type=F32)
                acc_sc[hh, n] = jnp.exp2(m_old - m_new) * acc_sc[hh, n] + pv

        for i in range(PIPE):
            logits(i, i)
            if i >= LAG:
                softmax_values(i - LAG, (i - LAG) % PIPE)

        def body(d, carry):
            for u in range(PIPE):
                i = PIPE + PIPE * d + u
                logits(i, u)
                softmax_values(i - LAG, (u - LAG) % PIPE)
            return carry

        lax.fori_loop(0, _pipe_trips(n_tiles), body, 0)

    run_pass(near_ref, n_near, True)
    run_pass(far_ref, n_far, False)

    for n in range(nb):
        out_t = jnp.concatenate(
            [acc_sc[hh, n, :MOBA_DH, :] * (1.0 / acc_sc[hh, n, MOBA_DH:MOBA_DH + 1, :])
             for hh in heads], axis=0)
        y_ref[0, n * blk:(n + 1) * blk, :] = out_t.T.astype(y_ref.dtype)


def _moba(mqv, mk, bias_tab, far_bias):
    b, s, _ = mk.shape
    blk = MOBA_BLOCK
    nb = s // blk
    npair = MOBA_HEADS // HEADS_PER_STEP
    (near, n_near), (far, n_far) = _moba_schedules(nb)
    smem = pl.BlockSpec(memory_space=pltpu.SMEM)
    return pl.pallas_call(
        functools.partial(_moba_kernel, n_near=n_near, n_far=n_far),
        grid=(npair, b),
        in_specs=[
            smem, smem, smem,
            pl.BlockSpec((1, LANES, s), lambda p, i: (i, p, 0)),
            pl.BlockSpec((1, LANES, s), lambda p, i: (i, npair + p, 0)),
            pl.BlockSpec((1, s, LANES), lambda p, i: (i, 0, p)),
            pl.BlockSpec((HEADS_PER_STEP, N_TABLES, blk, blk), lambda p, i: (p, 0, 0, 0)),
        ],
        out_specs=pl.BlockSpec((1, s, LANES), lambda p, i: (i, 0, p)),
        out_shape=jax.ShapeDtypeStruct((b, s, MOBA_W), BF16),
        scratch_shapes=[
            pltpu.VMEM((HEADS_PER_STEP, V_AUG, s), BF16),
            pltpu.VMEM((HEADS_PER_STEP, nb + SUBLANES, s), F32),
            pltpu.VMEM((LANES, HEADS_PER_STEP * s), BF16),
            pltpu.VMEM((PIPE, blk, HEADS_PER_STEP * blk), F32),
            pltpu.VMEM((PIPE, SUBLANES, HEADS_PER_STEP * blk), F32),
            pltpu.VMEM((HEADS_PER_STEP, nb, 1, blk), F32),
            pltpu.VMEM((HEADS_PER_STEP, nb, V_AUG, blk), F32),
        ],
        compiler_params=_params("parallel", "parallel"),
        name="moba",
    )(jnp.asarray(near), jnp.asarray(far), far_bias, mqv, mqv, mk, bias_tab)


def _mix_ffn_kernel(x_ref, lru_ref, gla_ref, moba_ref, wl_ref, wgl_ref, wm_ref, gmix_ref,
                    gpre_ref, wg_ref, wu_ref, wd_ref, gpost_ref, o_ref):
    rows_per = x_ref.shape[0] // ROW_CHAINS

    def chain(c):
        rows = slice(c * rows_per, (c + 1) * rows_per)
        y = jnp.dot(lru_ref[rows], wl_ref[...], preferred_element_type=F32)
        y = y + jnp.dot(gla_ref[rows], wgl_ref[...], preferred_element_type=F32)
        y = y + jnp.dot(moba_ref[rows], wm_ref[...], preferred_element_type=F32)
        yield
        x = x_ref[rows] + _rms(y, gmix_ref[...])
        h = _rms(x, gpre_ref[...]).astype(BF16)
        yield
        a = jax.nn.silu(jnp.dot(h, wg_ref[...], preferred_element_type=F32))
        yield
        a = (a * jnp.dot(h, wu_ref[...], preferred_element_type=F32)).astype(BF16)
        yield
        f = jnp.dot(a, wd_ref[...], preferred_element_type=F32)
        yield
        o_ref[rows] = x + _rms(f, gpost_ref[...])

    _round_robin([chain(c) for c in range(ROW_CHAINS)])


def _mix_ffn(x, y_lru, y_gla, y_moba, w_out, gmix, gpre, wg, wu, wd, gpost, tt):
    t, d = x.shape
    wl = w_out[:LRU_W]
    wgl = w_out[LRU_W:LRU_W + GLA_V_W]
    wm = w_out[LRU_W + GLA_V_W:]
    row = lambda w: pl.BlockSpec((tt, w), lambda i: (i, 0))
    once = lambda shape: pl.BlockSpec(shape, lambda i: (0, 0), pipeline_mode=pl.Buffered(1))
    return pl.pallas_call(
        _mix_ffn_kernel,
        grid=(t // tt,),
        in_specs=[row(d), row(LRU_W), row(GLA_V_W), row(MOBA_W),
                  once(wl.shape), once(wgl.shape), once(wm.shape), _const_spec((1, d)),
                  _const_spec((1, d)), once(wg.shape), once(wu.shape), once(wd.shape),
                  _const_spec((1, d))],
        out_specs=row(d),
        out_shape=jax.ShapeDtypeStruct((t, d), F32),
        compiler_params=_params("parallel"),
        name="mix_ffn",
    )(x, y_lru, y_gla, y_moba, wl, wgl, wm, gmix, gpre, wg, wu, wd, gpost)


def _block_diag(w):
    g, n, _ = w.shape
    eye = jnp.eye(g, dtype=w.dtype)
    return (eye[:, None, :, None] * w[:, :, None, :]).reshape(g * n, g * n)


def _split_w_in(w_in):
    sizes = (LRU_W, LRU_W, GLA_QK_W, GLA_QK_W, GLA_V_W, GLA_RANK, GLA_V_W, MOBA_W, MOBA_W, MOBA_W)
    offs = np.cumsum((0,) + sizes)
    col = lambda i: w_in[:, offs[i]:offs[i + 1]]
    lru_x, lru_g, gq, gk, gv, g_lr, g_out, mq, mk, mv = (col(i) for i in range(len(sizes)))
    g_lr = jnp.pad(g_lr, ((0, 0), (0, LANES - GLA_RANK)))
    w_nat = jnp.concatenate([lru_x, lru_g, gq, gk, gv, g_out, g_lr, mk], axis=1).astype(BF16)
    w_t = jnp.concatenate([mq, mv], axis=1).T.astype(BF16)
    return w_nat, w_t


def kernel(x, pre_mix_norm, post_mix_norm, pre_ffn_norm, post_ffn_norm, w_in, w_out, lru_conv_w, lru_conv_b, lru_wa, lru_ba, lru_wx, lru_bx, lru_lambda, gla_gate_w2, gla_gate_b, gla_norm, rel_bias, w_ffn_gate, w_ffn_up, w_ffn_down):
    b, s, d = x.shape
    assert s % MOBA_BLOCK == 0 and d == LRU_W + GLA_V_W + MOBA_W
    assert lru_wa.shape[1] == LRU_BLOCKS and (b * s) % TOKEN_TILE == 0
    depth = w_in.shape[0]
    ts_seq = SEQ_TILE
    nbat = next(n for n in (8, 4, 2, 1) if b % n == 0)
    tt = TOKEN_TILE
    row = lambda v: v.reshape(1, -1).astype(F32)

    bias_tab = _bias_tables(rel_bias)
    for l in range(depth):
        w_nat, w_t = _split_w_in(w_in[l])
        w2 = jnp.pad(gla_gate_w2[l], ((0, LANES - GLA_RANK), (0, 0))).astype(BF16)
        mk, mqv, y_lru, y_gla = _front(
            x, row(pre_mix_norm[l]), w_nat, w_t,
            (lru_conv_w[l].astype(F32), row(lru_conv_b[l]),
             _block_diag(lru_wa[l]).astype(BF16), row(lru_ba[l]),
             _block_diag(lru_wx[l]).astype(BF16), row(lru_bx[l]), row(lru_lambda[l])),
            (w2, row(gla_gate_b[l]), row(gla_norm[l])), ts_seq, nbat)
        y_moba = _moba(mqv, mk, bias_tab, rel_bias[REL_BUCKETS - 1].astype(F32))
        x2 = _mix_ffn(x.reshape(b * s, d), y_lru.reshape(b * s, -1), y_gla.reshape(b * s, -1),
                      y_moba.reshape(b * s, -1), w_out[l].astype(BF16), row(post_mix_norm[l]),
                      row(pre_ffn_norm[l]), w_ffn_gate[l].astype(BF16), w_ffn_up[l].astype(BF16),
                      w_ffn_down[l].astype(BF16), row(post_ffn_norm[l]), tt)
        x = x2.reshape(b, s, d)
    return x
```

```python
import functools
import math

import numpy as np
import jax
import jax.numpy as jnp
from jax import lax
from jax.experimental import pallas as pl
from jax.experimental.pallas import tpu as pltpu

F32 = jnp.float32
BF16 = jnp.bfloat16

LRU_W = 256
LRU_BLOCKS = 4
LRU_CONV = 4
LRU_C = 8.0
GLA_HEADS = 4
GLA_DV = 64
GLA_DK = 32
GLA_RANK = 16
GLA_TAU = 16.0
GLA_CHUNK = 64
GLA_QK_W = GLA_HEADS * GLA_DK
GLA_V_W = GLA_HEADS * GLA_DV
MOBA_HEADS = 8
MOBA_DH = 64
MOBA_BLOCK = 256
MOBA_TOPK = 3
MOBA_W = MOBA_HEADS * MOBA_DH
REL_BUCKETS = 32
REL_MAX_DIST = 128
RMS_EPS = 1e-6
NEG_INF = -1e30
LOG2E = math.log2(math.e)

LANES = 128
SUBLANES = 8
VMEM_LIMIT = 56 * 1024 * 1024

GLA_IN_W = GLA_QK_W * 2 + GLA_V_W * 2 + LANES
HEADS_PER_STEP = LANES // MOBA_DH

ROW_CHAINS = 2
SEQ_TILE = MOBA_BLOCK
TOKEN_TILE = 512

NT_DIMS = (((1,), (1,)), ((), ()))
TN_DIMS = (((0,), (0,)), ((), ()))


def _rms(x, g):
    return x * lax.rsqrt(jnp.mean(x * x, axis=-1, keepdims=True) + RMS_EPS) * g


def _split_hi_lo(x):
    hi = x.astype(BF16)
    return hi, (x - hi.astype(F32)).astype(BF16)


def _params(*sem):
    return pltpu.CompilerParams(dimension_semantics=sem, vmem_limit_bytes=VMEM_LIMIT)


def _const_spec(shape):
    return pl.BlockSpec(shape, lambda *_: (0,) * len(shape))


def _round_robin(chains):
    done = object()
    while chains:
        chains = [c for c in chains if next(c, done) is not done]


PROJ_COLS = 512


def _project(bb, x_ref, g_ref, wn_ref, wt_ref, lru_sc, gla_sc, mk_ref, mqv_ref):
    h = _rms(x_ref[bb], g_ref[...]).astype(BF16)
    yield
    natural = [(dst, c, min(PROJ_COLS, dst.shape[-1] - c))
               for dst in (lru_sc, gla_sc, mk_ref) for c in range(0, dst.shape[-1], PROJ_COLS)]
    col = 0
    for dst, c0, w in natural:
        pn = jnp.dot(h, wn_ref[:, col:col + w], preferred_element_type=F32)
        dst[bb, :, c0:c0 + w] = pn.astype(dst.dtype)
        col += w
        yield
    for r0 in range(0, 2 * MOBA_W, PROJ_COLS):
        pt = lax.dot_general(wt_ref[r0:r0 + PROJ_COLS, :], h, NT_DIMS, preferred_element_type=F32)
        if r0 < MOBA_W:
            pt = pt * (MOBA_DH ** -0.5 * LOG2E)
        mqv_ref[bb, r0:r0 + PROJ_COLS, :] = pt.astype(BF16)
        yield


def _shift_rows(x, prev_tail, k, row8):
    sh = pltpu.roll(x, k, axis=0)
    top = jnp.where(row8 < k, pltpu.roll(prev_tail, k, axis=0), sh[:SUBLANES])
    return jnp.concatenate([top, sh[SUBLANES:]], axis=0)


def _lru_tile(in_ref, cw_ref, cb_ref, wa_ref, ba_ref, wx_ref, bx_ref, lam_ref, y_ref,
              h_sc, tail_sc):
    j = pl.program_id(1)
    ts = in_ref.shape[1]
    row = lax.broadcasted_iota(jnp.int32, (ts, LRU_W), 0)
    row8 = lax.broadcasted_iota(jnp.int32, (SUBLANES, LRU_W), 0)
    def sequence(bb):
        xb = in_ref[bb, :, :LRU_W]
        gb = in_ref[bb, :, LRU_W:]
        tail = tail_sc[bb]
        xc = xb * cw_ref[LRU_CONV - 1:LRU_CONV, :] + cb_ref[...]
        for k in range(1, LRU_CONV):
            xc = xc + _shift_rows(xb, tail, k, row8) * cw_ref[LRU_CONV - 1 - k:LRU_CONV - k, :]
        tail_sc[bb] = xb[ts - SUBLANES:]
        yield

        xcb = xc.astype(BF16)
        r = jax.nn.sigmoid(jnp.dot(xcb, wa_ref[...], preferred_element_type=F32) + ba_ref[...])
        yield
        i = jax.nn.sigmoid(jnp.dot(xcb, wx_ref[...], preferred_element_type=F32) + bx_ref[...])
        yield
        log_a = -LRU_C * r * jax.nn.softplus(-lam_ref[...])
        a = jnp.exp(log_a)
        yield
        mult = jnp.sqrt(-jnp.tanh(log_a) * (a * a + 1.0))
        mult = jnp.where(row + j * ts == 0, 1.0, mult)
        u = mult * (i * xc)
        yield

        d = 1
        while d < ts:
            keep = row >= d
            a_sh = jnp.where(keep, pltpu.roll(a, d, axis=0), 1.0)
            u_sh = jnp.where(keep, pltpu.roll(u, d, axis=0), 0.0)
            u = a * u_sh + u
            a = a * a_sh
            d *= 2
            yield
        h = u + a * h_sc[bb]
        h_sc[bb] = h[ts - 1:ts]
        y_ref[bb] = (h * jax.nn.gelu(gb)).astype(y_ref.dtype)

    return [sequence(bb) for bb in range(in_ref.shape[0])]


def _gla_tile(in_ref, w2_ref, gb_ref, gain_ref, y_ref, st_sc):
    ts = in_ref.shape[1]
    c = GLA_CHUNK
    ri = lax.broadcasted_iota(jnp.int32, (ts, ts), 0)
    ci = lax.broadcasted_iota(jnp.int32, (ts, ts), 1)
    in_chunk = (ri // c == ci // c) & (ci <= ri)
    tri = jnp.where(in_chunk, 1.0, 0.0).astype(BF16)
    qk_lane = lax.broadcasted_iota(jnp.int32, (1, GLA_QK_W), 1) // GLA_DK
    v_lane = lax.broadcasted_iota(jnp.int32, (1, GLA_V_W), 1) // GLA_DV
    st_row = lax.broadcasted_iota(jnp.int32, (GLA_V_W, GLA_QK_W), 0) // GLA_DV
    st_col = lax.broadcasted_iota(jnp.int32, (GLA_V_W, GLA_QK_W), 1) // GLA_DK
    same_head = st_row == st_col
    hr = lax.broadcasted_iota(jnp.int32, (GLA_V_W, GLA_V_W), 0) // GLA_DV
    hc = lax.broadcasted_iota(jnp.int32, (GLA_V_W, GLA_V_W), 1) // GLA_DV
    head_mean = jnp.where(hr == hc, 1.0 / GLA_DV, 0.0).astype(BF16)
    chunks = range(ts // c)
    rows = [slice(n * c, (n + 1) * c) for n in chunks]

    def sequence(bb):
        q = in_ref[bb, :, :GLA_QK_W]
        k = in_ref[bb, :, GLA_QK_W:2 * GLA_QK_W]
        v = in_ref[bb, :, 2 * GLA_QK_W:2 * GLA_QK_W + GLA_V_W]
        g_out = in_ref[bb, :, 2 * GLA_QK_W + GLA_V_W:2 * GLA_QK_W + 2 * GLA_V_W]
        g_lr = in_ref[bb, :, 2 * GLA_QK_W + 2 * GLA_V_W:]

        gate = jnp.dot(g_lr.astype(BF16), w2_ref[...], preferred_element_type=F32) + gb_ref[...]
        log_alpha = jax.nn.log_sigmoid(gate) / GLA_TAU
        yield
        la_hi, la_lo = _split_hi_lo(log_alpha)
        cum = (jnp.dot(tri, la_hi, preferred_element_type=F32)
               + jnp.dot(tri, la_lo, preferred_element_type=F32))
        lasts = [cum[(n + 1) * c - 1:(n + 1) * c] for n in chunks]
        last_full = jnp.concatenate([jnp.broadcast_to(l, (c, GLA_QK_W)) for l in lasts], axis=0)
        yield

        q_d = (q * (GLA_DK ** -0.5) * jnp.exp(cum)).astype(BF16)
        k_d = (k * jnp.exp(-cum)).astype(BF16)
        yield
        k_l = (k * jnp.exp(last_full - cum)).astype(BF16)
        v_b = v.astype(BF16)
        yield

        o = jnp.zeros((ts, GLA_V_W), F32)
        for hh in range(GLA_HEADS):
            q_h = jnp.where(qk_lane == hh, q_d, jnp.zeros_like(q_d))
            attn = lax.dot_general(q_h, k_d, NT_DIMS, preferred_element_type=F32)
            attn = jnp.where(in_chunk, attn, 0.0).astype(BF16)
            v_h = jnp.where(v_lane == hh, v_b, jnp.zeros_like(v_b))
            o = o + jnp.dot(attn, v_h, preferred_element_type=F32)
            yield

        st = st_sc[bb]
        inter = []
        for n in chunks:
            inter.append(lax.dot_general(q_d[rows[n]], st.astype(BF16), NT_DIMS,
                                         preferred_element_type=F32))
            kv = lax.dot_general(v_b[rows[n]], k_l[rows[n]], TN_DIMS,
                                 preferred_element_type=F32)
            st = st * jnp.exp(lasts[n]) + jnp.where(same_head, kv, 0.0)
            yield
        st_sc[bb] = st
        o = o + jnp.concatenate(inter, axis=0)

        sq_hi, sq_lo = _split_hi_lo(o * o)
        ms = (jnp.dot(sq_hi, head_mean, preferred_element_type=F32)
              + jnp.dot(sq_lo, head_mean, preferred_element_type=F32))
        y = o * lax.rsqrt(ms + RMS_EPS) * gain_ref[...] * jax.nn.silu(g_out)
        y_ref[bb] = y.astype(y_ref.dtype)

    return [sequence(bb) for bb in range(in_ref.shape[0])]


SEQ_STAGGER = 2
N_PROJ_ARGS = 4
N_LRU_ARGS = 7
N_GLA_ARGS = 3


def _alternate(*gens):
    gens, done = list(gens), object()
    while gens:
        gens = [g for g in gens if next(g, done) is not done]
        yield


def _delayed(gen, steps):
    for _ in range(steps):
        yield
    yield from gen


def _front_kernel(*refs):
    proj_in = refs[:N_PROJ_ARGS]
    lru_in = refs[N_PROJ_ARGS:N_PROJ_ARGS + N_LRU_ARGS]
    gla_in = refs[N_PROJ_ARGS + N_LRU_ARGS:N_PROJ_ARGS + N_LRU_ARGS + N_GLA_ARGS]
    (mk_ref, mqv_ref, y_lru_ref, y_gla_ref,
     lru_sc, gla_sc, h_sc, tail_sc, st_sc) = refs[N_PROJ_ARGS + N_LRU_ARGS + N_GLA_ARGS:]

    @pl.when(pl.program_id(1) == 0)
    def _():
        h_sc[...] = jnp.zeros_like(h_sc)
        tail_sc[...] = jnp.zeros_like(tail_sc)
        st_sc[...] = jnp.zeros_like(st_sc)

    lru = _lru_tile(lru_sc, *lru_in, y_lru_ref, h_sc, tail_sc)
    gla = _gla_tile(gla_sc, *gla_in, y_gla_ref, st_sc)
    proj = [_project(bb, *proj_in, lru_sc, gla_sc, mk_ref, mqv_ref) for bb in range(len(lru))]

    def sequence(bb):
        yield from proj[bb]
        yield from _alternate(lru[bb], gla[bb])

    _round_robin([_delayed(sequence(bb), bb * SEQ_STAGGER) for bb in range(len(lru))])


def _front(x, g, wn, wt, lru_args, gla_args, ts, nbat):
    assert len(lru_args) == N_LRU_ARGS and len(gla_args) == N_GLA_ARGS
    b, s, d = x.shape
    tile = lambda w: pl.BlockSpec((nbat, ts, w), lambda i, j: (i, j, 0))
    consts = (g, wn, wt) + tuple(lru_args) + tuple(gla_args)
    return pl.pallas_call(
        _front_kernel,
        grid=(b // nbat, s // ts),
        in_specs=[tile(d)] + [_const_spec(a.shape) for a in consts],
        out_specs=[tile(MOBA_W),
                   pl.BlockSpec((nbat, 2 * MOBA_W, ts), lambda i, j: (i, 0, j)),
                   tile(LRU_W), tile(GLA_V_W)],
        out_shape=[jax.ShapeDtypeStruct((b, s, MOBA_W), BF16),
                   jax.ShapeDtypeStruct((b, 2 * MOBA_W, s), BF16),
                   jax.ShapeDtypeStruct((b, s, LRU_W), BF16),
                   jax.ShapeDtypeStruct((b, s, GLA_V_W), BF16)],
        scratch_shapes=[pltpu.VMEM((nbat, ts, 2 * LRU_W), F32),
                        pltpu.VMEM((nbat, ts, GLA_IN_W), F32),
                        pltpu.VMEM((nbat, 1, LRU_W), F32),
                        pltpu.VMEM((nbat, SUBLANES, LRU_W), F32),
                        pltpu.VMEM((nbat, GLA_V_W, GLA_QK_W), F32)],
        compiler_params=_params("parallel", "arbitrary"),
        name="front",
    )(x, *consts)


def _t5_bucket_np(rel):
    n = np.maximum(rel, 0)
    max_exact = REL_BUCKETS // 2
    nf = np.maximum(n, 1).astype(np.float32)
    large = max_exact + (np.log(nf / np.float32(max_exact)) / np.float32(math.log(REL_MAX_DIST / max_exact))
                         * np.float32(REL_BUCKETS - max_exact)).astype(np.int32)
    large = np.minimum(large, REL_BUCKETS - 1)
    return np.where(n < max_exact, n, large).astype(np.int32)


FAR_DIST = int(np.argmax(_t5_bucket_np(np.arange(4 * REL_MAX_DIST)) == REL_BUCKETS - 1))
assert np.all(_t5_bucket_np(np.arange(FAR_DIST, 1 << 16)) == REL_BUCKETS - 1)
assert FAR_DIST <= MOBA_BLOCK

N_TABLES = 2
V_AUG = MOBA_DH + 16
GATE_COLS = 1024


def _bias_kernel(bkt_ref, rb_ref, out_ref):
    h = pl.program_id(0)
    for t in range(2):
        bkt = bkt_ref[t]
        acc = jnp.zeros(bkt.shape, F32)
        for bb in range(REL_BUCKETS):
            acc = jnp.where(bkt == bb, rb_ref[bb, h], acc)
        out_ref[0, t] = jnp.where(bkt < 0, NEG_INF, acc * LOG2E)


def _bias_tables(rel_bias):
    ko = np.arange(MOBA_BLOCK)[:, None]
    qo = np.arange(MOBA_BLOCK)[None, :]
    own = np.where(qo >= ko, _t5_bucket_np(qo - ko), -1)
    bkt = np.stack([own, _t5_bucket_np(qo - ko + MOBA_BLOCK)]).astype(np.int32)
    return pl.pallas_call(
        _bias_kernel,
        grid=(MOBA_HEADS,),
        in_specs=[
            _const_spec((2, MOBA_BLOCK, MOBA_BLOCK)),
            pl.BlockSpec(memory_space=pltpu.SMEM),
        ],
        out_specs=pl.BlockSpec((1, N_TABLES, MOBA_BLOCK, MOBA_BLOCK), lambda h: (h, 0, 0, 0)),
        out_shape=jax.ShapeDtypeStruct((MOBA_HEADS, N_TABLES, MOBA_BLOCK, MOBA_BLOCK), F32),
        compiler_params=_params("parallel"),
        name="moba_bias",
    )(jnp.asarray(bkt), rel_bias.astype(F32))


NEAR_PIPE = 12
FAR_PIPE = 16
LAG = 4


def _pipe_trips(n_tiles, pipe):
    return -(-(n_tiles + LAG - pipe) // pipe)


def _moba_schedules(nb):
    near = [(n, n, n, 0) for n in range(nb)] + [(n, n - 1, n - 1, 1) for n in range(1, nb)]
    far = [(n, j, j, 0) for j in range(nb - 2) for n in range(j + 2, nb)]

    def table(tiles, pipe):
        length = pipe + pipe * _pipe_trips(len(tiles), pipe)
        rows = tiles + [(0, 0, nb, 0)] * (length - len(tiles))
        return np.asarray(rows, np.int32).T.copy(), len(tiles)

    return table(near, NEAR_PIPE), table(far, FAR_PIPE)


def _moba_kernel(near_ref, far_ref, farb_ref, q_ref, v_ref, k_ref, bias_ref, y_ref,
                 vaug_sc, mb_sc, qz_sc, s_sc, cmax_sc, m_sc, acc_sc, *, n_near, n_far):
    hp = pl.program_id(0)
    blk = MOBA_BLOCK
    seq = k_ref.shape[1]
    nb = seq // blk
    heads = range(HEADS_PER_STEP)

    ones_row = lax.broadcasted_iota(jnp.int32, (V_AUG - MOBA_DH, seq), 0) == 0
    kmean = jnp.concatenate(
        [jnp.mean(k_ref[0, jb * blk:(jb + 1) * blk, :].astype(F32), axis=0, keepdims=True)
         for jb in range(nb)], axis=0)
    lane_head = lax.broadcasted_iota(jnp.int32, (nb, LANES), 1) // MOBA_DH
    for hh in heads:
        vaug_sc[hh, :MOBA_DH, :] = v_ref[0, hh * MOBA_DH:(hh + 1) * MOBA_DH, :]
        vaug_sc[hh, MOBA_DH:, :] = jnp.where(ones_row, 1.0, 0.0).astype(BF16)
        mb_sc[hh, nb:, :] = jnp.full((mb_sc.shape[1] - nb, seq), NEG_INF, F32)
        m_sc[hh] = jnp.full(m_sc.shape[1:], NEG_INF, F32)
        acc_sc[hh] = jnp.zeros(acc_sc.shape[1:], F32)
        km_hi, km_lo = _split_hi_lo(jnp.where(lane_head == hh, kmean, 0.0))
        far_bias = farb_ref[hp * HEADS_PER_STEP + hh] * LOG2E
        for c0 in range(0, seq, GATE_COLS):
            cw = min(GATE_COLS, seq - c0)
            qc = q_ref[0, :, c0:c0 + cw]
            g = (jnp.dot(km_hi, qc, preferred_element_type=F32)
                 + jnp.dot(km_lo, qc, preferred_element_type=F32))
            key_blk = lax.broadcasted_iota(jnp.int32, (nb, cw), 0)
            qry_blk = (lax.broadcasted_iota(jnp.int32, (nb, cw), 1) + c0) // blk
            g = jnp.where(key_blk < qry_blk, g, -jnp.inf)
            key_f = key_blk.astype(F32)
            picked = jnp.zeros((nb, cw), F32)
            for _ in range(MOBA_TOPK):
                best = jnp.max(g, axis=0, keepdims=True)
                first = jnp.min(jnp.where(g == best, key_f, float(nb)), axis=0, keepdims=True)
                found = jnp.where(best > -jnp.inf, 1.0, 0.0)
                picked = picked + jnp.where(key_f == first, found, 0.0)
                g = jnp.where(key_f == first, -jnp.inf, g)
            mb_sc[hh, :nb, c0:c0 + cw] = jnp.where(
                picked > 0.0, jnp.where(key_blk < qry_blk - 1, far_bias, 0.0),
                jnp.where(key_blk == qry_blk, 0.0, NEG_INF))

    feat = lax.broadcasted_iota(jnp.int32, (LANES, blk), 0) // MOBA_DH
    for n in range(nb):
        qb = q_ref[0, :, n * blk:(n + 1) * blk]
        for hh in heads:
            c0 = (n * HEADS_PER_STEP + hh) * blk
            qz_sc[:, c0:c0 + blk] = jnp.where(feat == hh, qb, jnp.zeros_like(qb))

    def run_pass(sched_ref, n_tiles, with_table, pipe):
        def col(idx, r):
            return pl.multiple_of(sched_ref[r, idx] * blk, blk)

        def logits(idx, slot):
            qcols = HEADS_PER_STEP * blk
            qz = qz_sc[:, pl.ds(pl.multiple_of(sched_ref[0, idx] * qcols, qcols), qcols)]
            s = jnp.dot(k_ref[0, pl.ds(col(idx, 1), blk), :], qz, preferred_element_type=F32)
            if with_table:
                tab = sched_ref[3, idx]
                s = s + jnp.concatenate([bias_ref[hh, tab] for hh in heads], axis=1)
            s_sc[slot] = s
            cmax_sc[slot] = jnp.max(s.reshape(blk // SUBLANES, SUBLANES, s.shape[1]), axis=0)

        def softmax_values(idx, slot):
            n = sched_ref[0, idx]
            qcol = col(idx, 0)
            kcol = col(idx, 1)
            row = sched_ref[2, idx]
            for hh in heads:
                cols = slice(hh * blk, (hh + 1) * blk)
                cm = jnp.max(cmax_sc[slot, :, cols], axis=0, keepdims=True)
                mb = mb_sc[hh, pl.ds(row, 1), pl.ds(qcol, blk)]
                m_old = m_sc[hh, n]
                m_new = jnp.maximum(m_old, cm + mb)
                m_sc[hh, n] = m_new
                p = jnp.exp2(s_sc[slot, :, cols] + (mb - m_new)).astype(BF16)
                pv = jnp.dot(vaug_sc[hh, :, pl.ds(kcol, blk)], p, preferred_element_type=F32)
                acc_sc[hh, n] = jnp.exp2(m_old - m_new) * acc_sc[hh, n] + pv

        for i in range(pipe):
            logits(i, i)
            if i >= LAG:
                softmax_values(i - LAG, (i - LAG) % pipe)

        def body(d, carry):
            for u in range(pipe):
                i = pipe + pipe * d + u
                logits(i, u)
                softmax_values(i - LAG, (u - LAG) % pipe)
            return carry

        lax.fori_loop(0, _pipe_trips(n_tiles, pipe), body, 0)

    run_pass(near_ref, n_near, True, NEAR_PIPE)
    run_pass(far_ref, n_far, False, FAR_PIPE)

    for n in range(nb):
        out_t = jnp.concatenate(
            [acc_sc[hh, n, :MOBA_DH, :] * (1.0 / acc_sc[hh, n, MOBA_DH:MOBA_DH + 1, :])
             for hh in heads], axis=0)
        y_ref[0, n * blk:(n + 1) * blk, :] = out_t.T.astype(y_ref.dtype)


def _moba(mqv, mk, bias_tab, far_bias):
    b, s, _ = mk.shape
    blk = MOBA_BLOCK
    nb = s // blk
    npair = MOBA_HEADS // HEADS_PER_STEP
    (near, n_near), (far, n_far) = _moba_schedules(nb)
    smem = pl.BlockSpec(memory_space=pltpu.SMEM)
    slots = max(NEAR_PIPE, FAR_PIPE)
    return pl.pallas_call(
        functools.partial(_moba_kernel, n_near=n_near, n_far=n_far),
        grid=(npair, b),
        in_specs=[
            smem, smem, smem,
            pl.BlockSpec((1, LANES, s), lambda p, i: (i, p, 0)),
            pl.BlockSpec((1, LANES, s), lambda p, i: (i, npair + p, 0)),
            pl.BlockSpec((1, s, LANES), lambda p, i: (i, 0, p)),
            pl.BlockSpec((HEADS_PER_STEP, N_TABLES, blk, blk), lambda p, i: (p, 0, 0, 0)),
        ],
        out_specs=pl.BlockSpec((1, s, LANES), lambda p, i: (i, 0, p)),
        out_shape=jax.ShapeDtypeStruct((b, s, MOBA_W), BF16),
        scratch_shapes=[
            pltpu.VMEM((HEADS_PER_STEP, V_AUG, s), BF16),
            pltpu.VMEM((HEADS_PER_STEP, nb + SUBLANES, s), F32),
            pltpu.VMEM((LANES, HEADS_PER_STEP * s), BF16),
            pltpu.VMEM((slots, blk, HEADS_PER_STEP * blk), F32),
            pltpu.VMEM((slots, SUBLANES, HEADS_PER_STEP * blk), F32),
            pltpu.VMEM((HEADS_PER_STEP, nb, 1, blk), F32),
            pltpu.VMEM((HEADS_PER_STEP, nb, V_AUG, blk), F32),
        ],
        compiler_params=_params("parallel", "parallel"),
        name="moba",
    )(jnp.asarray(near), jnp.asarray(far), far_bias, mqv, mqv, mk, bias_tab)


def _mix_ffn_kernel(x_ref, lru_ref, gla_ref, moba_ref, wl_ref, wgl_ref, wm_ref, gmix_ref,
                    gpre_ref, wg_ref, wu_ref, wd_ref, gpost_ref, o_ref):
    rows_per = x_ref.shape[0] // ROW_CHAINS

    def chain(c):
        rows = slice(c * rows_per, (c + 1) * rows_per)
        y = jnp.dot(lru_ref[rows], wl_ref[...], preferred_element_type=F32)
        y = y + jnp.dot(gla_ref[rows], wgl_ref[...], preferred_element_type=F32)
        y = y + jnp.dot(moba_ref[rows], wm_ref[...], preferred_element_type=F32)
        yield
        x = x_ref[rows] + _rms(y, gmix_ref[...])
        h = _rms(x, gpre_ref[...]).astype(BF16)
        yield
        a = jax.nn.silu(jnp.dot(h, wg_ref[...], preferred_element_type=F32))
        yield
        a = (a * jnp.dot(h, wu_ref[...], preferred_element_type=F32)).astype(BF16)
        yield
        f = jnp.dot(a, wd_ref[...], preferred_element_type=F32)
        yield
        o_ref[rows] = x + _rms(f, gpost_ref[...])

    _round_robin([chain(c) for c in range(ROW_CHAINS)])


def _mix_ffn(x, y_lru, y_gla, y_moba, w_out, gmix, gpre, wg, wu, wd, gpost, tt):
    t, d = x.shape
    wl = w_out[:LRU_W]
    wgl = w_out[LRU_W:LRU_W + GLA_V_W]
    wm = w_out[LRU_W + GLA_V_W:]
    row = lambda w: pl.BlockSpec((tt, w), lambda i: (i, 0))
    once = lambda shape: pl.BlockSpec(shape, lambda i: (0, 0), pipeline_mode=pl.Buffered(1))
    return pl.pallas_call(
        _mix_ffn_kernel,
        grid=(t // tt,),
        in_specs=[row(d), row(LRU_W), row(GLA_V_W), row(MOBA_W),
                  once(wl.shape), once(wgl.shape), once(wm.shape), _const_spec((1, d)),
                  _const_spec((1, d)), once(wg.shape), once(wu.shape), once(wd.shape),
                  _const_spec((1, d))],
        out_specs=row(d),
        out_shape=jax.ShapeDtypeStruct((t, d), F32),
        compiler_params=_params("parallel"),
        name="mix_ffn",
    )(x, y_lru, y_gla, y_moba, wl, wgl, wm, gmix, gpre, wg, wu, wd, gpost)


def _block_diag(w):
    g, n, _ = w.shape
    eye = jnp.eye(g, dtype=w.dtype)
    return (eye[:, None, :, None] * w[:, :, None, :]).reshape(g * n, g * n)


def _split_w_in(w_in):
    sizes = (LRU_W, LRU_W, GLA_QK_W, GLA_QK_W, GLA_V_W, GLA_RANK, GLA_V_W, MOBA_W, MOBA_W, MOBA_W)
    offs = np.cumsum((0,) + sizes)
    col = lambda i: w_in[:, offs[i]:offs[i + 1]]
    lru_x, lru_g, gq, gk, gv, g_lr, g_out, mq, mk, mv = (col(i) for i in range(len(sizes)))
    g_lr = jnp.pad(g_lr, ((0, 0), (0, LANES - GLA_RANK)))
    w_nat = jnp.concatenate([lru_x, lru_g, gq, gk, gv, g_out, g_lr, mk], axis=1).astype(BF16)
    w_t = jnp.concatenate([mq, mv], axis=1).T.astype(BF16)
    return w_nat, w_t


def kernel(x, pre_mix_norm, post_mix_norm, pre_ffn_norm, post_ffn_norm, w_in, w_out, lru_conv_w, lru_conv_b, lru_wa, lru_ba, lru_wx, lru_bx, lru_lambda, gla_gate_w2, gla_gate_b, gla_norm, rel_bias, w_ffn_gate, w_ffn_up, w_ffn_down):
    b, s, d = x.shape
    assert s % MOBA_BLOCK == 0 and d == LRU_W + GLA_V_W + MOBA_W
    assert lru_wa.shape[1] == LRU_BLOCKS and (b * s) % TOKEN_TILE == 0
    depth = w_in.shape[0]
    ts_seq = SEQ_TILE
    nbat = next(n for n in (8, 4, 2, 1) if b % n == 0)
    tt = TOKEN_TILE
    row = lambda v: v.reshape(1, -1).astype(F32)

    bias_tab = _bias_tables(rel_bias)
    for l in range(depth):
        w_nat, w_t = _split_w_in(w_in[l])
        w2 = jnp.pad(gla_gate_w2[l], ((0, LANES - GLA_RANK), (0, 0))).astype(BF16)
        mk, mqv, y_lru, y_gla = _front(
            x, row(pre_mix_norm[l]), w_nat, w_t,
            (lru_conv_w[l].astype(F32), row(lru_conv_b[l]),
             _block_diag(lru_wa[l]).astype(BF16), row(lru_ba[l]),
             _block_diag(lru_wx[l]).astype(BF16), row(lru_bx[l]), row(lru_lambda[l])),
            (w2, row(gla_gate_b[l]), row(gla_norm[l])), ts_seq, nbat)
        y_moba = _moba(mqv, mk, bias_tab, rel_bias[REL_BUCKETS - 1].astype(F32))
        x2 = _mix_ffn(x.reshape(b * s, d), y_lru.reshape(b * s, -1), y_gla.reshape(b * s, -1),
                      y_moba.reshape(b * s, -1), w_out[l].astype(BF16), row(post_mix_norm[l]),
                      row(pre_ffn_norm[l]), w_ffn_gate[l].astype(BF16), w_ffn_up[l].astype(BF16),
                      w_ffn_down[l].astype(BF16), row(post_ffn_norm[l]), tt)
        x = x2.reshape(b, s, d)
    return x
```

```python
import functools
import math

import numpy as np
import jax
import jax.numpy as jnp
from jax import lax
from jax.experimental import pallas as pl
from jax.experimental.pallas import tpu as pltpu

F32 = jnp.float32
BF16 = jnp.bfloat16

LRU_W = 256
LRU_BLOCKS = 4
LRU_CONV = 4
LRU_C = 8.0
GLA_HEADS = 4
GLA_DV = 64
GLA_DK = 32
GLA_RANK = 16
GLA_TAU = 16.0
GLA_CHUNK = 64
GLA_QK_W = GLA_HEADS * GLA_DK
GLA_V_W = GLA_HEADS * GLA_DV
MOBA_HEADS = 8
MOBA_DH = 64
MOBA_BLOCK = 256
MOBA_TOPK = 3
MOBA_W = MOBA_HEADS * MOBA_DH
REL_BUCKETS = 32
REL_MAX_DIST = 128
RMS_EPS = 1e-6
NEG_INF = -1e30
LOG2E = math.log2(math.e)

LANES = 128
SUBLANES = 8
VMEM_LIMIT = 56 * 1024 * 1024

GLA_IN_W = GLA_QK_W * 2 + GLA_V_W * 2 + LANES
HEADS_PER_STEP = LANES // MOBA_DH

ROW_CHAINS = 2
SEQ_TILE = MOBA_BLOCK
TOKEN_TILE = 512

NT_DIMS = (((1,), (1,)), ((), ()))
TN_DIMS = (((0,), (0,)), ((), ()))


def _rms(x, g):
    return x * lax.rsqrt(jnp.mean(x * x, axis=-1, keepdims=True) + RMS_EPS) * g


def _split_hi_lo(x):
    hi = x.astype(BF16)
    return hi, (x - hi.astype(F32)).astype(BF16)


def _params(*sem):
    return pltpu.CompilerParams(dimension_semantics=sem, vmem_limit_bytes=VMEM_LIMIT)


def _const_spec(shape):
    return pl.BlockSpec(shape, lambda *_: (0,) * len(shape))


def _round_robin(chains):
    done = object()
    while chains:
        chains = [c for c in chains if next(c, done) is not done]


PROJ_COLS = 512


def _project(bb, x_ref, g_ref, wn_ref, wt_ref, lru_sc, gla_sc, mk_ref, mqv_ref):
    h = _rms(x_ref[bb], g_ref[...]).astype(BF16)
    yield
    natural = [(dst, c, min(PROJ_COLS, dst.shape[-1] - c))
               for dst in (lru_sc, gla_sc, mk_ref) for c in range(0, dst.shape[-1], PROJ_COLS)]
    col = 0
    for dst, c0, w in natural:
        pn = jnp.dot(h, wn_ref[:, col:col + w], preferred_element_type=F32)
        dst[bb, :, c0:c0 + w] = pn.astype(dst.dtype)
        col += w
        yield
    for r0 in range(0, 2 * MOBA_W, PROJ_COLS):
        pt = lax.dot_general(wt_ref[r0:r0 + PROJ_COLS, :], h, NT_DIMS, preferred_element_type=F32)
        if r0 < MOBA_W:
            pt = pt * (MOBA_DH ** -0.5 * LOG2E)
        mqv_ref[bb, r0:r0 + PROJ_COLS, :] = pt.astype(BF16)
        yield


def _shift_rows(x, prev_tail, k, row8):
    sh = pltpu.roll(x, k, axis=0)
    top = jnp.where(row8 < k, pltpu.roll(prev_tail, k, axis=0), sh[:SUBLANES])
    return jnp.concatenate([top, sh[SUBLANES:]], axis=0)


def _lru_tile(in_ref, cw_ref, cb_ref, wa_ref, ba_ref, wx_ref, bx_ref, lam_ref, y_ref,
              h_sc, tail_sc):
    j = pl.program_id(1)
    ts = in_ref.shape[1]
    row = lax.broadcasted_iota(jnp.int32, (ts, LRU_W), 0)
    row8 = lax.broadcasted_iota(jnp.int32, (SUBLANES, LRU_W), 0)
    def sequence(bb):
        xb = in_ref[bb, :, :LRU_W]
        gb = in_ref[bb, :, LRU_W:]
        tail = tail_sc[bb]
        xc = xb * cw_ref[LRU_CONV - 1:LRU_CONV, :] + cb_ref[...]
        for k in range(1, LRU_CONV):
            xc = xc + _shift_rows(xb, tail, k, row8) * cw_ref[LRU_CONV - 1 - k:LRU_CONV - k, :]
        tail_sc[bb] = xb[ts - SUBLANES:]
        yield

        xcb = xc.astype(BF16)
        r = jax.nn.sigmoid(jnp.dot(xcb, wa_ref[...], preferred_element_type=F32) + ba_ref[...])
        yield
        i = jax.nn.sigmoid(jnp.dot(xcb, wx_ref[...], preferred_element_type=F32) + bx_ref[...])
        yield
        log_a = -LRU_C * r * jax.nn.softplus(-lam_ref[...])
        a = jnp.exp(log_a)
        yield
        mult = jnp.sqrt(-jnp.tanh(log_a) * (a * a + 1.0))
        mult = jnp.where(row + j * ts == 0, 1.0, mult)
        u = mult * (i * xc)
        yield

        d = 1
        while d < ts:
            keep = row >= d
            a_sh = jnp.where(keep, pltpu.roll(a, d, axis=0), 1.0)
            u_sh = jnp.where(keep, pltpu.roll(u, d, axis=0), 0.0)
            u = a * u_sh + u
            a = a * a_sh
            d *= 2
            yield
        h = u + a * h_sc[bb]
        h_sc[bb] = h[ts - 1:ts]
        y_ref[bb] = (h * jax.nn.gelu(gb)).astype(y_ref.dtype)

    return [sequence(bb) for bb in range(in_ref.shape[0])]


def _gla_tile(in_ref, w2_ref, gb_ref, gain_ref, y_ref, st_sc):
    ts = in_ref.shape[1]
    c = GLA_CHUNK
    ri = lax.broadcasted_iota(jnp.int32, (ts, ts), 0)
    ci = lax.broadcasted_iota(jnp.int32, (ts, ts), 1)
    in_chunk = (ri // c == ci // c) & (ci <= ri)
    tri = jnp.where(in_chunk, 1.0, 0.0).astype(BF16)
    qk_lane = lax.broadcasted_iota(jnp.int32, (1, GLA_QK_W), 1) // GLA_DK
    v_lane = lax.broadcasted_iota(jnp.int32, (1, GLA_V_W), 1) // GLA_DV
    st_row = lax.broadcasted_iota(jnp.int32, (GLA_V_W, GLA_QK_W), 0) // GLA_DV
    st_col = lax.broadcasted_iota(jnp.int32, (GLA_V_W, GLA_QK_W), 1) // GLA_DK
    same_head = st_row == st_col
    hr = lax.broadcasted_iota(jnp.int32, (GLA_V_W, GLA_V_W), 0) // GLA_DV
    hc = lax.broadcasted_iota(jnp.int32, (GLA_V_W, GLA_V_W), 1) // GLA_DV
    head_mean = jnp.where(hr == hc, 1.0 / GLA_DV, 0.0).astype(BF16)
    chunks = range(ts // c)
    rows = [slice(n * c, (n + 1) * c) for n in chunks]

    def sequence(bb):
        q = in_ref[bb, :, :GLA_QK_W]
        k = in_ref[bb, :, GLA_QK_W:2 * GLA_QK_W]
        v = in_ref[bb, :, 2 * GLA_QK_W:2 * GLA_QK_W + GLA_V_W]
        g_out = in_ref[bb, :, 2 * GLA_QK_W + GLA_V_W:2 * GLA_QK_W + 2 * GLA_V_W]
        g_lr = in_ref[bb, :, 2 * GLA_QK_W + 2 * GLA_V_W:]

        gate = jnp.dot(g_lr.astype(BF16), w2_ref[...], preferred_element_type=F32) + gb_ref[...]
        log_alpha = jax.nn.log_sigmoid(gate) / GLA_TAU
        yield
        la_hi, la_lo = _split_hi_lo(log_alpha)
        cum = (jnp.dot(tri, la_hi, preferred_element_type=F32)
               + jnp.dot(tri, la_lo, preferred_element_type=F32))
        lasts = [cum[(n + 1) * c - 1:(n + 1) * c] for n in chunks]
        last_full = jnp.concatenate([jnp.broadcast_to(l, (c, GLA_QK_W)) for l in lasts], axis=0)
        yield

        q_d = (q * (GLA_DK ** -0.5) * jnp.exp(cum)).astype(BF16)
        k_d = (k * jnp.exp(-cum)).astype(BF16)
        yield
        k_l = (k * jnp.exp(last_full - cum)).astype(BF16)
        v_b = v.astype(BF16)
        yield

        o = jnp.zeros((ts, GLA_V_W), F32)
        for hh in range(GLA_HEADS):
            q_h = jnp.where(qk_lane == hh, q_d, jnp.zeros_like(q_d))
            attn = lax.dot_general(q_h, k_d, NT_DIMS, preferred_element_type=F32)
            attn = jnp.where(in_chunk, attn, 0.0).astype(BF16)
            v_h = jnp.where(v_lane == hh, v_b, jnp.zeros_like(v_b))
            o = o + jnp.dot(attn, v_h, preferred_element_type=F32)
            yield

        st = st_sc[bb]
        inter = []
        for n in chunks:
            inter.append(lax.dot_general(q_d[rows[n]], st.astype(BF16), NT_DIMS,
                                         preferred_element_type=F32))
            kv = lax.dot_general(v_b[rows[n]], k_l[rows[n]], TN_DIMS,
                                 preferred_element_type=F32)
            st = st * jnp.exp(lasts[n]) + jnp.where(same_head, kv, 0.0)
            yield
        st_sc[bb] = st
        o = o + jnp.concatenate(inter, axis=0)

        sq_hi, sq_lo = _split_hi_lo(o * o)
        ms = (jnp.dot(sq_hi, head_mean, preferred_element_type=F32)
              + jnp.dot(sq_lo, head_mean, preferred_element_type=F32))
        y = o * lax.rsqrt(ms + RMS_EPS) * gain_ref[...] * jax.nn.silu(g_out)
        y_ref[bb] = y.astype(y_ref.dtype)

    return [sequence(bb) for bb in range(in_ref.shape[0])]


SEQ_STAGGER = 2
N_PROJ_ARGS = 4
N_LRU_ARGS = 7
N_GLA_ARGS = 3


def _alternate(*gens):
    gens, done = list(gens), object()
    while gens:
        gens = [g for g in gens if next(g, done) is not done]
        yield


def _delayed(gen, steps):
    for _ in range(steps):
        yield
    yield from gen


def _front_kernel(*refs):
    proj_in = refs[:N_PROJ_ARGS]
    lru_in = refs[N_PROJ_ARGS:N_PROJ_ARGS + N_LRU_ARGS]
    gla_in = refs[N_PROJ_ARGS + N_LRU_ARGS:N_PROJ_ARGS + N_LRU_ARGS + N_GLA_ARGS]
    (mk_ref, mqv_ref, y_lru_ref, y_gla_ref,
     lru_sc, gla_sc, h_sc, tail_sc, st_sc) = refs[N_PROJ_ARGS + N_LRU_ARGS + N_GLA_ARGS:]

    @pl.when(pl.program_id(1) == 0)
    def _():
        h_sc[...] = jnp.zeros_like(h_sc)
        tail_sc[...] = jnp.zeros_like(tail_sc)
        st_sc[...] = jnp.zeros_like(st_sc)

    lru = _lru_tile(lru_sc, *lru_in, y_lru_ref, h_sc, tail_sc)
    gla = _gla_tile(gla_sc, *gla_in, y_gla_ref, st_sc)
    proj = [_project(bb, *proj_in, lru_sc, gla_sc, mk_ref, mqv_ref) for bb in range(len(lru))]

    def sequence(bb):
        yield from proj[bb]
        yield from _alternate(lru[bb], gla[bb])

    _round_robin([_delayed(sequence(bb), bb * SEQ_STAGGER) for bb in range(len(lru))])


def _front(x, g, wn, wt, lru_args, gla_args, ts, nbat):
    assert len(lru_args) == N_LRU_ARGS and len(gla_args) == N_GLA_ARGS
    b, s, d = x.shape
    tile = lambda w: pl.BlockSpec((nbat, ts, w), lambda i, j: (i, j, 0))
    consts = (g, wn, wt) + tuple(lru_args) + tuple(gla_args)
    return pl.pallas_call(
        _front_kernel,
        grid=(b // nbat, s // ts),
        in_specs=[tile(d)] + [_const_spec(a.shape) for a in consts],
        out_specs=[tile(MOBA_W),
                   pl.BlockSpec((nbat, 2 * MOBA_W, ts), lambda i, j: (i, 0, j)),
                   tile(LRU_W), tile(GLA_V_W)],
        out_shape=[jax.ShapeDtypeStruct((b, s, MOBA_W), BF16),
                   jax.ShapeDtypeStruct((b, 2 * MOBA_W, s), BF16),
                   jax.ShapeDtypeStruct((b, s, LRU_W), BF16),
                   jax.ShapeDtypeStruct((b, s, GLA_V_W), BF16)],
        scratch_shapes=[pltpu.VMEM((nbat, ts, 2 * LRU_W), F32),
                        pltpu.VMEM((nbat, ts, GLA_IN_W), F32),
                        pltpu.VMEM((nbat, 1, LRU_W), F32),
                        pltpu.VMEM((nbat, SUBLANES, LRU_W), F32),
                        pltpu.VMEM((nbat, GLA_V_W, GLA_QK_W), F32)],
        compiler_params=_params("parallel", "arbitrary"),
        name="front",
    )(x, *consts)


def _t5_bucket_np(rel):
    n = np.maximum(rel, 0)
    max_exact = REL_BUCKETS // 2
    nf = np.maximum(n, 1).astype(np.float32)
    large = max_exact + (np.log(nf / np.float32(max_exact)) / np.float32(math.log(REL_MAX_DIST / max_exact))
                         * np.float32(REL_BUCKETS - max_exact)).astype(np.int32)
    large = np.minimum(large, REL_BUCKETS - 1)
    return np.where(n < max_exact, n, large).astype(np.int32)


FAR_DIST = int(np.argmax(_t5_bucket_np(np.arange(4 * REL_MAX_DIST)) == REL_BUCKETS - 1))
assert np.all(_t5_bucket_np(np.arange(FAR_DIST, 1 << 16)) == REL_BUCKETS - 1)
assert FAR_DIST <= MOBA_BLOCK

N_TABLES = 2
V_AUG = MOBA_DH + 16
GATE_COLS = 1024


def _bias_kernel(bkt_ref, rb_ref, out_ref):
    h = pl.program_id(0)
    for t in range(2):
        bkt = bkt_ref[t]
        acc = jnp.zeros(bkt.shape, F32)
        for bb in range(REL_BUCKETS):
            acc = jnp.where(bkt == bb, rb_ref[bb, h], acc)
        out_ref[0, t] = jnp.where(bkt < 0, NEG_INF, acc * LOG2E)


def _bias_tables(rel_bias):
    ko = np.arange(MOBA_BLOCK)[:, None]
    qo = np.arange(MOBA_BLOCK)[None, :]
    own = np.where(qo >= ko, _t5_bucket_np(qo - ko), -1)
    bkt = np.stack([own, _t5_bucket_np(qo - ko + MOBA_BLOCK)]).astype(np.int32)
    return pl.pallas_call(
        _bias_kernel,
        grid=(MOBA_HEADS,),
        in_specs=[
            _const_spec((2, MOBA_BLOCK, MOBA_BLOCK)),
            pl.BlockSpec(memory_space=pltpu.SMEM),
        ],
        out_specs=pl.BlockSpec((1, N_TABLES, MOBA_BLOCK, MOBA_BLOCK), lambda h: (h, 0, 0, 0)),
        out_shape=jax.ShapeDtypeStruct((MOBA_HEADS, N_TABLES, MOBA_BLOCK, MOBA_BLOCK), F32),
        compiler_params=_params("parallel"),
        name="moba_bias",
    )(jnp.asarray(bkt), rel_bias.astype(F32))


NEAR_PIPE = 12
FAR_PIPE = 28
LAG = 4


def _pipe_trips(n_tiles, pipe):
    return -(-(n_tiles + LAG - pipe) // pipe)


def _moba_schedules(nb):
    near = [(n, n, n, 0) for n in range(nb)] + [(n, n - 1, n - 1, 1) for n in range(1, nb)]
    far = [(n, j, j, 0) for j in range(nb - 2) for n in range(j + 2, nb)]

    def table(tiles, pipe):
        length = pipe + pipe * _pipe_trips(len(tiles), pipe)
        rows = tiles + [(0, 0, nb, 0)] * (length - len(tiles))
        return np.asarray(rows, np.int32).T.copy(), len(tiles)

    return table(near, NEAR_PIPE), table(far, FAR_PIPE)


def _moba_kernel(near_ref, far_ref, farb_ref, q_ref, v_ref, k_ref, bias_ref, y_ref,
                 vaug_sc, mb_sc, qz_sc, s_sc, cmax_sc, m_sc, acc_sc, *, n_near, n_far):
    hp = pl.program_id(0)
    blk = MOBA_BLOCK
    seq = k_ref.shape[1]
    nb = seq // blk
    heads = range(HEADS_PER_STEP)

    ones_row = lax.broadcasted_iota(jnp.int32, (V_AUG - MOBA_DH, seq), 0) == 0
    kmean = jnp.concatenate(
        [jnp.mean(k_ref[0, jb * blk:(jb + 1) * blk, :].astype(F32), axis=0, keepdims=True)
         for jb in range(nb)], axis=0)
    lane_head = lax.broadcasted_iota(jnp.int32, (nb, LANES), 1) // MOBA_DH
    for hh in heads:
        vaug_sc[hh, :MOBA_DH, :] = v_ref[0, hh * MOBA_DH:(hh + 1) * MOBA_DH, :]
        vaug_sc[hh, MOBA_DH:, :] = jnp.where(ones_row, 1.0, 0.0).astype(BF16)
        mb_sc[hh, nb:, :] = jnp.full((mb_sc.shape[1] - nb, seq), NEG_INF, F32)
        m_sc[hh] = jnp.full(m_sc.shape[1:], NEG_INF, F32)
        acc_sc[hh] = jnp.zeros(acc_sc.shape[1:], F32)
        km_hi, km_lo = _split_hi_lo(jnp.where(lane_head == hh, kmean, 0.0))
        far_bias = farb_ref[hp * HEADS_PER_STEP + hh] * LOG2E
        for c0 in range(0, seq, GATE_COLS):
            cw = min(GATE_COLS, seq - c0)
            qc = q_ref[0, :, c0:c0 + cw]
            g = (jnp.dot(km_hi, qc, preferred_element_type=F32)
                 + jnp.dot(km_lo, qc, preferred_element_type=F32))
            key_blk = lax.broadcasted_iota(jnp.int32, (nb, cw), 0)
            qry_blk = (lax.broadcasted_iota(jnp.int32, (nb, cw), 1) + c0) // blk
            g = jnp.where(key_blk < qry_blk, g, -jnp.inf)
            key_f = key_blk.astype(F32)
            picked = jnp.zeros((nb, cw), F32)
            for _ in range(MOBA_TOPK):
                best = jnp.max(g, axis=0, keepdims=True)
                first = jnp.min(jnp.where(g == best, key_f, float(nb)), axis=0, keepdims=True)
                found = jnp.where(best > -jnp.inf, 1.0, 0.0)
                picked = picked + jnp.where(key_f == first, found, 0.0)
                g = jnp.where(key_f == first, -jnp.inf, g)
            mb_sc[hh, :nb, c0:c0 + cw] = jnp.where(
                picked > 0.0, jnp.where(key_blk < qry_blk - 1, far_bias, 0.0),
                jnp.where(key_blk == qry_blk, 0.0, NEG_INF))

    feat = lax.broadcasted_iota(jnp.int32, (LANES, blk), 0) // MOBA_DH
    for n in range(nb):
        qb = q_ref[0, :, n * blk:(n + 1) * blk]
        for hh in heads:
            c0 = (n * HEADS_PER_STEP + hh) * blk
            qz_sc[:, c0:c0 + blk] = jnp.where(feat == hh, qb, jnp.zeros_like(qb))

    def run_pass(sched_ref, n_tiles, with_table, pipe):
        def col(idx, r):
            return pl.multiple_of(sched_ref[r, idx] * blk, blk)

        def logits(idx, slot):
            qcols = HEADS_PER_STEP * blk
            qz = qz_sc[:, pl.ds(pl.multiple_of(sched_ref[0, idx] * qcols, qcols), qcols)]
            s = jnp.dot(k_ref[0, pl.ds(col(idx, 1), blk), :], qz, preferred_element_type=F32)
            if with_table:
                tab = sched_ref[3, idx]
                s = s + jnp.concatenate([bias_ref[hh, tab] for hh in heads], axis=1)
            s_sc[slot] = s
            cmax_sc[slot] = jnp.max(s.reshape(blk // SUBLANES, SUBLANES, s.shape[1]), axis=0)

        def softmax_values(idx, slot):
            n = sched_ref[0, idx]
            qcol = col(idx, 0)
            kcol = col(idx, 1)
            row = sched_ref[2, idx]
            for hh in heads:
                cols = slice(hh * blk, (hh + 1) * blk)
                cm = jnp.max(cmax_sc[slot, :, cols], axis=0, keepdims=True)
                mb = mb_sc[hh, pl.ds(row, 1), pl.ds(qcol, blk)]
                m_old = m_sc[hh, n]
                m_new = jnp.maximum(m_old, cm + mb)
                m_sc[hh, n] = m_new
                p = jnp.exp2(s_sc[slot, :, cols] + (mb - m_new)).astype(BF16)
                pv = jnp.dot(vaug_sc[hh, :, pl.ds(kcol, blk)], p, preferred_element_type=F32)
                acc_sc[hh, n] = jnp.exp2(m_old - m_new) * acc_sc[hh, n] + pv

        for i in range(pipe):
            logits(i, i)
            if i >= LAG:
                softmax_values(i - LAG, (i - LAG) % pipe)

        def body(d, carry):
            for u in range(pipe):
                i = pipe + pipe * d + u
                logits(i, u)
                softmax_values(i - LAG, (u - LAG) % pipe)
            return carry

        lax.fori_loop(0, _pipe_trips(n_tiles, pipe), body, 0)

    run_pass(near_ref, n_near, True, NEAR_PIPE)
    run_pass(far_ref, n_far, False, FAR_PIPE)

    for n in range(nb):
        out_t = jnp.concatenate(
            [acc_sc[hh, n, :MOBA_DH, :] * (1.0 / acc_sc[hh, n, MOBA_DH:MOBA_DH + 1, :])
             for hh in heads], axis=0)
        y_ref[0, n * blk:(n + 1) * blk, :] = out_t.T.astype(y_ref.dtype)


def _moba(mqv, mk, bias_tab, far_bias):
    b, s, _ = mk.shape
    blk = MOBA_BLOCK
    nb = s // blk
    npair = MOBA_HEADS // HEADS_PER_STEP
    (near, n_near), (far, n_far) = _moba_schedules(nb)
    smem = pl.BlockSpec(memory_space=pltpu.SMEM)
    slots = max(NEAR_PIPE, FAR_PIPE)
    return pl.pallas_call(
        functools.partial(_moba_kernel, n_near=n_near, n_far=n_far),
        grid=(npair, b),
        in_specs=[
            smem, smem, smem,
            pl.BlockSpec((1, LANES, s), lambda p, i: (i, p, 0)),
            pl.BlockSpec((1, LANES, s), lambda p, i: (i, npair + p, 0)),
            pl.BlockSpec((1, s, LANES), lambda p, i: (i, 0, p)),
            pl.BlockSpec((HEADS_PER_STEP, N_TABLES, blk, blk), lambda p, i: (p, 0, 0, 0)),
        ],
        out_specs=pl.BlockSpec((1, s, LANES), lambda p, i: (i, 0, p)),
        out_shape=jax.ShapeDtypeStruct((b, s, MOBA_W), BF16),
        scratch_shapes=[
            pltpu.VMEM((HEADS_PER_STEP, V_AUG, s), BF16),
            pltpu.VMEM((HEADS_PER_STEP, nb + SUBLANES, s), F32),
            pltpu.VMEM((LANES, HEADS_PER_STEP * s), BF16),
            pltpu.VMEM((slots, blk, HEADS_PER_STEP * blk), F32),
            pltpu.VMEM((slots, SUBLANES, HEADS_PER_STEP * blk), F32),
            pltpu.VMEM((HEADS_PER_STEP, nb, 1, blk), F32),
            pltpu.VMEM((HEADS_PER_STEP, nb, V_AUG, blk), F32),
        ],
        compiler_params=_params("parallel", "parallel"),
        name="moba",
    )(jnp.asarray(near), jnp.asarray(far), far_bias, mqv, mqv, mk, bias_tab)


def _mix_ffn_kernel(x_ref, lru_ref, gla_ref, moba_ref, wl_ref, wgl_ref, wm_ref, gmix_ref,
                    gpre_ref, wg_ref, wu_ref, wd_ref, gpost_ref, o_ref):
    rows_per = x_ref.shape[0] // ROW_CHAINS

    def chain(c):
        rows = slice(c * rows_per, (c + 1) * rows_per)
        y = jnp.dot(lru_ref[rows], wl_ref[...], preferred_element_type=F32)
        y = y + jnp.dot(gla_ref[rows], wgl_ref[...], preferred_element_type=F32)
        y = y + jnp.dot(moba_ref[rows], wm_ref[...], preferred_element_type=F32)
        yield
        x = x_ref[rows] + _rms(y, gmix_ref[...])
        h = _rms(x, gpre_ref[...]).astype(BF16)
        yield
        a = jax.nn.silu(jnp.dot(h, wg_ref[...], preferred_element_type=F32))
        yield
        a = (a * jnp.dot(h, wu_ref[...], preferred_element_type=F32)).astype(BF16)
        yield
        f = jnp.dot(a, wd_ref[...], preferred_element_type=F32)
        yield
        o_ref[rows] = x + _rms(f, gpost_ref[...])

    _round_robin([chain(c) for c in range(ROW_CHAINS)])


def _mix_ffn(x, y_lru, y_gla, y_moba, w_out, gmix, gpre, wg, wu, wd, gpost, tt):
    t, d = x.shape
    wl = w_out[:LRU_W]
    wgl = w_out[LRU_W:LRU_W + GLA_V_W]
    wm = w_out[LRU_W + GLA_V_W:]
    row = lambda w: pl.BlockSpec((tt, w), lambda i: (i, 0))
    once = lambda shape: pl.BlockSpec(shape, lambda i: (0, 0), pipeline_mode=pl.Buffered(1))
    return pl.pallas_call(
        _mix_ffn_kernel,
        grid=(t // tt,),
        in_specs=[row(d), row(LRU_W), row(GLA_V_W), row(MOBA_W),
                  once(wl.shape), once(wgl.shape), once(wm.shape), _const_spec((1, d)),
                  _const_spec((1, d)), once(wg.shape), once(wu.shape), once(wd.shape),
                  _const_spec((1, d))],
        out_specs=row(d),
        out_shape=jax.ShapeDtypeStruct((t, d), F32),
        compiler_params=_params("parallel"),
        name="mix_ffn",
    )(x, y_lru, y_gla, y_moba, wl, wgl, wm, gmix, gpre, wg, wu, wd, gpost)


def _block_diag(w):
    g, n, _ = w.shape
    eye = jnp.eye(g, dtype=w.dtype)
    return (eye[:, None, :, None] * w[:, :, None, :]).reshape(g * n, g * n)


def _split_w_in(w_in):
    sizes = (LRU_W, LRU_W, GLA_QK_W, GLA_QK_W, GLA_V_W, GLA_RANK, GLA_V_W, MOBA_W, MOBA_W, MOBA_W)
    offs = np.cumsum((0,) + sizes)
    col = lambda i: w_in[:, offs[i]:offs[i + 1]]
    lru_x, lru_g, gq, gk, gv, g_lr, g_out, mq, mk, mv = (col(i) for i in range(len(sizes)))
    g_lr = jnp.pad(g_lr, ((0, 0), (0, LANES - GLA_RANK)))
    w_nat = jnp.concatenate([lru_x, lru_g, gq, gk, gv, g_out, g_lr, mk], axis=1).astype(BF16)
    w_t = jnp.concatenate([mq, mv], axis=1).T.astype(BF16)
    return w_nat, w_t


def kernel(x, pre_mix_norm, post_mix_norm, pre_ffn_norm, post_ffn_norm, w_in, w_out, lru_conv_w, lru_conv_b, lru_wa, lru_ba, lru_wx, lru_bx, lru_lambda, gla_gate_w2, gla_gate_b, gla_norm, rel_bias, w_ffn_gate, w_ffn_up, w_ffn_down):
    b, s, d = x.shape
    assert s % MOBA_BLOCK == 0 and d == LRU_W + GLA_V_W + MOBA_W
    assert lru_wa.shape[1] == LRU_BLOCKS and (b * s) % TOKEN_TILE == 0
    depth = w_in.shape[0]
    ts_seq = SEQ_TILE
    nbat = next(n for n in (8, 4, 2, 1) if b % n == 0)
    tt = TOKEN_TILE
    row = lambda v: v.reshape(1, -1).astype(F32)

    bias_tab = _bias_tables(rel_bias)
    for l in range(depth):
        w_nat, w_t = _split_w_in(w_in[l])
        w2 = jnp.pad(gla_gate_w2[l], ((0, LANES - GLA_RANK), (0, 0))).astype(BF16)
        mk, mqv, y_lru, y_gla = _front(
            x, row(pre_mix_norm[l]), w_nat, w_t,
            (lru_conv_w[l].astype(F32), row(lru_conv_b[l]),
             _block_diag(lru_wa[l]).astype(BF16), row(lru_ba[l]),
             _block_diag(lru_wx[l]).astype(BF16), row(lru_bx[l]), row(lru_lambda[l])),
            (w2, row(gla_gate_b[l]), row(gla_norm[l])), ts_seq, nbat)
        y_moba = _moba(mqv, mk, bias_tab, rel_bias[REL_BUCKETS - 1].astype(F32))
        x2 = _mix_ffn(x.reshape(b * s, d), y_lru.reshape(b * s, -1), y_gla.reshape(b * s, -1),
                      y_moba.reshape(b * s, -1), w_out[l].astype(BF16), row(post_mix_norm[l]),
                      row(pre_ffn_norm[l]), w_ffn_gate[l].astype(BF16), w_ffn_up[l].astype(BF16),
                      w_ffn_down[l].astype(BF16), row(post_ffn_norm[l]), tt)
        x = x2.reshape(b, s, d)
    return x
```

```python
import functools
import math

import numpy as np
import jax
import jax.numpy as jnp
from jax import lax
from jax.experimental import pallas as pl
from jax.experimental.pallas import tpu as pltpu

F32 = jnp.float32
BF16 = jnp.bfloat16

LRU_W = 256
LRU_BLOCKS = 4
LRU_CONV = 4
LRU_C = 8.0
GLA_HEADS = 4
GLA_DV = 64
GLA_DK = 32
GLA_RANK = 16
GLA_TAU = 16.0
GLA_CHUNK = 64
GLA_QK_W = GLA_HEADS * GLA_DK
GLA_V_W = GLA_HEADS * GLA_DV
MOBA_HEADS = 8
MOBA_DH = 64
MOBA_BLOCK = 256
MOBA_TOPK = 3
MOBA_W = MOBA_HEADS * MOBA_DH
REL_BUCKETS = 32
REL_MAX_DIST = 128
RMS_EPS = 1e-6
NEG_INF = -1e30
LOG2E = math.log2(math.e)

LANES = 128
SUBLANES = 8
VMEM_LIMIT = 56 * 1024 * 1024

GLA_IN_W = GLA_QK_W * 2 + GLA_V_W * 2 + LANES
HEADS_PER_STEP = LANES // MOBA_DH

ROW_CHAINS = 2
SEQ_TILE = MOBA_BLOCK
TOKEN_TILE = 512

NT_DIMS = (((1,), (1,)), ((), ()))
TN_DIMS = (((0,), (0,)), ((), ()))


def _rms(x, g):
    return x * lax.rsqrt(jnp.mean(x * x, axis=-1, keepdims=True) + RMS_EPS) * g


def _split_hi_lo(x):
    hi = x.astype(BF16)
    return hi, (x - hi.astype(F32)).astype(BF16)


def _params(*sem):
    return pltpu.CompilerParams(dimension_semantics=sem, vmem_limit_bytes=VMEM_LIMIT)


def _const_spec(shape):
    return pl.BlockSpec(shape, lambda *_: (0,) * len(shape))


def _round_robin(chains):
    done = object()
    while chains:
        chains = [c for c in chains if next(c, done) is not done]


PROJ_COLS = 512


def _project(bb, x_ref, g_ref, wn_ref, wt_ref, lru_sc, gla_sc, mk_ref, mqv_ref):
    h = _rms(x_ref[bb], g_ref[...]).astype(BF16)
    yield
    natural = [(dst, c, min(PROJ_COLS, dst.shape[-1] - c))
               for dst in (lru_sc, gla_sc, mk_ref) for c in range(0, dst.shape[-1], PROJ_COLS)]
    col = 0
    for dst, c0, w in natural:
        pn = jnp.dot(h, wn_ref[:, col:col + w], preferred_element_type=F32)
        dst[bb, :, c0:c0 + w] = pn.astype(dst.dtype)
        col += w
        yield
    for r0 in range(0, 2 * MOBA_W, PROJ_COLS):
        pt = lax.dot_general(wt_ref[r0:r0 + PROJ_COLS, :], h, NT_DIMS, preferred_element_type=F32)
        if r0 < MOBA_W:
            pt = pt * (MOBA_DH ** -0.5 * LOG2E)
        mqv_ref[bb, r0:r0 + PROJ_COLS, :] = pt.astype(BF16)
        yield


def _shift_rows(x, prev_tail, k, row8):
    sh = pltpu.roll(x, k, axis=0)
    top = jnp.where(row8 < k, pltpu.roll(prev_tail, k, axis=0), sh[:SUBLANES])
    return jnp.concatenate([top, sh[SUBLANES:]], axis=0)


def _lru_tile(in_ref, cw_ref, cb_ref, wa_ref, ba_ref, wx_ref, bx_ref, lam_ref, y_ref,
              h_sc, tail_sc):
    j = pl.program_id(1)
    ts = in_ref.shape[1]
    row = lax.broadcasted_iota(jnp.int32, (ts, LRU_W), 0)
    row8 = lax.broadcasted_iota(jnp.int32, (SUBLANES, LRU_W), 0)
    def sequence(bb):
        xb = in_ref[bb, :, :LRU_W]
        gb = in_ref[bb, :, LRU_W:]
        tail = tail_sc[bb]
        xc = xb * cw_ref[LRU_CONV - 1:LRU_CONV, :] + cb_ref[...]
        for k in range(1, LRU_CONV):
            xc = xc + _shift_rows(xb, tail, k, row8) * cw_ref[LRU_CONV - 1 - k:LRU_CONV - k, :]
        tail_sc[bb] = xb[ts - SUBLANES:]
        yield

        xcb = xc.astype(BF16)
        r = jax.nn.sigmoid(jnp.dot(xcb, wa_ref[...], preferred_element_type=F32) + ba_ref[...])
        yield
        i = jax.nn.sigmoid(jnp.dot(xcb, wx_ref[...], preferred_element_type=F32) + bx_ref[...])
        yield
        log_a = -LRU_C * r * jax.nn.softplus(-lam_ref[...])
        a = jnp.exp(log_a)
        yield
        mult = jnp.sqrt(-jnp.tanh(log_a) * (a * a + 1.0))
        mult = jnp.where(row + j * ts == 0, 1.0, mult)
        u = mult * (i * xc)
        yield

        d = 1
        while d < ts:
            keep = row >= d
            a_sh = jnp.where(keep, pltpu.roll(a, d, axis=0), 1.0)
            u_sh = jnp.where(keep, pltpu.roll(u, d, axis=0), 0.0)
            u = a * u_sh + u
            a = a * a_sh
            d *= 2
            yield
        h = u + a * h_sc[bb]
        h_sc[bb] = h[ts - 1:ts]
        y_ref[bb] = (h * jax.nn.gelu(gb)).astype(y_ref.dtype)

    return [sequence(bb) for bb in range(in_ref.shape[0])]


def _gla_tile(in_ref, w2_ref, gb_ref, gain_ref, y_ref, st_sc):
    ts = in_ref.shape[1]
    c = GLA_CHUNK
    ri = lax.broadcasted_iota(jnp.int32, (ts, ts), 0)
    ci = lax.broadcasted_iota(jnp.int32, (ts, ts), 1)
    in_chunk = (ri // c == ci // c) & (ci <= ri)
    tri = jnp.where(in_chunk, 1.0, 0.0).astype(BF16)
    qk_lane = lax.broadcasted_iota(jnp.int32, (1, GLA_QK_W), 1) // GLA_DK
    v_lane = lax.broadcasted_iota(jnp.int32, (1, GLA_V_W), 1) // GLA_DV
    st_row = lax.broadcasted_iota(jnp.int32, (GLA_V_W, GLA_QK_W), 0) // GLA_DV
    st_col = lax.broadcasted_iota(jnp.int32, (GLA_V_W, GLA_QK_W), 1) // GLA_DK
    same_head = st_row == st_col
    hr = lax.broadcasted_iota(jnp.int32, (GLA_V_W, GLA_V_W), 0) // GLA_DV
    hc = lax.broadcasted_iota(jnp.int32, (GLA_V_W, GLA_V_W), 1) // GLA_DV
    head_mean = jnp.where(hr == hc, 1.0 / GLA_DV, 0.0).astype(BF16)
    chunks = range(ts // c)
    rows = [slice(n * c, (n + 1) * c) for n in chunks]

    def sequence(bb):
        q = in_ref[bb, :, :GLA_QK_W]
        k = in_ref[bb, :, GLA_QK_W:2 * GLA_QK_W]
        v = in_ref[bb, :, 2 * GLA_QK_W:2 * GLA_QK_W + GLA_V_W]
        g_out = in_ref[bb, :, 2 * GLA_QK_W + GLA_V_W:2 * GLA_QK_W + 2 * GLA_V_W]
        g_lr = in_ref[bb, :, 2 * GLA_QK_W + 2 * GLA_V_W:]

        gate = jnp.dot(g_lr.astype(BF16), w2_ref[...], preferred_element_type=F32) + gb_ref[...]
        log_alpha = jax.nn.log_sigmoid(gate) / GLA_TAU
        yield
        la_hi, la_lo = _split_hi_lo(log_alpha)
        cum = (jnp.dot(tri, la_hi, preferred_element_type=F32)
               + jnp.dot(tri, la_lo, preferred_element_type=F32))
        lasts = [cum[(n + 1) * c - 1:(n + 1) * c] for n in chunks]
        last_full = jnp.concatenate([jnp.broadcast_to(l, (c, GLA_QK_W)) for l in lasts], axis=0)
        yield

        q_d = (q * (GLA_DK ** -0.5) * jnp.exp(cum)).astype(BF16)
        k_d = (k * jnp.exp(-cum)).astype(BF16)
        yield
        k_l = (k * jnp.exp(last_full - cum)).astype(BF16)
        v_b = v.astype(BF16)
        yield

        o = jnp.zeros((ts, GLA_V_W), F32)
        for hh in range(GLA_HEADS):
            q_h = jnp.where(qk_lane == hh, q_d, jnp.zeros_like(q_d))
            attn = lax.dot_general(q_h, k_d, NT_DIMS, preferred_element_type=F32)
            attn = jnp.where(in_chunk, attn, 0.0).astype(BF16)
            v_h = jnp.where(v_lane == hh, v_b, jnp.zeros_like(v_b))
            o = o + jnp.dot(attn, v_h, preferred_element_type=F32)
            yield

        st = st_sc[bb]
        inter = []
        for n in chunks:
            inter.append(lax.dot_general(q_d[rows[n]], st.astype(BF16), NT_DIMS,
                                         preferred_element_type=F32))
            kv = lax.dot_general(v_b[rows[n]], k_l[rows[n]], TN_DIMS,
                                 preferred_element_type=F32)
            st = st * jnp.exp(lasts[n]) + jnp.where(same_head, kv, 0.0)
            yield
        st_sc[bb] = st
        o = o + jnp.concatenate(inter, axis=0)

        sq_hi, sq_lo = _split_hi_lo(o * o)
        ms = (jnp.dot(sq_hi, head_mean, preferred_element_type=F32)
              + jnp.dot(sq_lo, head_mean, preferred_element_type=F32))
        y = o * lax.rsqrt(ms + RMS_EPS) * gain_ref[...] * jax.nn.silu(g_out)
        y_ref[bb] = y.astype(y_ref.dtype)

    return [sequence(bb) for bb in range(in_ref.shape[0])]


SEQ_STAGGER = 2
N_PROJ_ARGS = 4
N_LRU_ARGS = 7
N_GLA_ARGS = 3


def _alternate(*gens):
    gens, done = list(gens), object()
    while gens:
        gens = [g for g in gens if next(g, done) is not done]
        yield


def _delayed(gen, steps):
    for _ in range(steps):
        yield
    yield from gen


def _front_kernel(*refs):
    proj_in = refs[:N_PROJ_ARGS]
    lru_in = refs[N_PROJ_ARGS:N_PROJ_ARGS + N_LRU_ARGS]
    gla_in = refs[N_PROJ_ARGS + N_LRU_ARGS:N_PROJ_ARGS + N_LRU_ARGS + N_GLA_ARGS]
    (mk_ref, mqv_ref, y_lru_ref, y_gla_ref,
     lru_sc, gla_sc, h_sc, tail_sc, st_sc) = refs[N_PROJ_ARGS + N_LRU_ARGS + N_GLA_ARGS:]

    @pl.when(pl.program_id(1) == 0)
    def _():
        h_sc[...] = jnp.zeros_like(h_sc)
        tail_sc[...] = jnp.zeros_like(tail_sc)
        st_sc[...] = jnp.zeros_like(st_sc)

    lru = _lru_tile(lru_sc, *lru_in, y_lru_ref, h_sc, tail_sc)
    gla = _gla_tile(gla_sc, *gla_in, y_gla_ref, st_sc)
    proj = [_project(bb, *proj_in, lru_sc, gla_sc, mk_ref, mqv_ref) for bb in range(len(lru))]

    def sequence(bb):
        yield from proj[bb]
        yield from _alternate(lru[bb], gla[bb])

    _round_robin([_delayed(sequence(bb), bb * SEQ_STAGGER) for bb in range(len(lru))])


def _front(x, g, wn, wt, lru_args, gla_args, ts, nbat):
    assert len(lru_args) == N_LRU_ARGS and len(gla_args) == N_GLA_ARGS
    b, s, d = x.shape
    tile = lambda w: pl.BlockSpec((nbat, ts, w), lambda i, j: (i, j, 0))
    consts = (g, wn, wt) + tuple(lru_args) + tuple(gla_args)
    return pl.pallas_call(
        _front_kernel,
        grid=(b // nbat, s // ts),
        in_specs=[tile(d)] + [_const_spec(a.shape) for a in consts],
        out_specs=[tile(MOBA_W),
                   pl.BlockSpec((nbat, 2 * MOBA_W, ts), lambda i, j: (i, 0, j)),
                   tile(LRU_W), tile(GLA_V_W)],
        out_shape=[jax.ShapeDtypeStruct((b, s, MOBA_W), BF16),
                   jax.ShapeDtypeStruct((b, 2 * MOBA_W, s), BF16),
                   jax.ShapeDtypeStruct((b, s, LRU_W), BF16),
                   jax.ShapeDtypeStruct((b, s, GLA_V_W), BF16)],
        scratch_shapes=[pltpu.VMEM((nbat, ts, 2 * LRU_W), F32),
                        pltpu.VMEM((nbat, ts, GLA_IN_W), F32),
                        pltpu.VMEM((nbat, 1, LRU_W), F32),
                        pltpu.VMEM((nbat, SUBLANES, LRU_W), F32),
                        pltpu.VMEM((nbat, GLA_V_W, GLA_QK_W), F32)],
        compiler_params=_params("parallel", "arbitrary"),
        name="front",
    )(x, *consts)


def _t5_bucket_np(rel):
    n = np.maximum(rel, 0)
    max_exact = REL_BUCKETS // 2
    nf = np.maximum(n, 1).astype(np.float32)
    large = max_exact + (np.log(nf / np.float32(max_exact)) / np.float32(math.log(REL_MAX_DIST / max_exact))
                         * np.float32(REL_BUCKETS - max_exact)).astype(np.int32)
    large = np.minimum(large, REL_BUCKETS - 1)
    return np.where(n < max_exact, n, large).astype(np.int32)


FAR_DIST = int(np.argmax(_t5_bucket_np(np.arange(4 * REL_MAX_DIST)) == REL_BUCKETS - 1))
assert np.all(_t5_bucket_np(np.arange(FAR_DIST, 1 << 16)) == REL_BUCKETS - 1)
assert FAR_DIST <= MOBA_BLOCK

N_TABLES = 2
V_AUG = MOBA_DH + 16
GATE_COLS = 1024


def _bias_kernel(bkt_ref, rb_ref, out_ref):
    h = pl.program_id(0)
    for t in range(2):
        bkt = bkt_ref[t]
        acc = jnp.zeros(bkt.shape, F32)
        for bb in range(REL_BUCKETS):
            acc = jnp.where(bkt == bb, rb_ref[bb, h], acc)
        out_ref[0, t] = jnp.where(bkt < 0, NEG_INF, acc * LOG2E)


def _bias_tables(rel_bias):
    ko = np.arange(MOBA_BLOCK)[:, None]
    qo = np.arange(MOBA_BLOCK)[None, :]
    own = np.where(qo >= ko, _t5_bucket_np(qo - ko), -1)
    bkt = np.stack([own, _t5_bucket_np(qo - ko + MOBA_BLOCK)]).astype(np.int32)
    return pl.pallas_call(
        _bias_kernel,
        grid=(MOBA_HEADS,),
        in_specs=[
            _const_spec((2, MOBA_BLOCK, MOBA_BLOCK)),
            pl.BlockSpec(memory_space=pltpu.SMEM),
        ],
        out_specs=pl.BlockSpec((1, N_TABLES, MOBA_BLOCK, MOBA_BLOCK), lambda h: (h, 0, 0, 0)),
        out_shape=jax.ShapeDtypeStruct((MOBA_HEADS, N_TABLES, MOBA_BLOCK, MOBA_BLOCK), F32),
        compiler_params=_params("parallel"),
        name="moba_bias",
    )(jnp.asarray(bkt), rel_bias.astype(F32))


NEAR_PIPE = 12
FAR_PIPE = 28
LAG = 5


def _pipe_trips(n_tiles, pipe):
    return -(-(n_tiles + LAG - pipe) // pipe)


def _moba_schedules(nb):
    near = [(n, n, n, 0) for n in range(nb)] + [(n, n - 1, n - 1, 1) for n in range(1, nb)]
    far = [(n, j, j, 0) for j in range(nb - 2) for n in range(j + 2, nb)]

    def table(tiles, pipe):
        length = pipe + pipe * _pipe_trips(len(tiles), pipe)
        rows = tiles + [(0, 0, nb, 0)] * (length - len(tiles))
        return np.asarray(rows, np.int32).T.copy(), len(tiles)

    return table(near, NEAR_PIPE), table(far, FAR_PIPE)


def _moba_kernel(near_ref, far_ref, farb_ref, q_ref, v_ref, k_ref, bias_ref, y_ref,
                 vaug_sc, mb_sc, qz_sc, s_sc, cmax_sc, m_sc, acc_sc, *, n_near, n_far):
    hp = pl.program_id(0)
    blk = MOBA_BLOCK
    seq = k_ref.shape[1]
    nb = seq // blk
    heads = range(HEADS_PER_STEP)

    ones_row = lax.broadcasted_iota(jnp.int32, (V_AUG - MOBA_DH, seq), 0) == 0
    kmean = jnp.concatenate(
        [jnp.mean(k_ref[0, jb * blk:(jb + 1) * blk, :].astype(F32), axis=0, keepdims=True)
         for jb in range(nb)], axis=0)
    lane_head = lax.broadcasted_iota(jnp.int32, (nb, LANES), 1) // MOBA_DH
    for hh in heads:
        vaug_sc[hh, :MOBA_DH, :] = v_ref[0, hh * MOBA_DH:(hh + 1) * MOBA_DH, :]
        vaug_sc[hh, MOBA_DH:, :] = jnp.where(ones_row, 1.0, 0.0).astype(BF16)
        mb_sc[hh, nb:, :] = jnp.full((mb_sc.shape[1] - nb, seq), NEG_INF, F32)
        m_sc[hh] = jnp.full(m_sc.shape[1:], NEG_INF, F32)
        acc_sc[hh] = jnp.zeros(acc_sc.shape[1:], F32)
        km_hi, km_lo = _split_hi_lo(jnp.where(lane_head == hh, kmean, 0.0))
        far_bias = farb_ref[hp * HEADS_PER_STEP + hh] * LOG2E
        for c0 in range(0, seq, GATE_COLS):
            cw = min(GATE_COLS, seq - c0)
            qc = q_ref[0, :, c0:c0 + cw]
            g = (jnp.dot(km_hi, qc, preferred_element_type=F32)
                 + jnp.dot(km_lo, qc, preferred_element_type=F32))
            key_blk = lax.broadcasted_iota(jnp.int32, (nb, cw), 0)
            qry_blk = (lax.broadcasted_iota(jnp.int32, (nb, cw), 1) + c0) // blk
            g = jnp.where(key_blk < qry_blk, g, -jnp.inf)
            key_f = key_blk.astype(F32)
            picked = jnp.zeros((nb, cw), F32)
            for _ in range(MOBA_TOPK):
                best = jnp.max(g, axis=0, keepdims=True)
                first = jnp.min(jnp.where(g == best, key_f, float(nb)), axis=0, keepdims=True)
                found = jnp.where(best > -jnp.inf, 1.0, 0.0)
                picked = picked + jnp.where(key_f == first, found, 0.0)
                g = jnp.where(key_f == first, -jnp.inf, g)
            mb_sc[hh, :nb, c0:c0 + cw] = jnp.where(
                picked > 0.0, jnp.where(key_blk < qry_blk - 1, far_bias, 0.0),
                jnp.where(key_blk == qry_blk, 0.0, NEG_INF))

    feat = lax.broadcasted_iota(jnp.int32, (LANES, blk), 0) // MOBA_DH
    for n in range(nb):
        qb = q_ref[0, :, n * blk:(n + 1) * blk]
        for hh in heads:
            c0 = (n * HEADS_PER_STEP + hh) * blk
            qz_sc[:, c0:c0 + blk] = jnp.where(feat == hh, qb, jnp.zeros_like(qb))

    def run_pass(sched_ref, n_tiles, with_table, pipe):
        def col(idx, r):
            return pl.multiple_of(sched_ref[r, idx] * blk, blk)

        def logits(idx, slot):
            qcols = HEADS_PER_STEP * blk
            qz = qz_sc[:, pl.ds(pl.multiple_of(sched_ref[0, idx] * qcols, qcols), qcols)]
            s = jnp.dot(k_ref[0, pl.ds(col(idx, 1), blk), :], qz, preferred_element_type=F32)
            if with_table:
                tab = sched_ref[3, idx]
                s = s + jnp.concatenate([bias_ref[hh, tab] for hh in heads], axis=1)
            s_sc[slot] = s
            cmax_sc[slot] = jnp.max(s.reshape(blk // SUBLANES, SUBLANES, s.shape[1]), axis=0)

        def softmax_values(idx, slot):
            n = sched_ref[0, idx]
            qcol = col(idx, 0)
            kcol = col(idx, 1)
            row = sched_ref[2, idx]
            for hh in heads:
                cols = slice(hh * blk, (hh + 1) * blk)
                cm = jnp.max(cmax_sc[slot, :, cols], axis=0, keepdims=True)
                mb = mb_sc[hh, pl.ds(row, 1), pl.ds(qcol, blk)]
                m_old = m_sc[hh, n]
                m_new = jnp.maximum(m_old, cm + mb)
                m_sc[hh, n] = m_new
                p = jnp.exp2(s_sc[slot, :, cols] + (mb - m_new)).astype(BF16)
                pv = jnp.dot(vaug_sc[hh, :, pl.ds(kcol, blk)], p, preferred_element_type=F32)
                acc_sc[hh, n] = jnp.exp2(m_old - m_new) * acc_sc[hh, n] + pv

        for i in range(pipe):
            logits(i, i)
            if i >= LAG:
                softmax_values(i - LAG, (i - LAG) % pipe)

        def body(d, carry):
            for u in range(pipe):
                i = pipe + pipe * d + u
                logits(i, u)
                softmax_values(i - LAG, (u - LAG) % pipe)
            return carry

        lax.fori_loop(0, _pipe_trips(n_tiles, pipe), body, 0)

    run_pass(near_ref, n_near, True, NEAR_PIPE)
    run_pass(far_ref, n_far, False, FAR_PIPE)

    for n in range(nb):
        out_t = jnp.concatenate(
            [acc_sc[hh, n, :MOBA_DH, :] * (1.0 / acc_sc[hh, n, MOBA_DH:MOBA_DH + 1, :])
             for hh in heads], axis=0)
        y_ref[0, n * blk:(n + 1) * blk, :] = out_t.T.astype(y_ref.dtype)


def _moba(mqv, mk, bias_tab, far_bias):
    b, s, _ = mk.shape
    blk = MOBA_BLOCK
    nb = s // blk
    npair = MOBA_HEADS // HEADS_PER_STEP
    (near, n_near), (far, n_far) = _moba_schedules(nb)
    smem = pl.BlockSpec(memory_space=pltpu.SMEM)
    slots = max(NEAR_PIPE, FAR_PIPE)
    return pl.pallas_call(
        functools.partial(_moba_kernel, n_near=n_near, n_far=n_far),
        grid=(npair, b),
        in_specs=[
            smem, smem, smem,
            pl.BlockSpec((1, LANES, s), lambda p, i: (i, p, 0)),
            pl.BlockSpec((1, LANES, s), lambda p, i: (i, npair + p, 0)),
            pl.BlockSpec((1, s, LANES), lambda p, i: (i, 0, p)),
            pl.BlockSpec((HEADS_PER_STEP, N_TABLES, blk, blk), lambda p, i: (p, 0, 0, 0)),
        ],
        out_specs=pl.BlockSpec((1, s, LANES), lambda p, i: (i, 0, p)),
        out_shape=jax.ShapeDtypeStruct((b, s, MOBA_W), BF16),
        scratch_shapes=[
            pltpu.VMEM((HEADS_PER_STEP, V_AUG, s), BF16),
            pltpu.VMEM((HEADS_PER_STEP, nb + SUBLANES, s), F32),
            pltpu.VMEM((LANES, HEADS_PER_STEP * s), BF16),
            pltpu.VMEM((slots, blk, HEADS_PER_STEP * blk), F32),
            pltpu.VMEM((slots, SUBLANES, HEADS_PER_STEP * blk), F32),
            pltpu.VMEM((HEADS_PER_STEP, nb, 1, blk), F32),
            pltpu.VMEM((HEADS_PER_STEP, nb, V_AUG, blk), F32),
        ],
        compiler_params=_params("parallel", "parallel"),
        name="moba",
    )(jnp.asarray(near), jnp.asarray(far), far_bias, mqv, mqv, mk, bias_tab)


def _mix_ffn_kernel(x_ref, lru_ref, gla_ref, moba_ref, wl_ref, wgl_ref, wm_ref, gmix_ref,
                    gpre_ref, wg_ref, wu_ref, wd_ref, gpost_ref, o_ref):
    rows_per = x_ref.shape[0] // ROW_CHAINS

    def chain(c):
        rows = slice(c * rows_per, (c + 1) * rows_per)
        y = jnp.dot(lru_ref[rows], wl_ref[...], preferred_element_type=F32)
        y = y + jnp.dot(gla_ref[rows], wgl_ref[...], preferred_element_type=F32)
        y = y + jnp.dot(moba_ref[rows], wm_ref[...], preferred_element_type=F32)
        yield
        x = x_ref[rows] + _rms(y, gmix_ref[...])
        h = _rms(x, gpre_ref[...]).astype(BF16)
        yield
        a = jax.nn.silu(jnp.dot(h, wg_ref[...], preferred_element_type=F32))
        yield
        a = (a * jnp.dot(h, wu_ref[...], preferred_element_type=F32)).astype(BF16)
        yield
        f = jnp.dot(a, wd_ref[...], preferred_element_type=F32)
        yield
        o_ref[rows] = x + _rms(f, gpost_ref[...])

    _round_robin([chain(c) for c in range(ROW_CHAINS)])


def _mix_ffn(x, y_lru, y_gla, y_moba, w_out, gmix, gpre, wg, wu, wd, gpost, tt):
    t, d = x.shape
    wl = w_out[:LRU_W]
    wgl = w_out[LRU_W:LRU_W + GLA_V_W]
    wm = w_out[LRU_W + GLA_V_W:]
    row = lambda w: pl.BlockSpec((tt, w), lambda i: (i, 0))
    once = lambda shape: pl.BlockSpec(shape, lambda i: (0, 0), pipeline_mode=pl.Buffered(1))
    return pl.pallas_call(
        _mix_ffn_kernel,
        grid=(t // tt,),
        in_specs=[row(d), row(LRU_W), row(GLA_V_W), row(MOBA_W),
                  once(wl.shape), once(wgl.shape), once(wm.shape), _const_spec((1, d)),
                  _const_spec((1, d)), once(wg.shape), once(wu.shape), once(wd.shape),
                  _const_spec((1, d))],
        out_specs=row(d),
        out_shape=jax.ShapeDtypeStruct((t, d), F32),
        compiler_params=_params("parallel"),
        name="mix_ffn",
    )(x, y_lru, y_gla, y_moba, wl, wgl, wm, gmix, gpre, wg, wu, wd, gpost)


def _block_diag(w):
    g, n, _ = w.shape
    eye = jnp.eye(g, dtype=w.dtype)
    return (eye[:, None, :, None] * w[:, :, None, :]).reshape(g * n, g * n)


def _split_w_in(w_in):
    sizes = (LRU_W, LRU_W, GLA_QK_W, GLA_QK_W, GLA_V_W, GLA_RANK, GLA_V_W, MOBA_W, MOBA_W, MOBA_W)
    offs = np.cumsum((0,) + sizes)
    col = lambda i: w_in[:, offs[i]:offs[i + 1]]
    lru_x, lru_g, gq, gk, gv, g_lr, g_out, mq, mk, mv = (col(i) for i in range(len(sizes)))
    g_lr = jnp.pad(g_lr, ((0, 0), (0, LANES - GLA_RANK)))
    w_nat = jnp.concatenate([lru_x, lru_g, gq, gk, gv, g_out, g_lr, mk], axis=1).astype(BF16)
    w_t = jnp.concatenate([mq, mv], axis=1).T.astype(BF16)
    return w_nat, w_t


def kernel(x, pre_mix_norm, post_mix_norm, pre_ffn_norm, post_ffn_norm, w_in, w_out, lru_conv_w, lru_conv_b, lru_wa, lru_ba, lru_wx, lru_bx, lru_lambda, gla_gate_w2, gla_gate_b, gla_norm, rel_bias, w_ffn_gate, w_ffn_up, w_ffn_down):
    b, s, d = x.shape
    assert s % MOBA_BLOCK == 0 and d == LRU_W + GLA_V_W + MOBA_W
    assert lru_wa.shape[1] == LRU_BLOCKS and (b * s) % TOKEN_TILE == 0
    depth = w_in.shape[0]
    ts_seq = SEQ_TILE
    nbat = next(n for n in (8, 4, 2, 1) if b % n == 0)
    tt = TOKEN_TILE
    row = lambda v: v.reshape(1, -1).astype(F32)

    bias_tab = _bias_tables(rel_bias)
    for l in range(depth):
        w_nat, w_t = _split_w_in(w_in[l])
        w2 = jnp.pad(gla_gate_w2[l], ((0, LANES - GLA_RANK), (0, 0))).astype(BF16)
        mk, mqv, y_lru, y_gla = _front(
            x, row(pre_mix_norm[l]), w_nat, w_t,
            (lru_conv_w[l].astype(F32), row(lru_conv_b[l]),
             _block_diag(lru_wa[l]).astype(BF16), row(lru_ba[l]),
             _block_diag(lru_wx[l]).astype(BF16), row(lru_bx[l]), row(lru_lambda[l])),
            (w2, row(gla_gate_b[l]), row(gla_norm[l])), ts_seq, nbat)
        y_moba = _moba(mqv, mk, bias_tab, rel_bias[REL_BUCKETS - 1].astype(F32))
        x2 = _mix_ffn(x.reshape(b * s, d), y_lru.reshape(b * s, -1), y_gla.reshape(b * s, -1),
                      y_moba.reshape(b * s, -1), w_out[l].astype(BF16), row(post_mix_norm[l]),
                      row(pre_ffn_norm[l]), w_ffn_gate[l].astype(BF16), w_ffn_up[l].astype(BF16),
                      w_ffn_down[l].astype(BF16), row(post_ffn_norm[l]), tt)
        x = x2.reshape(b, s, d)
    return x
```
